```python
import math
import jax, jax.numpy as jnp
from jax import lax
import numpy as np

D_MODEL = 1024
BATCH = 2
SEQ = 8192
DEPTH = 4
DEC_BATCH = 32
DEC_SEQ = 8
PAST_LEN = 8192
PAGE_SIZE = 128

HEAD_DIM = 64
N_GDN_HEADS = 8
N_DIFF_HEADS = 4
DIFF_V_DIM = 2 * HEAD_DIM
GDN_WIDTH = N_GDN_HEADS * HEAD_DIM
DIFF_WIDTH = N_DIFF_HEADS * DIFF_V_DIM
MIX_WIDTH = GDN_WIDTH + DIFF_WIDTH
CONV_W = 4
CONV_DIM = 3 * GDN_WIDTH
GDN_CHUNK = 64
D_FF = 2816
N_BUCKETS = 32
MAX_DISTANCE = 128
Q_BLOCK = 128
RMS_EPS = 1e-6
IN_SPLITS = (CONV_DIM, GDN_WIDTH, N_GDN_HEADS, N_GDN_HEADS, DIFF_WIDTH, DIFF_WIDTH, DIFF_WIDTH)
IN_COLS = sum(IN_SPLITS)

kernel_name = "hymba_gdn_diffattn_macaron_step"


def rms_norm(x, g, eps=RMS_EPS):
    xf = x.astype(jnp.float32)
    y = xf * lax.rsqrt(jnp.mean(xf * xf, axis=-1, keepdims=True) + eps)
    return (y * g.astype(jnp.float32)).astype(x.dtype)


def l2_norm(x, eps=1e-6):
    return x * lax.rsqrt(jnp.sum(x * x, axis=-1, keepdims=True) + eps)


def swiglu_ffn(x, w_gate_up, w_down):
    gate, up = jnp.split(x @ w_gate_up, 2, axis=-1)
    return (jax.nn.silu(gate) * up) @ w_down


def causal_conv(x, buf, w):
    xp = jnp.concatenate([buf.astype(x.dtype), x], axis=1)
    L = x.shape[1]
    y = sum(xp[:, i:i + L] * w[i] for i in range(CONV_W))
    return jax.nn.silu(y), xp[:, -(CONV_W - 1):]


def gated_delta_chunked(q, k, v, g, beta, s0):
    B, L, H, dk = q.shape
    dv = v.shape[-1]
    C = min(GDN_CHUNK, L)
    pad = (-L) % C
    if pad:
        padf = lambda a: jnp.pad(a, [(0, 0), (0, pad)] + [(0, 0)] * (a.ndim - 2))
        q, k, v, g, beta = (padf(a) for a in (q, k, v, g, beta))
    NC = (L + pad) // C
    chunk4 = lambda a: a.transpose(0, 2, 1, 3).reshape(B, H, NC, C, a.shape[-1])
    q = chunk4(q) * dk ** -0.5
    k = chunk4(k)
    v = chunk4(v)
    g = jnp.cumsum(g.transpose(0, 2, 1).reshape(B, H, NC, C), axis=-1)
    beta = beta.transpose(0, 2, 1).reshape(B, H, NC, C)
    kb = k * beta[..., None]
    vb = v * beta[..., None]
    causal = jnp.tril(jnp.ones((C, C), bool))
    strict = jnp.tril(jnp.ones((C, C), bool), -1)
    decay = jnp.exp(jnp.where(causal, g[..., :, None] - g[..., None, :], -jnp.inf))
    a = jnp.where(strict, jnp.einsum('bhnid,bhnjd->bhnij', kb, k) * decay, 0.0)
    eye = jnp.eye(C, dtype=a.dtype)
    rhs = jnp.concatenate([vb, kb * jnp.exp(g)[..., None]], axis=-1)
    sol = lax.linalg.triangular_solve(eye + a, rhs, left_side=True, lower=True, unit_diagonal=True)
    u, w = sol[..., :dv], sol[..., dv:]
    qk = jnp.where(causal, jnp.einsum('bhnid,bhnjd->bhnij', q, k) * decay, 0.0)
    g_last = g[..., -1]
    k_to_end = k * jnp.exp(g_last[..., None] - g)[..., None]

    def step(s, xs):
        q_c, qk_c, u_c, w_c, kend_c, g_c, glast_c = xs
        v_new = u_c - jnp.einsum('bhcd,bhde->bhce', w_c, s)
        o = (jnp.einsum('bhcd,bhde->bhce', q_c * jnp.exp(g_c)[..., None], s)
             + jnp.einsum('bhij,bhje->bhie', qk_c, v_new))
        s = s * jnp.exp(glast_c)[..., None, None] + jnp.einsum('bhcd,bhce->bhde', kend_c, v_new)
        return s, o

    xs = tuple(jnp.moveaxis(t, 2, 0) for t in (q, qk, u, w, k_to_end, g, g_last))
    s_final, o = lax.scan(step, s0, xs)
    o = jnp.moveaxis(o, 0, 2).reshape(B, H, NC * C, dv)[:, :, :L].transpose(0, 2, 1, 3)
    return o, s_final


def gdn_mixer(qkv_pre, gate, b_logit, a_logit, conv_buf, s0, conv_w, a_log, dt_bias, out_gain):
    B, L, _ = qkv_pre.shape
    f32 = jnp.float32
    qkv, conv_state = causal_conv(qkv_pre, conv_buf, conv_w)
    q, k, v = jnp.split(qkv.astype(f32), 3, axis=-1)
    shp = (B, L, N_GDN_HEADS, HEAD_DIM)
    q = l2_norm(q.reshape(shp))
    k = l2_norm(k.reshape(shp))
    v = v.reshape(shp)
    beta = jax.nn.sigmoid(b_logit.astype(f32))
    g = -jnp.exp(a_log.astype(f32)) * jax.nn.softplus(a_logit.astype(f32) + dt_bias.astype(f32))
    o, s = gated_delta_chunked(q, k, v, g, beta, s0.astype(f32))
    o = rms_norm(o, out_gain) * jax.nn.silu(gate.astype(f32).reshape(shp))
    return o.reshape(B, L, GDN_WIDTH).astype(qkv_pre.dtype), s.astype(qkv_pre.dtype), conv_state


def rel_bucket(rel):
    n = jnp.maximum(rel, 0)
    max_exact = N_BUCKETS // 2
    nf = jnp.maximum(n, 1).astype(jnp.float32)
    large = max_exact + (jnp.log(nf / max_exact) / math.log(MAX_DISTANCE / max_exact)
                         * (N_BUCKETS - max_exact)).astype(jnp.int32)
    large = jnp.minimum(large, N_BUCKETS - 1)
    return jnp.where(n < max_exact, n, large)


def diff_attend(q, k, v, q_pos, k_pos, lam, rel_bias):
    rel = q_pos[:, None] - k_pos[None, :]
    bias = jnp.take(rel_bias.astype(jnp.float32), rel_bucket(rel), axis=0)
    bias = jnp.where((rel >= 0)[..., None], bias, -jnp.inf).transpose(2, 0, 1)
    s = jnp.einsum('bqhmd,bkhmd->bhmqk', q, k).astype(jnp.float32) + bias[None, :, None]
    p = jax.nn.softmax(s, axis=-1)
    attn = p[:, :, 0] - lam * p[:, :, 1]
    return jnp.einsum('bhqk,bkhe->bqhe', attn.astype(v.dtype), v)


def diff_mixer(dq, dk, dv, past_k, past_v, q_pos, layer_idx, q_gain, k_gain, lam_params, out_gain, rel_bias):
    B, L = dq.shape[:2]
    q = rms_norm(dq.reshape(B, L, N_DIFF_HEADS, 2, HEAD_DIM), q_gain) * HEAD_DIM ** -0.5
    k = rms_norm(dk.reshape(B, L, N_DIFF_HEADS, 2, HEAD_DIM), k_gain)
    v = dv.reshape(B, L, N_DIFF_HEADS, DIFF_V_DIM)
    k_rows = k.reshape(B, L, N_DIFF_HEADS, 2 * HEAD_DIM)
    if past_k is None:
        k_all, v_all = k, v
    else:
        k_all = jnp.concatenate([past_k.reshape(B, -1, N_DIFF_HEADS, 2, HEAD_DIM).astype(k.dtype), k], axis=1)
        v_all = jnp.concatenate([past_v.astype(v.dtype), v], axis=1)
    k_pos = jnp.arange(k_all.shape[1], dtype=jnp.int32)
    lam_init = 0.8 - 0.6 * math.exp(-0.3 * layer_idx)
    lp = lam_params.astype(jnp.float32)
    lam = jnp.exp(jnp.sum(lp[0] * lp[1])) - jnp.exp(jnp.sum(lp[2] * lp[3])) + lam_init
    if L % Q_BLOCK == 0:
        nb = L // Q_BLOCK
        qb = jnp.moveaxis(q.reshape(B, nb, Q_BLOCK, N_DIFF_HEADS, 2, HEAD_DIM), 1, 0)
        pb = q_pos.reshape(nb, Q_BLOCK)
        o = lax.map(lambda a: diff_attend(a[0], k_all, v_all, a[1], k_pos, lam, rel_bias), (qb, pb))
        o = jnp.moveaxis(o, 0, 1).reshape(B, L, N_DIFF_HEADS, DIFF_V_DIM)
    else:
        o = diff_attend(q, k_all, v_all, q_pos, k_pos, lam, rel_bias)
    o = rms_norm(o, out_gain) * (1.0 - lam_init)
    return o.reshape(B, L, DIFF_WIDTH), k_rows, v


def decoder_layer(x, l, conv_buf, s0, past_k, past_v, q_pos, p):
    B, L, _ = x.shape
    x = x + 0.5 * swiglu_ffn(rms_norm(x, p['ffn1_norm'][l]), p['ffn1_w_gate_up'][l], p['ffn1_w_down'][l])
    h = rms_norm(x, p['mix_norm'][l])
    proj = h @ p['w_in'][l]
    qkv_pre, gate, b_logit, a_logit, dq, dk, dv = jnp.split(proj, list(np.cumsum(IN_SPLITS)[:-1]), axis=-1)
    o_gdn, s_new, conv_new = gdn_mixer(qkv_pre, gate, b_logit, a_logit, conv_buf, s0, p['conv_w'][l],
                                       p['gdn_a_log'][l], p['gdn_dt_bias'][l], p['gdn_out_norm'][l])
    o_diff, k_rows, v_rows = diff_mixer(dq, dk, dv, past_k, past_v, q_pos, l, p['diff_q_norm'][l],
                                        p['diff_k_norm'][l], p['diff_lambda'][l], p['diff_out_norm'][l],
                                        p['rel_bias'])
    x = x + jnp.concatenate([o_gdn, o_diff], axis=-1) @ p['w_out'][l]
    x = x + 0.5 * swiglu_ffn(rms_norm(x, p['ffn2_norm'][l]), p['ffn2_w_gate_up'][l], p['ffn2_w_down'][l])
    return x, k_rows, v_rows, s_new, conv_new


def run_trunk(x, cache_k, cache_v, state_gdn, state_conv, page_table, p):
    B, L, _ = x.shape
    past_len = 0 if page_table is None else page_table.shape[1] * PAGE_SIZE
    q_pos = past_len + jnp.arange(L, dtype=jnp.int32)
    ks, vs, ss, cs = [], [], [], []
    for l in range(DEPTH):
        if page_table is None:
            conv_buf = jnp.zeros((B, CONV_W - 1, CONV_DIM), x.dtype)
            s0 = jnp.zeros((B, N_GDN_HEADS, HEAD_DIM, HEAD_DIM), jnp.float32)
            past_k = past_v = None
        else:
            conv_buf, s0 = state_conv[l], state_gdn[l]
            past_k = cache_k[l][page_table].reshape(B, past_len, N_DIFF_HEADS, 2 * HEAD_DIM)
            past_v = cache_v[l][page_table].reshape(B, past_len, N_DIFF_HEADS, DIFF_V_DIM)
        x, k_rows, v_rows, s_new, conv_new = decoder_layer(x, l, conv_buf, s0, past_k, past_v, q_pos, p)
        ks.append(k_rows); vs.append(v_rows); ss.append(s_new); cs.append(conv_new)
    return x, jnp.stack(ks), jnp.stack(vs), jnp.stack(ss), jnp.stack(cs)


def setup_inputs(seed: int = 0) -> dict:
    key = jax.random.key(seed)
    ks = jax.random.split(key, 32)
    f32 = jnp.float32
    n_pages = PAST_LEN // PAGE_SIZE
    used = DEC_BATCH * n_pages
    n_pool = used + max(1, used // 4)
    nrm = lambda k, shape, scale: jax.random.normal(k, shape, f32) * scale
    gain = lambda k, shape: 1.0 + 0.02 * jax.random.normal(k, shape, f32)
    dt = jnp.exp(jax.random.uniform(ks[20], (DEPTH, N_GDN_HEADS), f32, math.log(1e-3), math.log(1e-1)))
    page_table = jax.random.permutation(ks[6], n_pool)[:used].reshape(DEC_BATCH, n_pages).astype(jnp.int32)
    return {
        "x_prompt": nrm(ks[0], (BATCH, SEQ, D_MODEL), 1.0),
        "x_sample": nrm(ks[1], (DEC_BATCH, DEC_SEQ, D_MODEL), 1.0),
        "cache_k": nrm(ks[2], (DEPTH, n_pool, PAGE_SIZE, N_DIFF_HEADS, 2 * HEAD_DIM), 1.0),
        "cache_v": nrm(ks[3], (DEPTH, n_pool, PAGE_SIZE, N_DIFF_HEADS, DIFF_V_DIM), 1.0),
        "state_gdn": nrm(ks[4], (DEPTH, DEC_BATCH, N_GDN_HEADS, HEAD_DIM, HEAD_DIM), 0.1),
        "state_conv": nrm(ks[5], (DEPTH, DEC_BATCH, CONV_W - 1, CONV_DIM), 1.0),
        "page_table": page_table,
        "ffn1_norm": gain(ks[7], (DEPTH, D_MODEL)),
        "ffn1_w_gate_up": nrm(ks[8], (DEPTH, D_MODEL, 2 * D_FF), D_MODEL ** -0.5),
        "ffn1_w_down": nrm(ks[9], (DEPTH, D_FF, D_MODEL), D_FF ** -0.5),
        "mix_norm": gain(ks[10], (DEPTH, D_MODEL)),
        "w_in": nrm(ks[11], (DEPTH, D_MODEL, IN_COLS), D_MODEL ** -0.5),
        "conv_w": nrm(ks[12], (DEPTH, CONV_W, CONV_DIM), CONV_W ** -0.5),
        "gdn_a_log": jnp.log(jax.random.uniform(ks[13], (DEPTH, N_GDN_HEADS), f32, 1.0, 16.0)),
        "gdn_dt_bias": dt + jnp.log(-jnp.expm1(-dt)),
        "gdn_out_norm": gain(ks[14], (DEPTH, HEAD_DIM)),
        "diff_q_norm": gain(ks[15], (DEPTH, HEAD_DIM)),
        "diff_k_norm": gain(ks[16], (DEPTH, HEAD_DIM)),
        "diff_lambda": nrm(ks[17], (DEPTH, 4, HEAD_DIM), 0.1),
        "diff_out_norm": gain(ks[18], (DEPTH, DIFF_V_DIM)),
        "rel_bias": nrm(ks[19], (N_BUCKETS, N_DIFF_HEADS), 0.5),
        "w_out": nrm(ks[21], (DEPTH, MIX_WIDTH, D_MODEL), MIX_WIDTH ** -0.5),
        "ffn2_norm": gain(ks[22], (DEPTH, D_MODEL)),
        "ffn2_w_gate_up": nrm(ks[23], (DEPTH, D_MODEL, 2 * D_FF), D_MODEL ** -0.5),
        "ffn2_w_down": nrm(ks[24], (DEPTH, D_FF, D_MODEL), D_FF ** -0.5),
    }


def reference(x_prompt, x_sample, cache_k, cache_v, state_gdn, state_conv, page_table,
              ffn1_norm, ffn1_w_gate_up, ffn1_w_down, mix_norm, w_in, conv_w, gdn_a_log, gdn_dt_bias,
              gdn_out_norm, diff_q_norm, diff_k_norm, diff_lambda, diff_out_norm, rel_bias, w_out,
              ffn2_norm, ffn2_w_gate_up, ffn2_w_down):
    p = dict(ffn1_norm=ffn1_norm, ffn1_w_gate_up=ffn1_w_gate_up, ffn1_w_down=ffn1_w_down,
             mix_norm=mix_norm, w_in=w_in, conv_w=conv_w, gdn_a_log=gdn_a_log, gdn_dt_bias=gdn_dt_bias,
             gdn_out_norm=gdn_out_norm, diff_q_norm=diff_q_norm, diff_k_norm=diff_k_norm,
             diff_lambda=diff_lambda, diff_out_norm=diff_out_norm, rel_bias=rel_bias, w_out=w_out,
             ffn2_norm=ffn2_norm, ffn2_w_gate_up=ffn2_w_gate_up, ffn2_w_down=ffn2_w_down)
    y_prompt, k_p, v_p, s_p, c_p = run_trunk(x_prompt, None, None, None, None, None, p)
    y_sample, k_s, v_s, s_s, c_s = run_trunk(x_sample, cache_k, cache_v, state_gdn, state_conv, page_table, p)
    return (y_prompt, y_sample, k_p, v_p, s_p, c_p, k_s, v_s, s_s, c_s)
```

```python
import functools
import math

import jax
import jax.numpy as jnp
from jax import lax
from jax.experimental import pallas as pl
from jax.experimental.pallas import tpu as pltpu

F32 = jnp.float32
BF16 = jnp.bfloat16

HEAD_DIM = 64
GDN_CHUNK = 64
MAX_DISTANCE = 128
RMS_EPS = 1e-6
L2_EPS = 1e-6
LANES = 128
VMEM_LIMIT = 52 * 1024 * 1024
NEG_BIG = -1e30


def _dot(a, b):
    return jnp.dot(a, b, preferred_element_type=F32)


def _dot_nt(a, b):
    return lax.dot_general(a, b, (((1,), (1,)), ((), ())), preferred_element_type=F32)


def _dot_tn(a, b):
    return lax.dot_general(a, b, (((0,), (0,)), ((), ())), preferred_element_type=F32)


def _split3(x):
    hi = x.astype(BF16)
    r = x - hi.astype(F32)
    mid = r.astype(BF16)
    lo = (r - mid.astype(F32)).astype(BF16)
    return hi, mid, lo


def _dot_x3(a, b):
    a0, a1, a2 = _split3(a)
    b0, b1, b2 = _split3(b)
    return (_dot(a0, b0) + (_dot(a0, b1) + _dot(a1, b0))
            + (_dot(a0, b2) + _dot(a1, b1) + _dot(a2, b0)))


def _dot_exact_lhs(a_bf, b):
    b0, b1, b2 = _split3(b)
    return _dot(a_bf, b0) + _dot(a_bf, b1) + _dot(a_bf, b2)


def _group_sumsq(x, gmat):
    sq = x * x
    hi = sq.astype(BF16)
    lo = (sq - hi.astype(F32)).astype(BF16)
    return _dot(hi, gmat) + _dot(lo, gmat)


def _group_matrix(width):
    r = lax.broadcasted_iota(jnp.int32, (width, width), 0) // HEAD_DIM
    c = lax.broadcasted_iota(jnp.int32, (width, width), 1) // HEAD_DIM
    return jnp.where(r == c, 1.0, 0.0).astype(BF16)


def _ffn_kernel(has_mix, n_f, *refs):
    if has_mix:
        (x_ref, og_ref, od_ref, wo_ref, nw_ref, wg_ref, wu_ref, wd_ref,
         o_ref, h_s, acc_s, x_s) = refs
    else:
        x_ref, nw_ref, wg_ref, wu_ref, wd_ref, o_ref, h_s, acc_s, x_s = refs
    f = pl.program_id(1)

    @pl.when(f == 0)
    def _():
        x = x_ref[...]
        if has_mix:
            half = og_ref.shape[-1]
            x = x + _dot(og_ref[...], wo_ref[0:half, :]) + _dot(od_ref[...], wo_ref[half:, :])
        x_s[...] = x
        ms = jnp.mean(x * x, axis=-1, keepdims=True)
        h_s[...] = (x * lax.rsqrt(ms + RMS_EPS) * nw_ref[...]).astype(BF16)
        acc_s[...] = jnp.zeros_like(acc_s)

    h = h_s[...]
    g = _dot(h, wg_ref[...])
    u = _dot(h, wu_ref[...])
    a = (g * jax.nn.sigmoid(g)) * u
    acc_s[...] += _dot(a.astype(BF16), wd_ref[...])

    @pl.when(f == n_f - 1)
    def _():
        o_ref[...] = x_s[...] + 0.5 * acc_s[...]


def _ffn_tiles(m, d_ff):
    tm = 512 if m % 512 == 0 else m
    tf = d_ff // 2 if (d_ff // 2) % LANES == 0 else d_ff
    return tm, tf


def _ffn(x, norm_w, w_gu, w_down, layer, mix=None):
    m, d = x.shape
    d_ff = w_down.shape[1]
    tm, tf = _ffn_tiles(m, d_ff)
    n_f = d_ff // tf
    row = lambda i, f: (i, 0)
    in_specs = [pl.BlockSpec((tm, d), row)]
    args = [x]
    if mix is not None:
        og, od, w_out = mix
        in_specs += [pl.BlockSpec((tm, og.shape[1]), row), pl.BlockSpec((tm, od.shape[1]), row),
                     pl.BlockSpec((None,) + w_out.shape[1:], lambda i, f: (layer, 0, 0))]
        args += [og, od, w_out]
    in_specs += [
        pl.BlockSpec((None, 1, d), lambda i, f: (layer, 0, 0)),
        pl.BlockSpec((None, d, tf), lambda i, f: (layer, 0, f)),
        pl.BlockSpec((None, d, tf), lambda i, f: (layer, 0, f + n_f)),
        pl.BlockSpec((None, tf, d), lambda i, f: (layer, f, 0)),
    ]
    args += [norm_w, w_gu, w_gu, w_down]
    return pl.pallas_call(
        functools.partial(_ffn_kernel, mix is not None, n_f),
        grid=(m // tm, n_f),
        in_specs=in_specs,
        out_specs=pl.BlockSpec((tm, d), row),
        out_shape=jax.ShapeDtypeStruct((m, d), F32),
        scratch_shapes=[pltpu.VMEM((tm, d), BF16), pltpu.VMEM((tm, d), F32), pltpu.VMEM((tm, d), F32)],
        compiler_params=pltpu.CompilerParams(
            dimension_semantics=("parallel", "arbitrary"), vmem_limit_bytes=VMEM_LIMIT),
        name="ffn_mix" if mix is not None else "ffn",
    )(*args)


def _inproj_kernel(widths, x_ref, nw_ref, w_ref, qg_ref, kg_ref,
                   qkv_ref, gate_ref, ba_ref, q_ref, k_ref, v_ref, kb_ref, vb_ref):
    c_qkv, c_gate, c_d, c_ba = widths
    x = x_ref[...]
    ms = jnp.mean(x * x, axis=-1, keepdims=True)
    h = (x * lax.rsqrt(ms + RMS_EPS) * nw_ref[...]).astype(BF16)
    o = 0
    qkv_ref[...] = _dot(h, w_ref[:, o:o + c_qkv]); o += c_qkv
    gate_ref[...] = _dot(h, w_ref[:, o:o + c_gate]); o += c_gate
    dq = _dot(h, w_ref[:, o:o + c_d]); o += c_d
    dk = _dot(h, w_ref[:, o:o + c_d]); o += c_d
    dv = _dot(h, w_ref[:, o:o + c_d]); o += c_d
    ba_ref[...] = _dot(h, w_ref[:, o:o + c_ba])
    gmat = _group_matrix(c_d)
    inv_hd = 1.0 / HEAD_DIM
    qn = dq * lax.rsqrt(_group_sumsq(dq, gmat) * inv_hd + RMS_EPS) * qg_ref[...]
    kn = dk * lax.rsqrt(_group_sumsq(dk, gmat) * inv_hd + RMS_EPS) * kg_ref[...]
    q_ref[...] = qn * (HEAD_DIM ** -0.5)
    k_ref[...] = kn
    v_ref[...] = dv
    kb_ref[...] = kn.astype(BF16)
    vb_ref[...] = dv.astype(BF16)


def _inproj(x, norm_w, w_in, q_gain, k_gain, widths, layer):
    m, d = x.shape
    c_qkv, c_gate, c_d, c_ba = widths
    tm = 512 if m % 512 == 0 else m
    row = lambda i: (i, 0)
    lay3 = lambda i: (layer, 0, 0)
    outs = [(c_qkv, F32), (c_gate, F32), (c_ba, F32), (c_d, F32), (c_d, F32), (c_d, F32), (c_d, BF16), (c_d, BF16)]
    return pl.pallas_call(
        functools.partial(_inproj_kernel, widths),
        grid=(m // tm,),
        in_specs=[pl.BlockSpec((tm, d), row),
                  pl.BlockSpec((None, 1, d), lay3),
                  pl.BlockSpec((None,) + w_in.shape[1:], lay3),
                  pl.BlockSpec((None, 1, c_d), lay3),
                  pl.BlockSpec((None, 1, c_d), lay3)],
        out_specs=[pl.BlockSpec((tm, c), row) for c, _ in outs],
        out_shape=[jax.ShapeDtypeStruct((m, c), dt) for c, dt in outs],
        compiler_params=pltpu.CompilerParams(
            dimension_semantics=("parallel",), vmem_limit_bytes=VMEM_LIMIT),
        name="inproj",
    )(x, norm_w, w_in, q_gain, k_gain)


def _gdn_kernel(n_heads, chunk, qkv_ref, gate_ref, ba_ref, cbuf_ref, s0_ref, cw_ref, alog_ref, dtb_ref,
                gain_ref, o_ref, snew_ref, cnew_ref, xp_s, s_s):
    bb = qkv_ref.shape[0]
    c = chunk
    width = n_heads * HEAD_DIM
    n_tail = cbuf_ref.shape[1]
    n_taps = n_tail + 1
    top = 8 - n_tail
    t = pl.program_id(1)
    n_t = pl.num_programs(1)

    @pl.when(t == 0)
    def _():
        xp_s[:, top:8, :] = cbuf_ref[...]
        s_s[...] = s0_ref[...]

    xp_s[:, 8:8 + c, :] = qkv_ref[...]

    cw = cw_ref[...]
    gmat = _group_matrix(width)
    ri = lax.broadcasted_iota(jnp.int32, (c, c), 0)
    ci = lax.broadcasted_iota(jnp.int32, (c, c), 1)
    causal = ri >= ci
    strict = ri > ci
    tril_bf = jnp.where(causal, 1.0, 0.0).astype(BF16)
    eye = jnp.where(ri == ci, 1.0, 0.0).astype(F32)
    n_double = max(int(math.ceil(math.log2(c))) - 1, 0)
    neg_a = -jnp.exp(alog_ref[...])
    dtb = dtb_ref[...]
    gain = gain_ref[...]

    for b in range(bb):
        y = xp_s[b, top:top + c, :] * cw[0:1, :]
        for i in range(1, n_taps):
            y = y + xp_s[b, top + i:top + i + c, :] * cw[i:i + 1, :]
        qkv = y * jax.nn.sigmoid(y)
        q_all = qkv[:, 0:width]
        k_all = qkv[:, width:2 * width]
        v_all = qkv[:, 2 * width:3 * width]
        q_all = q_all * lax.rsqrt(_group_sumsq(q_all, gmat) + L2_EPS) * (HEAD_DIM ** -0.5)
        k_all = k_all * lax.rsqrt(_group_sumsq(k_all, gmat) + L2_EPS)
        ba = ba_ref[b]
        beta_all = jax.nn.sigmoid(ba[:, 0:LANES])
        z = ba[:, LANES:2 * LANES] + dtb
        softplus = jnp.maximum(z, 0.0) + jnp.log1p(jnp.exp(-jnp.abs(z)))
        g_all = neg_a * softplus
        gc_all = _dot_exact_lhs(tril_bf, g_all)
        gate_all = gate_ref[b]
        gate_all = gate_all * jax.nn.sigmoid(gate_all)

        for h in range(n_heads):
            sl = slice(h * HEAD_DIM, (h + 1) * HEAD_DIM)
            q = q_all[:, sl]
            k = k_all[:, sl]
            v = v_all[:, sl]
            beta = beta_all[:, h:h + 1]
            g = g_all[:, h:h + 1]
            gc = gc_all[:, h:h + 1]
            kb = k * beta
            vb = v * beta
            dm = _dot_exact_lhs(tril_bf, jnp.where(strict, g, 0.0))
            decay = jnp.where(causal, jnp.exp(jnp.where(causal, dm, 0.0)), 0.0)
            k_bf = k.astype(BF16)
            a = jnp.where(strict, _dot_nt(kb.astype(BF16), k_bf) * decay, 0.0)
            p = -a
            tinv = eye + p
            for _ in range(n_double):
                p = _dot_x3(p, p)
                tinv = tinv + _dot_x3(tinv, p)
            egc = jnp.exp(gc)
            u = _dot_x3(tinv, vb)
            w = _dot_x3(tinv, kb * egc)
            qk = jnp.where(causal, _dot_nt(q.astype(BF16), k_bf) * decay, 0.0)
            s = s_s[b, h]
            s_bf = s.astype(BF16)
            v_new = u - _dot(w.astype(BF16), s_bf)
            v_new_bf = v_new.astype(BF16)
            o = _dot((q * egc).astype(BF16), s_bf) + _dot(qk.astype(BF16), v_new_bf)
            g_last = gc[c - 1:c, :]
            k_end = k * jnp.exp(g_last - gc)
            s_s[b, h] = s * jnp.exp(g_last) + _dot_tn(k_end.astype(BF16), v_new_bf)
            ms = jnp.mean(o * o, axis=-1, keepdims=True)
            on = o * lax.rsqrt(ms + RMS_EPS) * gain
            o_ref[b, :, sl] = (on * gate_all[:, sl]).astype(o_ref.dtype)

    tail = xp_s[:, 8 + c - n_tail:8 + c, :]
    xp_s[:, top:8, :] = tail

    @pl.when(t == n_t - 1)
    def _():
        cnew_ref[...] = tail
        snew_ref[...] = s_s[...]


def _gdn(qkv, gate, ba, conv_buf, s0, conv_w, a_log, dt_bias, out_gain, layer, batch_block):
    b, l, w3 = qkv.shape
    width = w3 // 3
    n_heads = width // HEAD_DIM
    chunk = min(GDN_CHUNK, l)
    assert l % chunk == 0 and chunk % 8 == 0 and b % batch_block == 0
    n_tail = conv_buf.shape[1]
    assert n_tail <= min(8, chunk)
    bb = batch_block
    blk = lambda i, t: (i, t, 0)
    fix3 = lambda i, t: (i, 0, 0)
    lay3 = lambda i, t: (layer, 0, 0)
    return pl.pallas_call(
        functools.partial(_gdn_kernel, n_heads, chunk),
        grid=(b // bb, l // chunk),
        in_specs=[pl.BlockSpec((bb, chunk, w3), blk),
                  pl.BlockSpec((bb, chunk, width), blk),
                  pl.BlockSpec((bb, chunk, ba.shape[2]), blk),
                  pl.BlockSpec((bb, n_tail, w3), fix3),
                  pl.BlockSpec((bb, n_heads, HEAD_DIM, HEAD_DIM), lambda i, t: (i, 0, 0, 0)),
                  pl.BlockSpec((None,) + conv_w.shape[1:], lay3),
                  pl.BlockSpec((None, 1, LANES), lay3),
                  pl.BlockSpec((None, 1, LANES), lay3),
                  pl.BlockSpec((None, 1, HEAD_DIM), lay3)],
        out_specs=[pl.BlockSpec((bb, chunk, width), blk),
                   pl.BlockSpec((bb, n_heads, HEAD_DIM, HEAD_DIM), lambda i, t: (i, 0, 0, 0)),
                   pl.BlockSpec((bb, n_tail, w3), fix3)],
        out_shape=[jax.ShapeDtypeStruct((b, l, width), BF16),
                   jax.ShapeDtypeStruct(s0.shape, F32),
                   jax.ShapeDtypeStruct(conv_buf.shape, F32)],
        scratch_shapes=[pltpu.VMEM((bb, 8 + chunk, w3), F32),
                        pltpu.VMEM((bb, n_heads, HEAD_DIM, HEAD_DIM), F32)],
        compiler_params=pltpu.CompilerParams(
            dimension_semantics=("parallel", "arbitrary"), vmem_limit_bytes=VMEM_LIMIT),
        name="gdn",
    )(qkv, gate, ba, conv_buf, s0, conv_w, a_log, dt_bias, out_gain)


def _rel_bucket(rel, n_buckets):
    n = jnp.maximum(rel, 0)
    max_exact = n_buckets // 2
    nf = jnp.maximum(n, 1).astype(F32)
    large = max_exact + (jnp.log(nf / max_exact) / math.log(MAX_DISTANCE / max_exact)
                         * (n_buckets - max_exact)).astype(jnp.int32)
    large = jnp.minimum(large, n_buckets - 1)
    return jnp.where(n < max_exact, n, large)


def _bias_tile(rel_bias, rel):
    n_buckets = rel_bias.shape[0]
    tab = rel_bias.astype(F32) - rel_bias[n_buckets - 1].astype(F32)[None, :]
    bias = jnp.take(tab, _rel_bucket(rel, n_buckets), axis=0)
    bias = jnp.where((rel >= 0)[..., None], bias, -jnp.inf)
    return jnp.moveaxis(bias, -1, 0)


def _lambda(lp_ref, li_ref):
    lp = lp_ref[...]
    e1 = jnp.exp(jnp.sum(lp[0:1, :] * lp[1:2, :], axis=-1, keepdims=True))
    e2 = jnp.exp(jnp.sum(lp[2:3, :] * lp[3:4, :], axis=-1, keepdims=True))
    return e1 - e2 + li_ref[:, 0:1]


def _online_update(s, v_bf, m_ref, l_ref, acc_ref):
    m_prev = m_ref[...]
    m_new = jnp.maximum(m_prev, jnp.max(s, axis=-1, keepdims=True))
    p = jnp.exp(s - m_new[:, 0:1])
    alpha = jnp.exp(m_prev - m_new)
    l_ref[...] = alpha * l_ref[...] + jnp.sum(p, axis=-1, keepdims=True)
    acc_ref[...] = alpha * acc_ref[...] + _dot(p.astype(BF16), v_bf)
    m_ref[...] = m_new


def _attn_kernel(tk, q_ref, k_ref, v_ref, bias_ref, lp_ref, li_ref, gain_ref, o_ref,
                 q_s, m_s, l_s, acc_s):
    tq = q_ref.shape[0]
    qi = pl.program_id(2)
    q = q_ref[...]
    lane = lax.broadcasted_iota(jnp.int32, q.shape, 1)
    q_s[0] = jnp.where(lane < HEAD_DIM, q, 0.0).astype(BF16)
    q_s[1] = jnp.where(lane >= HEAD_DIM, q, 0.0).astype(BF16)
    m_s[...] = jnp.full_like(m_s, NEG_BIG)
    l_s[...] = jnp.zeros_like(l_s)
    acc_s[...] = jnp.zeros_like(acc_s)

    def block(ki, bias):
        start = pl.multiple_of(ki * tk, tk)
        k = k_ref[pl.ds(start, tk), :]
        v = v_ref[pl.ds(start, tk), :]
        for m in range(2):
            s = _dot_nt(q_s[m], k)
            if bias is not None:
                s = s + bias
            _online_update(s, v, m_s.at[m], l_s.at[m], acc_s.at[m])

    def far(ki, carry):
        block(ki, None)
        return carry

    lax.fori_loop(0, jnp.maximum(qi - 1, 0), far, 0)

    @pl.when(qi >= 1)
    def _():
        block(qi - 1, bias_ref[1])

    block(qi, bias_ref[0])

    lam = _lambda(lp_ref, li_ref)
    o = acc_s[0] / l_s[0][:, 0:1] - lam * (acc_s[1] / l_s[1][:, 0:1])
    ms = jnp.mean(o * o, axis=-1, keepdims=True)
    o_ref[...] = (o * lax.rsqrt(ms + RMS_EPS) * gain_ref[...] * (1.0 - li_ref[:, 0:1])).astype(o_ref.dtype)


def _attn_tile(l):
    return 512 if l % 512 == 0 else l


def _attn(q, k_bf, v_bf, bias, lam_params, lam_init, out_gain, layer):
    b, l, width = q.shape
    dv = 2 * HEAD_DIM
    n_heads = width // dv
    tq = tk = bias.shape[-1]
    lay3 = lambda bi, h, qi: (layer, 0, 0)
    return pl.pallas_call(
        functools.partial(_attn_kernel, tk),
        grid=(b, n_heads, l // tq),
        in_specs=[pl.BlockSpec((None, tq, dv), lambda bi, h, qi: (bi, qi, h)),
                  pl.BlockSpec((None, l, dv), lambda bi, h, qi: (bi, 0, h)),
                  pl.BlockSpec((None, l, dv), lambda bi, h, qi: (bi, 0, h)),
                  pl.BlockSpec((None, 2, tq, tk), lambda bi, h, qi: (h, 0, 0, 0)),
                  pl.BlockSpec((None,) + lam_params.shape[1:], lay3),
                  pl.BlockSpec((None, 1, LANES), lay3),
                  pl.BlockSpec((None, 1, dv), lay3)],
        out_specs=pl.BlockSpec((None, tq, dv), lambda bi, h, qi: (bi, qi, h)),
        out_shape=jax.ShapeDtypeStruct((b, l, width), BF16),
        scratch_shapes=[pltpu.VMEM((2, tq, dv), BF16),
                        pltpu.VMEM((2, tq, LANES), F32),
                        pltpu.VMEM((2, tq, LANES), F32),
                        pltpu.VMEM((2, tq, dv), F32)],
        compiler_params=pltpu.CompilerParams(
            dimension_semantics=("parallel", "parallel", "arbitrary"), vmem_limit_bytes=VMEM_LIMIT),
        name="attn",
    )(q, k_bf, v_bf, bias, lam_params, lam_init, out_gain)


def _decode_kernel(n_pp, n_heads, pt_ref, q_ref, kn_ref, vn_ref, bias_ref, lp_ref, li_ref, gain_ref, *rest):
    k_refs = rest[:n_pp]
    v_refs = rest[n_pp:2 * n_pp]
    o_ref, qbd_s, m_s, l_s, acc_s, kpad_s, vpad_s = rest[2 * n_pp:]
    t_new = q_ref.shape[0]
    page = k_refs[0].shape[0]
    dv = 2 * HEAD_DIM
    rows = 2 * n_heads * t_new
    j = pl.program_id(1)
    n_j = pl.num_programs(1)

    @pl.when(j == 0)
    def _():
        q = q_ref[...]
        qt = jnp.concatenate([q] * (2 * n_heads), axis=0)
        r = lax.broadcasted_iota(jnp.int32, qt.shape, 0) // t_new
        cgrp = lax.broadcasted_iota(jnp.int32, qt.shape, 1) // HEAD_DIM
        qbd_s[...] = jnp.where(r == cgrp, qt, 0.0)
        m_s[...] = jnp.full_like(m_s, NEG_BIG)
        l_s[...] = jnp.zeros_like(l_s)
        acc_s[...] = jnp.zeros_like(acc_s)
        kpad_s[...] = jnp.zeros_like(kpad_s)
        vpad_s[...] = jnp.zeros_like(vpad_s)
        kpad_s[0:t_new, :] = kn_ref[...]
        vpad_s[0:t_new, :] = vn_ref[...]

    qbd = qbd_s[...]

    def step(k_list, v_list, bias_list):
        s_list = []
        for kk, bb in zip(k_list, bias_list):
            s = _dot_nt(qbd, kk)
            s_list.append(s if bb is None else s + bb)
        m_prev = m_s[...]
        m_cur = s_list[0].max(axis=-1, keepdims=True)
        for s in s_list[1:]:
            m_cur = jnp.maximum(m_cur, s.max(axis=-1, keepdims=True))
        m_new = jnp.maximum(m_prev, m_cur)
        alpha = jnp.exp(m_prev - m_new)
        lsum = jnp.zeros((rows, 1), F32)
        pv = jnp.zeros((rows, v_list[0].shape[-1]), F32)
        for s, vv in zip(s_list, v_list):
            p = jnp.exp(s - m_new[:, 0:1])
            lsum = lsum + jnp.sum(p, axis=-1, keepdims=True)
            pv = pv + _dot(p, vv)
        l_s[...] = alpha * l_s[...] + lsum
        acc_s[...] = alpha[:, 0:1] * acc_s[...] + pv
        m_s[...] = m_new

    @pl.when(j < n_j - 1)
    def _():
        step([r[...] for r in k_refs], [r[...] for r in v_refs], [None] * n_pp)

    @pl.when(j == n_j - 1)
    def _():
        ks = [r[...] for r in k_refs] + [kpad_s[...]]
        vs = [r[...] for r in v_refs] + [vpad_s[...]]
        step(ks, vs, [None] * (n_pp - 1) + [bias_ref[0], bias_ref[1]])
        lam = _lambda(lp_ref, li_ref)
        acc = acc_s[...]
        inv_l = 1.0 / l_s[...][:, 0:1]
        scale = 1.0 - li_ref[:, 0:1]
        for h in range(n_heads):
            r0 = 2 * h * t_new
            cs = slice(h * dv, (h + 1) * dv)
            o0 = acc[r0:r0 + t_new, cs] * inv_l[r0:r0 + t_new]
            o1 = acc[r0 + t_new:r0 + 2 * t_new, cs] * inv_l[r0 + t_new:r0 + 2 * t_new]
            o = o0 - lam * o1
            ms = jnp.mean(o * o, axis=-1, keepdims=True)
            o_ref[:, cs] = (o * lax.rsqrt(ms + RMS_EPS) * gain_ref[...] * scale).astype(o_ref.dtype)


def _decode(q, k_new, v_new, cache_k, cache_v, page_table, bias, lam_params, lam_init, out_gain, layer,
            pages_per_step):
    b, t_new, width = q.shape
    dv = 2 * HEAD_DIM
    n_heads = width // dv
    page = cache_k.shape[2]
    n_pages = page_table.shape[1]
    n_pp = pages_per_step
    assert n_pages % n_pp == 0 and t_new <= page
    rows = 2 * n_heads * t_new
    seq = lambda bi, j, pt: (bi, 0, 0)
    lay3 = lambda bi, j, pt: (layer, 0, 0)

    def page_spec(i):
        return pl.BlockSpec((None, None, page, width), lambda bi, j, pt: (layer, pt[bi, j * n_pp + i], 0, 0))

    grid_spec = pltpu.PrefetchScalarGridSpec(
        num_scalar_prefetch=1,
        grid=(b, n_pages // n_pp),
        in_specs=[pl.BlockSpec((None, t_new, width), seq),
                  pl.BlockSpec((None, t_new, width), seq),
                  pl.BlockSpec((None, t_new, width), seq),
                  pl.BlockSpec(bias.shape, lambda bi, j, pt: (0, 0, 0)),
                  pl.BlockSpec((None,) + lam_params.shape[1:], lay3),
                  pl.BlockSpec((None, 1, LANES), lay3),
                  pl.BlockSpec((None, 1, dv), lay3)]
                 + [page_spec(i) for i in range(n_pp)] * 2,
        out_specs=pl.BlockSpec((None, t_new, width), seq),
        scratch_shapes=[pltpu.VMEM((rows, width), F32),
                        pltpu.VMEM((rows, LANES), F32),
                        pltpu.VMEM((rows, LANES), F32),
                        pltpu.VMEM((rows, width), F32),
                        pltpu.VMEM((page, width), F32),
                        pltpu.VMEM((page, width), F32)])
    return pl.pallas_call(
        functools.partial(_decode_kernel, n_pp, n_heads),
        grid_spec=grid_spec,
        out_shape=jax.ShapeDtypeStruct((b, t_new, width), BF16),
        compiler_params=pltpu.CompilerParams(
            dimension_semantics=("parallel", "arbitrary"), vmem_limit_bytes=VMEM_LIMIT),
        name="decode",
    )(page_table, q, k_new, v_new, bias, lam_params, lam_init, out_gain,
      *([cache_k] * n_pp), *([cache_v] * n_pp))


def _pad_lanes(a, width=LANES):
    return jnp.pad(a, [(0, 0)] * (a.ndim - 1) + [(0, width - a.shape[-1])])


def kernel(x_prompt, x_sample, cache_k, cache_v, state_gdn, state_conv, page_table, ffn1_norm, ffn1_w_gate_up, ffn1_w_down, mix_norm, w_in, conv_w, gdn_a_log, gdn_dt_bias, gdn_out_norm, diff_q_norm, diff_k_norm, diff_lambda, diff_out_norm, rel_bias, w_out, ffn2_norm, ffn2_w_gate_up, ffn2_w_down):
    depth, d_model, _ = w_in.shape
    n_gdn = gdn_a_log.shape[1]
    n_diff = rel_bias.shape[1]
    gdn_w = n_gdn * HEAD_DIM
    diff_w = n_diff * 2 * HEAD_DIM
    conv_dim = conv_w.shape[2]
    assert conv_dim == 3 * gdn_w and w_out.shape[1] == gdn_w + diff_w
    page = cache_k.shape[2]
    assert page >= MAX_DISTANCE
    past_len = page_table.shape[1] * page

    o = 0
    cols = {}
    for name, wd in (("qkv", conv_dim), ("gate", gdn_w), ("b", n_gdn), ("a", n_gdn),
                     ("dq", diff_w), ("dk", diff_w), ("dv", diff_w)):
        cols[name] = w_in[:, :, o:o + wd]
        o += wd
    w_in_r = jnp.concatenate([cols["qkv"], cols["gate"], cols["dq"], cols["dk"], cols["dv"],
                              _pad_lanes(cols["b"]), _pad_lanes(cols["a"])], axis=-1).astype(BF16)
    widths = (conv_dim, gdn_w, diff_w, 2 * LANES)
    bf = lambda a: a.astype(BF16)
    w1_gu, w1_d, w2_gu, w2_d, w_out_bf = bf(ffn1_w_gate_up), bf(ffn1_w_down), bf(ffn2_w_gate_up), bf(ffn2_w_down), bf(w_out)
    row3 = lambda a: a.reshape(depth, 1, -1)
    n1, n2, nm = row3(ffn1_norm), row3(ffn2_norm), row3(mix_norm)
    q_gain = row3(jnp.tile(diff_q_norm, (1, diff_w // HEAD_DIM)))
    k_gain = row3(jnp.tile(diff_k_norm, (1, diff_w // HEAD_DIM)))
    a_log = row3(_pad_lanes(gdn_a_log))
    dt_bias = row3(_pad_lanes(gdn_dt_bias))
    gdn_gain = row3(gdn_out_norm)
    diff_gain = row3(diff_out_norm)
    lam_init = jnp.asarray([0.8 - 0.6 * math.exp(-0.3 * l) for l in range(depth)], F32)
    lam_init = jnp.broadcast_to(lam_init[:, None, None], (depth, 1, LANES))
    cache_k2 = cache_k.reshape(cache_k.shape[:3] + (diff_w,))
    cache_v2 = cache_v.reshape(cache_v.shape[:3] + (diff_w,))

    def bias_prompt(l):
        t = _attn_tile(l)
        i = jnp.arange(t, dtype=jnp.int32)
        rel = i[:, None] - i[None, :]
        return jnp.stack([_bias_tile(rel_bias, rel), _bias_tile(rel_bias, rel + t)], axis=1)

    def bias_sample(t_new):
        tok = jnp.arange(t_new, dtype=jnp.int32)
        lane = jnp.arange(page, dtype=jnp.int32)
        rel_last = (page + tok)[:, None] - lane[None, :]
        rel_new = jnp.where(lane[None, :] < t_new, tok[:, None] - lane[None, :], -1)
        tiles = jnp.stack([_bias_tile(rel_bias, rel_last), _bias_tile(rel_bias, rel_new)], axis=0)
        tiles = jnp.broadcast_to(tiles[:, :, None], (2, n_diff, 2, t_new, page))
        return tiles.reshape(2, n_diff * 2 * t_new, page)

    def run(x, paged):
        b, l, _ = x.shape
        m = b * l
        xf = x.reshape(m, d_model)
        ks, vs, ss, cs = [], [], [], []
        if paged:
            bias = bias_sample(l)
        else:
            bias = bias_prompt(l)
        for layer in range(depth):
            xf = _ffn(xf, n1, w1_gu, w1_d, layer)
            qkv, gate, ba, q, k_rows, v_rows, k_bf, v_bf = _inproj(xf, nm, w_in_r, q_gain, k_gain, widths, layer)
            r3 = lambda a: a.reshape(b, l, a.shape[-1])
            if paged:
                conv_buf, s0, bb = state_conv[layer], state_gdn[layer], 4 if b % 4 == 0 else 1
            else:
                conv_buf = jnp.zeros((b, conv_w.shape[1] - 1, conv_dim), F32)
                s0 = jnp.zeros((b, n_gdn, HEAD_DIM, HEAD_DIM), F32)
                bb = b
            o_gdn, s_new, conv_new = _gdn(r3(qkv), r3(gate), r3(ba), conv_buf, s0, conv_w, a_log, dt_bias,
                                          gdn_gain, layer, bb)
            if paged:
                o_diff = _decode(r3(q), r3(k_rows), r3(v_rows), cache_k2, cache_v2, page_table, bias,
                                 diff_lambda, lam_init, diff_gain, layer, 8 if page_table.shape[1] % 8 == 0 else 1)
            else:
                o_diff = _attn(r3(q), r3(k_bf), r3(v_bf), bias, diff_lambda, lam_init, diff_gain, layer)
            xf = _ffn(xf, n2, w2_gu, w2_d, layer,
                      mix=(o_gdn.reshape(m, gdn_w), o_diff.reshape(m, diff_w), w_out_bf))
            ks.append(k_rows.reshape(b, l, n_diff, 2 * HEAD_DIM))
            vs.append(v_rows.reshape(b, l, n_diff, 2 * HEAD_DIM))
            ss.append(s_new)
            cs.append(conv_new)
        return xf.reshape(b, l, d_model), jnp.stack(ks), jnp.stack(vs), jnp.stack(ss), jnp.stack(cs)

    y_p, k_p, v_p, s_p, c_p = run(x_prompt, False)
    y_s, k_s, v_s, s_s, c_s = run(x_sample, True)
    return (y_p, y_s, k_p, v_p, s_p, c_p, k_s, v_s, s_s, c_s)
```

```python
import functools
import math

import jax
import jax.numpy as jnp
from jax import lax
from jax.experimental import pallas as pl
from jax.experimental.pallas import tpu as pltpu

F32 = jnp.float32
BF16 = jnp.bfloat16

HEAD_DIM = 64
GDN_CHUNK = 64
MAX_DISTANCE = 128
RMS_EPS = 1e-6
L2_EPS = 1e-6
LANES = 128
MXU_DIM = 256
VMEM_LIMIT = 52 * 1024 * 1024
NEG_BIG = -1e30


def _dot(a, b):
    return jnp.dot(a, b, preferred_element_type=F32)


def _dot_nt(a, b):
    return lax.dot_general(a, b, (((1,), (1,)), ((), ())), preferred_element_type=F32)


def _dot_tn(a, b):
    return lax.dot_general(a, b, (((0,), (0,)), ((), ())), preferred_element_type=F32)


def _split3(x):
    hi = x.astype(BF16)
    r = x - hi.astype(F32)
    mid = r.astype(BF16)
    lo = (r - mid.astype(F32)).astype(BF16)
    return hi, mid, lo


def _dot_exact_lhs(a_bf, b):
    b0, b1, b2 = _split3(b)
    return _dot(a_bf, b0) + _dot(a_bf, b1) + _dot(a_bf, b2)


def _group_sumsq(x, gmat):
    sq = x * x
    hi = sq.astype(BF16)
    lo = (sq - hi.astype(F32)).astype(BF16)
    return _dot(hi, gmat) + _dot(lo, gmat)


def _group_matrix(width):
    r = lax.broadcasted_iota(jnp.int32, (width, width), 0) // HEAD_DIM
    c = lax.broadcasted_iota(jnp.int32, (width, width), 1) // HEAD_DIM
    return jnp.where(r == c, 1.0, 0.0).astype(BF16)


def _ffn_kernel(has_mix, n_f, *refs):
    if has_mix:
        (x_ref, og_ref, od_ref, wo_ref, nw_ref, wg_ref, wu_ref, wd_ref,
         o_ref, h_s, acc_s, x_s) = refs
    else:
        x_ref, nw_ref, wg_ref, wu_ref, wd_ref, o_ref, h_s, acc_s, x_s = refs
    f = pl.program_id(1)

    @pl.when(f == 0)
    def _():
        x = x_ref[...]
        if has_mix:
            half = og_ref.shape[-1]
            x = x + _dot(og_ref[...], wo_ref[0:half, :]) + _dot(od_ref[...], wo_ref[half:, :])
        x_s[...] = x
        ms = jnp.mean(x * x, axis=-1, keepdims=True)
        h_s[...] = (x * lax.rsqrt(ms + RMS_EPS) * nw_ref[...]).astype(BF16)
        acc_s[...] = jnp.zeros_like(acc_s)

    h = h_s[...]
    g = _dot(h, wg_ref[...])
    u = _dot(h, wu_ref[...])
    a = (g * jax.nn.sigmoid(g)) * u
    acc_s[...] += _dot(a.astype(BF16), wd_ref[...])

    @pl.when(f == n_f - 1)
    def _():
        o_ref[...] = x_s[...] + 0.5 * acc_s[...]


def _ffn_tiles(m, d_ff):
    tm = 512 if m % 512 == 0 else m
    tf = d_ff // 2 if (d_ff // 2) % LANES == 0 else d_ff
    return tm, tf


def _ffn(x, norm_w, w_gu, w_down, layer, mix=None):
    m, d = x.shape
    d_ff = w_down.shape[1]
    tm, tf = _ffn_tiles(m, d_ff)
    n_f = d_ff // tf
    row = lambda i, f: (i, 0)
    in_specs = [pl.BlockSpec((tm, d), row)]
    args = [x]
    if mix is not None:
        og, od, w_out = mix
        in_specs += [pl.BlockSpec((tm, og.shape[1]), row), pl.BlockSpec((tm, od.shape[1]), row),
                     pl.BlockSpec((None,) + w_out.shape[1:], lambda i, f: (layer, 0, 0))]
        args += [og, od, w_out]
    in_specs += [
        pl.BlockSpec((None, 1, d), lambda i, f: (layer, 0, 0)),
        pl.BlockSpec((None, d, tf), lambda i, f: (layer, 0, f)),
        pl.BlockSpec((None, d, tf), lambda i, f: (layer, 0, f + n_f)),
        pl.BlockSpec((None, tf, d), lambda i, f: (layer, f, 0)),
    ]
    args += [norm_w, w_gu, w_gu, w_down]
    return pl.pallas_call(
        functools.partial(_ffn_kernel, mix is not None, n_f),
        grid=(m // tm, n_f),
        in_specs=in_specs,
        out_specs=pl.BlockSpec((tm, d), row),
        out_shape=jax.ShapeDtypeStruct((m, d), F32),
        scratch_shapes=[pltpu.VMEM((tm, d), BF16), pltpu.VMEM((tm, d), F32), pltpu.VMEM((tm, d), F32)],
        compiler_params=pltpu.CompilerParams(
            dimension_semantics=("parallel", "arbitrary"), vmem_limit_bytes=VMEM_LIMIT),
        name="ffn_mix" if mix is not None else "ffn",
    )(*args)


def _inproj_kernel(widths, x_ref, nw_ref, w_ref, qg_ref, kg_ref,
                   qkv_ref, gate_ref, ba_ref, q_ref, k_ref, v_ref, kb_ref, vb_ref):
    c_qkv, c_gate, c_d, c_ba = widths
    x = x_ref[...]
    ms = jnp.mean(x * x, axis=-1, keepdims=True)
    h = (x * lax.rsqrt(ms + RMS_EPS) * nw_ref[...]).astype(BF16)
    o = 0
    qkv_ref[...] = _dot(h, w_ref[:, o:o + c_qkv]); o += c_qkv
    gate_ref[...] = _dot(h, w_ref[:, o:o + c_gate]); o += c_gate
    dq = _dot(h, w_ref[:, o:o + c_d]); o += c_d
    dk = _dot(h, w_ref[:, o:o + c_d]); o += c_d
    dv = _dot(h, w_ref[:, o:o + c_d]); o += c_d
    ba_ref[...] = _dot(h, w_ref[:, o:o + c_ba])
    gmat = _group_matrix(c_d)
    inv_hd = 1.0 / HEAD_DIM
    qn = dq * lax.rsqrt(_group_sumsq(dq, gmat) * inv_hd + RMS_EPS) * qg_ref[...]
    kn = dk * lax.rsqrt(_group_sumsq(dk, gmat) * inv_hd + RMS_EPS) * kg_ref[...]
    q_ref[...] = qn * (HEAD_DIM ** -0.5)
    k_ref[...] = kn
    v_ref[...] = dv
    kb_ref[...] = kn.astype(BF16)
    vb_ref[...] = dv.astype(BF16)


def _inproj(x, norm_w, w_in, q_gain, k_gain, widths, layer):
    m, d = x.shape
    c_qkv, c_gate, c_d, c_ba = widths
    tm = 512 if m % 512 == 0 else m
    row = lambda i: (i, 0)
    lay3 = lambda i: (layer, 0, 0)
    outs = [(c_qkv, F32), (c_gate, F32), (c_ba, F32), (c_d, F32), (c_d, F32), (c_d, F32), (c_d, BF16), (c_d, BF16)]
    return pl.pallas_call(
        functools.partial(_inproj_kernel, widths),
        grid=(m // tm,),
        in_specs=[pl.BlockSpec((tm, d), row),
                  pl.BlockSpec((None, 1, d), lay3),
                  pl.BlockSpec((None,) + w_in.shape[1:], lay3),
                  pl.BlockSpec((None, 1, c_d), lay3),
                  pl.BlockSpec((None, 1, c_d), lay3)],
        out_specs=[pl.BlockSpec((tm, c), row) for c, _ in outs],
        out_shape=[jax.ShapeDtypeStruct((m, c), dt) for c, dt in outs],
        compiler_params=pltpu.CompilerParams(
            dimension_semantics=("parallel",), vmem_limit_bytes=VMEM_LIMIT),
        name="inproj",
    )(x, norm_w, w_in, q_gain, k_gain)


def _block_mask(rows, cols, row_group, col_group):
    r = lax.broadcasted_iota(jnp.int32, (rows, cols), 0) // row_group
    c = lax.broadcasted_iota(jnp.int32, (rows, cols), 1) // col_group
    return r == c


def _block_diag(x, mask):
    reps = mask.shape[0] // x.shape[0]
    return jnp.where(mask, jnp.concatenate([x] * reps, axis=0), 0.0).astype(BF16)


def _dot_exact_rhs(a, b_bf):
    a0, a1, a2 = _split3(a)
    return _dot(a0, b_bf) + _dot(a1, b_bf) + _dot(a2, b_bf)


def _gdn_kernel(n_heads, chunk, gh, qkv_ref, gate_ref, ba_ref, cbuf_ref, s0_ref, cw_ref, alog_ref, dtb_ref,
                gain_ref, o_ref, snew_ref, cnew_ref, xp_s, s_s):
    bb = qkv_ref.shape[0]
    c = chunk
    d = HEAD_DIM
    width = n_heads * d
    n_grp = n_heads // gh
    tw = gh * c
    dw = gh * d
    n_tail = cbuf_ref.shape[1]
    n_taps = n_tail + 1
    top = 8 - n_tail
    t = pl.program_id(1)
    n_t = pl.num_programs(1)

    @pl.when(t == 0)
    def _():
        xp_s[:, top:8, :] = cbuf_ref[...]
        s_s[...] = jnp.zeros_like(s_s)
        for b in range(bb):
            for h in range(n_heads):
                o = (h % gh) * d
                s_s[b, h // gh, o:o + d, o:o + d] = s0_ref[b, h]

    xp_s[:, 8:8 + c, :] = qkv_ref[...]

    cw = cw_ref[...]
    gmat = _group_matrix(width)
    ri = lax.broadcasted_iota(jnp.int32, (c, c), 0)
    ci = lax.broadcasted_iota(jnp.int32, (c, c), 1)
    tril_bf = jnp.where(ri >= ci, 1.0, 0.0).astype(BF16)
    row_t = lax.broadcasted_iota(jnp.int32, (c, tw), 0)
    col_t = lax.broadcasted_iota(jnp.int32, (c, tw), 1) % c
    causal = row_t >= col_t
    strict = row_t > col_t
    eye = jnp.where(row_t == col_t, 1.0, 0.0).astype(F32)
    bd_tt = _block_mask(tw, tw, c, c)
    bd_td = _block_mask(tw, dw, c, d)
    bd_dd = _block_mask(dw, dw, d, d)
    gmat_g = jnp.where(bd_dd, 1.0, 0.0).astype(BF16)
    expand_d = jnp.where(_block_mask(LANES, width, 1, d), 1.0, 0.0).astype(BF16)
    expand_t = jnp.where(_block_mask(LANES, n_heads * c, 1, c), 1.0, 0.0).astype(BF16)
    neg_a = -jnp.exp(alog_ref[...])
    dtb = dtb_ref[...]
    gain = gain_ref[...]

    for b in range(bb):
        y = xp_s[b, top:top + c, :] * cw[0:1, :]
        for i in range(1, n_taps):
            y = y + xp_s[b, top + i:top + i + c, :] * cw[i:i + 1, :]
        qkv = y * jax.nn.sigmoid(y)
        q_all = qkv[:, 0:width]
        k_all = qkv[:, width:2 * width]
        v_all = qkv[:, 2 * width:3 * width]
        q_all = q_all * lax.rsqrt(_group_sumsq(q_all, gmat) + L2_EPS) * (HEAD_DIM ** -0.5)
        k_all = k_all * lax.rsqrt(_group_sumsq(k_all, gmat) + L2_EPS)
        ba = ba_ref[b]
        beta_all = jax.nn.sigmoid(ba[:, 0:LANES])
        z = ba[:, LANES:2 * LANES] + dtb
        softplus = jnp.maximum(z, 0.0) + jnp.log1p(jnp.exp(-jnp.abs(z)))
        g_all = neg_a * softplus
        gc_all = _dot_exact_lhs(tril_bf, g_all)
        gate_all = gate_ref[b]
        gate_all = gate_all * jax.nn.sigmoid(gate_all)
        beta_e = _dot_exact_rhs(beta_all, expand_d)
        gc_e = _dot_exact_rhs(gc_all, expand_d)
        g_t = _dot_exact_rhs(g_all, expand_t)
        egc_e = jnp.exp(gc_e)
        glast_e = gc_e[c - 1:c, :]
        kend_e = jnp.exp(glast_e - gc_e)
        sdec_e = jnp.exp(glast_e)

        for gi in range(n_grp):
            sl = slice(gi * dw, (gi + 1) * dw)
            q = q_all[:, sl]
            k = k_all[:, sl]
            beta = beta_e[:, sl]
            egc = egc_e[:, sl]
            kb = k * beta
            vb = v_all[:, sl] * beta
            dm = _dot_exact_lhs(tril_bf, jnp.where(strict, g_t[:, gi * tw:(gi + 1) * tw], 0.0))
            decay = jnp.where(causal, jnp.exp(dm), 0.0)
            k_bd = _block_diag(k, bd_td)
            a = jnp.where(strict, _dot_nt(kb.astype(BF16), k_bd) * decay, 0.0)
            tinv = eye - jnp.where(row_t // 2 == col_t // 2, a, 0.0)
            size = 2
            while size < c:
                lower_left = ((row_t // (2 * size) == col_t // (2 * size))
                              & (row_t % (2 * size) >= size) & (col_t % (2 * size) < size))
                a21 = _block_diag(jnp.where(lower_left, a, 0.0), bd_tt)
                t2_a21 = _dot(tinv.astype(BF16), a21)
                tinv = tinv - _dot(t2_a21.astype(BF16), _block_diag(tinv, bd_tt))
                size *= 2
            tinv_bf = tinv.astype(BF16)
            u = _dot(tinv_bf, _block_diag(vb, bd_td))
            w = _dot(tinv_bf, _block_diag(kb * egc, bd_td))
            qk = jnp.where(causal, _dot_nt(q.astype(BF16), k_bd) * decay, 0.0)
            s = s_s[b, gi]
            ws = _dot(jnp.concatenate([w, q * egc], axis=0).astype(BF16), s.astype(BF16))
            v_new = u - ws[0:c]
            o = ws[c:2 * c] + _dot(qk.astype(BF16), _block_diag(v_new, bd_td))
            k_end = k * kend_e[:, sl]
            cross = _dot_tn(k_end.astype(BF16), v_new.astype(BF16))
            s_s[b, gi] = s * sdec_e[:, sl] + jnp.where(bd_dd, cross, 0.0)
            ms = _group_sumsq(o, gmat_g) * (1.0 / d)
            on = o * lax.rsqrt(ms + RMS_EPS) * gain[:, sl]
            o_ref[b, :, sl] = (on * gate_all[:, sl]).astype(o_ref.dtype)

    tail = xp_s[:, 8 + c - n_tail:8 + c, :]
    xp_s[:, top:8, :] = tail

    @pl.when(t == n_t - 1)
    def _():
        cnew_ref[...] = tail
        for b in range(bb):
            for h in range(n_heads):
                o = (h % gh) * d
                snew_ref[b, h] = s_s[b, h // gh, o:o + d, o:o + d]


def _gdn(qkv, gate, ba, conv_buf, s0, conv_w, a_log, dt_bias, out_gain, layer, batch_block):
    b, l, w3 = qkv.shape
    width = w3 // 3
    n_heads = width // HEAD_DIM
    chunk = min(GDN_CHUNK, l)
    assert l % chunk == 0 and chunk % 8 == 0 and chunk & (chunk - 1) == 0 and b % batch_block == 0
    n_tail = conv_buf.shape[1]
    assert n_tail <= min(8, chunk)
    bb = batch_block
    gh = min(n_heads, max(1, MXU_DIM // chunk))
    assert n_heads % gh == 0
    blk = lambda i, t: (i, t, 0)
    fix3 = lambda i, t: (i, 0, 0)
    lay3 = lambda i, t: (layer, 0, 0)
    return pl.pallas_call(
        functools.partial(_gdn_kernel, n_heads, chunk, gh),
        grid=(b // bb, l // chunk),
        in_specs=[pl.BlockSpec((bb, chunk, w3), blk),
                  pl.BlockSpec((bb, chunk, width), blk),
                  pl.BlockSpec((bb, chunk, ba.shape[2]), blk),
                  pl.BlockSpec((bb, n_tail, w3), fix3),
                  pl.BlockSpec((bb, n_heads, HEAD_DIM, HEAD_DIM), lambda i, t: (i, 0, 0, 0)),
                  pl.BlockSpec((None,) + conv_w.shape[1:], lay3),
                  pl.BlockSpec((None, 1, LANES), lay3),
                  pl.BlockSpec((None, 1, LANES), lay3),
                  pl.BlockSpec((None, 1, width), lay3)],
        out_specs=[pl.BlockSpec((bb, chunk, width), blk),
                   pl.BlockSpec((bb, n_heads, HEAD_DIM, HEAD_DIM), lambda i, t: (i, 0, 0, 0)),
                   pl.BlockSpec((bb, n_tail, w3), fix3)],
        out_shape=[jax.ShapeDtypeStruct((b, l, width), BF16),
                   jax.ShapeDtypeStruct(s0.shape, F32),
                   jax.ShapeDtypeStruct(conv_buf.shape, F32)],
        scratch_shapes=[pltpu.VMEM((bb, 8 + chunk, w3), F32),
                        pltpu.VMEM((bb, n_heads // gh, gh * HEAD_DIM, gh * HEAD_DIM), F32)],
        compiler_params=pltpu.CompilerParams(
            dimension_semantics=("parallel", "arbitrary"), vmem_limit_bytes=VMEM_LIMIT),
        name="gdn",
    )(qkv, gate, ba, conv_buf, s0, conv_w, a_log, dt_bias, out_gain)


def _rel_bucket(rel, n_buckets):
    n = jnp.maximum(rel, 0)
    max_exact = n_buckets // 2
    nf = jnp.maximum(n, 1).astype(F32)
    large = max_exact + (jnp.log(nf / max_exact) / math.log(MAX_DISTANCE / max_exact)
                         * (n_buckets - max_exact)).astype(jnp.int32)
    large = jnp.minimum(large, n_buckets - 1)
    return jnp.where(n < max_exact, n, large)


def _bias_tile(rel_bias, rel):
    n_buckets, n_heads = rel_bias.shape
    tab = rel_bias.astype(F32) - rel_bias[n_buckets - 1].astype(F32)[None, :]
    bucket = _rel_bucket(rel, n_buckets)[None]
    bias = jnp.zeros((n_heads,) + rel.shape, F32)
    for n in range(n_buckets):
        bias = jnp.where(bucket == n, tab[n].reshape((n_heads,) + (1,) * rel.ndim), bias)
    return jnp.where((rel >= 0)[None], bias, -jnp.inf)


def _lambda(lp_ref, li_ref):
    lp = lp_ref[...]
    e1 = jnp.exp(jnp.sum(lp[0:1, :] * lp[1:2, :], axis=-1, keepdims=True))
    e2 = jnp.exp(jnp.sum(lp[2:3, :] * lp[3:4, :], axis=-1, keepdims=True))
    return e1 - e2 + li_ref[:, 0:1]


def _online_update(s, v_bf, m_ref, l_ref, acc_ref):
    m_prev = m_ref[...]
    m_new = jnp.maximum(m_prev, jnp.max(s, axis=-1, keepdims=True))
    p = jnp.exp(s - m_new[:, 0:1])
    alpha = jnp.exp(m_prev - m_new)
    l_ref[...] = alpha * l_ref[...] + jnp.sum(p, axis=-1, keepdims=True)
    acc_ref[...] = alpha * acc_ref[...] + _dot(p.astype(BF16), v_bf)
    m_ref[...] = m_new


def _attn_kernel(tk, q_ref, k_ref, v_ref, bias_ref, lp_ref, li_ref, gain_ref, o_ref,
                 q_s, m_s, l_s, acc_s):
    tq = q_ref.shape[0]
    qi = pl.program_id(2)
    q = q_ref[...]
    lane = lax.broadcasted_iota(jnp.int32, q.shape, 1)
    q_s[0] = jnp.where(lane < HEAD_DIM, q, 0.0).astype(BF16)
    q_s[1] = jnp.where(lane >= HEAD_DIM, q, 0.0).astype(BF16)
    m_s[...] = jnp.full_like(m_s, NEG_BIG)
    l_s[...] = jnp.zeros_like(l_s)
    acc_s[...] = jnp.zeros_like(acc_s)

    def block(ki, bias):
        start = pl.multiple_of(ki * tk, tk)
        k = k_ref[pl.ds(start, tk), :]
        v = v_ref[pl.ds(start, tk), :]
        for m in range(2):
            s = _dot_nt(q_s[m], k)
            if bias is not None:
                s = s + bias
            _online_update(s, v, m_s.at[m], l_s.at[m], acc_s.at[m])

    def far(ki, carry):
        block(ki, None)
        return carry

    lax.fori_loop(0, jnp.maximum(qi - 1, 0), far, 0)

    @pl.when(qi >= 1)
    def _():
        block(qi - 1, bias_ref[1])

    block(qi, bias_ref[0])

    lam = _lambda(lp_ref, li_ref)
    o = acc_s[0] / l_s[0][:, 0:1] - lam * (acc_s[1] / l_s[1][:, 0:1])
    ms = jnp.mean(o * o, axis=-1, keepdims=True)
    o_ref[...] = (o * lax.rsqrt(ms + RMS_EPS) * gain_ref[...] * (1.0 - li_ref[:, 0:1])).astype(o_ref.dtype)


def _attn_tile(l):
    return 512 if l % 512 == 0 else l


def _attn(q, k_bf, v_bf, bias, lam_params, lam_init, out_gain, layer):
    b, l, width = q.shape
    dv = 2 * HEAD_DIM
    n_heads = width // dv
    tq = tk = bias.shape[-1]
    lay3 = lambda bi, h, qi: (layer, 0, 0)
    return pl.pallas_call(
        functools.partial(_attn_kernel, tk),
        grid=(b, n_heads, l // tq),
        in_specs=[pl.BlockSpec((None, tq, dv), lambda bi, h, qi: (bi, qi, h)),
                  pl.BlockSpec((None, l, dv), lambda bi, h, qi: (bi, 0, h)),
                  pl.BlockSpec((None, l, dv), lambda bi, h, qi: (bi, 0, h)),
                  pl.BlockSpec((None, 2, tq, tk), lambda bi, h, qi: (h, 0, 0, 0)),
                  pl.BlockSpec((None,) + lam_params.shape[1:], lay3),
                  pl.BlockSpec((None, 1, LANES), lay3),
                  pl.BlockSpec((None, 1, dv), lay3)],
        out_specs=pl.BlockSpec((None, tq, dv), lambda bi, h, qi: (bi, qi, h)),
        out_shape=jax.ShapeDtypeStruct((b, l, width), BF16),
        scratch_shapes=[pltpu.VMEM((2, tq, dv), BF16),
                        pltpu.VMEM((2, tq, LANES), F32),
                        pltpu.VMEM((2, tq, LANES), F32),
                        pltpu.VMEM((2, tq, dv), F32)],
        compiler_params=pltpu.CompilerParams(
            dimension_semantics=("parallel", "parallel", "arbitrary"), vmem_limit_bytes=VMEM_LIMIT),
        name="attn",
    )(q, k_bf, v_bf, bias, lam_params, lam_init, out_gain)


def _decode_kernel(n_pp, n_heads, pt_ref, q_ref, kn_ref, vn_ref, bias_ref, lp_ref, li_ref, gain_ref, *rest):
    k_refs = rest[:n_pp]
    v_refs = rest[n_pp:2 * n_pp]
    o_ref, q_s, m_s, l_s, acc_s, kpad_s, vpad_s = rest[2 * n_pp:]
    t_new = q_ref.shape[0]
    page = k_refs[0].shape[0] // n_heads
    dv = 2 * HEAD_DIM
    rows = 2 * t_new
    j = pl.program_id(1)
    n_j = pl.num_programs(1)

    @pl.when(j == 0)
    def _():
        m_s[...] = jnp.full_like(m_s, NEG_BIG)
        l_s[...] = jnp.zeros_like(l_s)
        acc_s[...] = jnp.zeros_like(acc_s)
        kpad_s[...] = jnp.zeros_like(kpad_s)
        vpad_s[...] = jnp.zeros_like(vpad_s)
        lane = lax.broadcasted_iota(jnp.int32, (t_new, dv), 1)
        for h in range(n_heads):
            cs = slice(h * dv, (h + 1) * dv)
            qh = q_ref[:, cs]
            q_s[h, 0:t_new, :] = jnp.where(lane < HEAD_DIM, qh, 0.0)
            q_s[h, t_new:rows, :] = jnp.where(lane >= HEAD_DIM, qh, 0.0)
            kpad_s[h, 0:t_new, :] = kn_ref[:, cs]
            vpad_s[h, 0:t_new, :] = vn_ref[:, cs]

    def head_rows(ref, h):
        return ref[pl.ds(h, page, stride=n_heads), :]

    def step(k_get, v_get, bias_list):
        for h in range(n_heads):
            qh = q_s[h]
            s_list = []
            for i, bb in enumerate(bias_list):
                s = _dot_nt(qh, k_get(i, h))
                s_list.append(s if bb is None else s + bb[h * rows:(h + 1) * rows, :])
            m_prev = m_s[h]
            m_cur = s_list[0].max(axis=-1, keepdims=True)
            for s in s_list[1:]:
                m_cur = jnp.maximum(m_cur, s.max(axis=-1, keepdims=True))
            m_new = jnp.maximum(m_prev, m_cur)
            alpha = jnp.exp(m_prev - m_new)
            lsum = jnp.zeros((rows, 1), F32)
            pv = jnp.zeros((rows, dv), F32)
            m_keys = m_new if page == LANES else m_new[:, 0:1]
            for i, s in enumerate(s_list):
                p = jnp.exp(s - m_keys)
                lsum = lsum + jnp.sum(p, axis=-1, keepdims=True)
                pv = pv + _dot(p, v_get(i, h))
            l_s[h] = alpha * l_s[h] + lsum
            acc_s[h] = alpha * acc_s[h] + pv
            m_s[h] = m_new

    @pl.when(j < n_j - 1)
    def _():
        step(lambda i, h: head_rows(k_refs[i], h), lambda i, h: head_rows(v_refs[i], h), [None] * n_pp)

    @pl.when(j == n_j - 1)
    def _():
        k_get = lambda i, h: head_rows(k_refs[i], h) if i < n_pp else kpad_s[h]
        v_get = lambda i, h: head_rows(v_refs[i], h) if i < n_pp else vpad_s[h]
        step(k_get, v_get, [None] * (n_pp - 1) + [bias_ref[0], bias_ref[1]])
        lam = _lambda(lp_ref, li_ref)
        scale = 1.0 - li_ref[:, 0:1]
        for h in range(n_heads):
            on = acc_s[h] / l_s[h]
            o = on[0:t_new] - lam * on[t_new:rows]
            ms = jnp.mean(o * o, axis=-1, keepdims=True)
            o_ref[:, h * dv:(h + 1) * dv] = (o * lax.rsqrt(ms + RMS_EPS) * gain_ref[...] * scale).astype(o_ref.dtype)


def _decode(q, k_new, v_new, cache_k, cache_v, page_table, bias, lam_params, lam_init, out_gain, layer,
            pages_per_step):
    b, t_new, width = q.shape
    dv = 2 * HEAD_DIM
    n_heads = width // dv
    page = cache_k.shape[2] // n_heads
    n_pages = page_table.shape[1]
    n_pp = pages_per_step
    assert n_pages % n_pp == 0 and t_new <= page
    rows = 2 * t_new
    seq = lambda bi, j, pt: (bi, 0, 0)
    lay3 = lambda bi, j, pt: (layer, 0, 0)

    def page_spec(i):
        return pl.BlockSpec((None, None, page * n_heads, dv),
                            lambda bi, j, pt: (layer, pt[bi, j * n_pp + i], 0, 0))

    grid_spec = pltpu.PrefetchScalarGridSpec(
        num_scalar_prefetch=1,
        grid=(b, n_pages // n_pp),
        in_specs=[pl.BlockSpec((None, t_new, width), seq),
                  pl.BlockSpec((None, t_new, width), seq),
                  pl.BlockSpec((None, t_new, width), seq),
                  pl.BlockSpec(bias.shape, lambda bi, j, pt: (0, 0, 0)),
                  pl.BlockSpec((None,) + lam_params.shape[1:], lay3),
                  pl.BlockSpec((None, 1, LANES), lay3),
                  pl.BlockSpec((None, 1, dv), lay3)]
                 + [page_spec(i) for i in range(n_pp)] * 2,
        out_specs=pl.BlockSpec((None, t_new, width), seq),
        scratch_shapes=[pltpu.VMEM((n_heads, rows, dv), F32),
                        pltpu.VMEM((n_heads, rows, LANES), F32),
                        pltpu.VMEM((n_heads, rows, LANES), F32),
                        pltpu.VMEM((n_heads, rows, dv), F32),
                        pltpu.VMEM((n_heads, page, dv), F32),
                        pltpu.VMEM((n_heads, page, dv), F32)])
    return pl.pallas_call(
        functools.partial(_decode_kernel, n_pp, n_heads),
        grid_spec=grid_spec,
        out_shape=jax.ShapeDtypeStruct((b, t_new, width), BF16),
        compiler_params=pltpu.CompilerParams(
            dimension_semantics=("parallel", "arbitrary"), vmem_limit_bytes=VMEM_LIMIT),
        name="decode",
    )(page_table, q, k_new, v_new, bias, lam_params, lam_init, out_gain,
      *([cache_k] * n_pp), *([cache_v] * n_pp))


def _pad_lanes(a, width=LANES):
    return jnp.pad(a, [(0, 0)] * (a.ndim - 1) + [(0, width - a.shape[-1])])


def kernel(x_prompt, x_sample, cache_k, cache_v, state_gdn, state_conv, page_table, ffn1_norm, ffn1_w_gate_up, ffn1_w_down, mix_norm, w_in, conv_w, gdn_a_log, gdn_dt_bias, gdn_out_norm, diff_q_norm, diff_k_norm, diff_lambda, diff_out_norm, rel_bias, w_out, ffn2_norm, ffn2_w_gate_up, ffn2_w_down):
    depth, d_model, _ = w_in.shape
    n_gdn = gdn_a_log.shape[1]
    n_diff = rel_bias.shape[1]
    gdn_w = n_gdn * HEAD_DIM
    diff_w = n_diff * 2 * HEAD_DIM
    conv_dim = conv_w.shape[2]
    assert conv_dim == 3 * gdn_w and w_out.shape[1] == gdn_w + diff_w
    page = cache_k.shape[2]
    assert page >= MAX_DISTANCE
    past_len = page_table.shape[1] * page

    o = 0
    cols = {}
    for name, wd in (("qkv", conv_dim), ("gate", gdn_w), ("b", n_gdn), ("a", n_gdn),
                     ("dq", diff_w), ("dk", diff_w), ("dv", diff_w)):
        cols[name] = w_in[:, :, o:o + wd]
        o += wd
    w_in_r = jnp.concatenate([cols["qkv"], cols["gate"], cols["dq"], cols["dk"], cols["dv"],
                              _pad_lanes(cols["b"]), _pad_lanes(cols["a"])], axis=-1).astype(BF16)
    widths = (conv_dim, gdn_w, diff_w, 2 * LANES)
    bf = lambda a: a.astype(BF16)
    w1_gu, w1_d, w2_gu, w2_d, w_out_bf = bf(ffn1_w_gate_up), bf(ffn1_w_down), bf(ffn2_w_gate_up), bf(ffn2_w_down), bf(w_out)
    row3 = lambda a: a.reshape(depth, 1, -1)
    n1, n2, nm = row3(ffn1_norm), row3(ffn2_norm), row3(mix_norm)
    q_gain = row3(jnp.tile(diff_q_norm, (1, diff_w // HEAD_DIM)))
    k_gain = row3(jnp.tile(diff_k_norm, (1, diff_w // HEAD_DIM)))
    a_log = row3(_pad_lanes(gdn_a_log))
    dt_bias = row3(_pad_lanes(gdn_dt_bias))
    gdn_gain = row3(jnp.tile(gdn_out_norm, (1, n_gdn)))
    diff_gain = row3(diff_out_norm)
    lam_init = jnp.asarray([0.8 - 0.6 * math.exp(-0.3 * l) for l in range(depth)], F32)
    lam_init = jnp.broadcast_to(lam_init[:, None, None], (depth, 1, LANES))
    cache_k2 = cache_k.reshape(cache_k.shape[:2] + (page * n_diff, 2 * HEAD_DIM))
    cache_v2 = cache_v.reshape(cache_v.shape[:2] + (page * n_diff, 2 * HEAD_DIM))

    def bias_prompt(l):
        t = _attn_tile(l)
        sub = min(t, MAX_DISTANCE)
        assert t % sub == 0 and (l == t or sub == MAX_DISTANCE)
        i = jnp.arange(sub, dtype=jnp.int32)
        rel = i[:, None] - i[None, :]
        nb = t // sub
        d_blk = jnp.tile(_bias_tile(rel_bias, rel), (1, nb, nb))
        s_blk = jnp.tile(_bias_tile(rel_bias, rel + sub), (1, nb, nb))
        br = (jnp.arange(t, dtype=jnp.int32) // sub)[:, None]
        bc = (jnp.arange(t, dtype=jnp.int32) // sub)[None, :]
        diag = jnp.where(br == bc, d_blk, jnp.where(br == bc + 1, s_blk, jnp.where(br > bc, 0.0, -jnp.inf)))
        subd = jnp.where((br == 0) & (bc == nb - 1), s_blk, 0.0)
        return jnp.stack([diag, subd], axis=1)

    def bias_sample(t_new):
        tok = jnp.arange(t_new, dtype=jnp.int32)
        lane = jnp.arange(page, dtype=jnp.int32)
        rel_last = (page + tok)[:, None] - lane[None, :]
        rel_new = jnp.where(lane[None, :] < t_new, tok[:, None] - lane[None, :], -1)
        tiles = jnp.stack([_bias_tile(rel_bias, rel_last), _bias_tile(rel_bias, rel_new)], axis=0)
        tiles = jnp.broadcast_to(tiles[:, :, None], (2, n_diff, 2, t_new, page))
        return tiles.reshape(2, n_diff * 2 * t_new, page)

    def run(x, paged):
        b, l, _ = x.shape
        m = b * l
        xf = x.reshape(m, d_model)
        ks, vs, ss, cs = [], [], [], []
        if paged:
            bias = bias_sample(l)
        else:
            bias = bias_prompt(l)
        for layer in range(depth):
            xf = _ffn(xf, n1, w1_gu, w1_d, layer)
            qkv, gate, ba, q, k_rows, v_rows, k_bf, v_bf = _inproj(xf, nm, w_in_r, q_gain, k_gain, widths, layer)
            r3 = lambda a: a.reshape(b, l, a.shape[-1])
            if paged:
                conv_buf, s0, bb = state_conv[layer], state_gdn[layer], 4 if b % 4 == 0 else 1
            else:
                conv_buf = jnp.zeros((b, conv_w.shape[1] - 1, conv_dim), F32)
                s0 = jnp.zeros((b, n_gdn, HEAD_DIM, HEAD_DIM), F32)
                bb = b
            o_gdn, s_new, conv_new = _gdn(r3(qkv), r3(gate), r3(ba), conv_buf, s0, conv_w, a_log, dt_bias,
                                          gdn_gain, layer, bb)
            if paged:
                o_diff = _decode(r3(q), r3(k_rows), r3(v_rows), cache_k2, cache_v2, page_table, bias,
                                 diff_lambda, lam_init, diff_gain, layer, 8 if page_table.shape[1] % 8 == 0 else 1)
            else:
                o_diff = _attn(r3(q), r3(k_bf), r3(v_bf), bias, diff_lambda, lam_init, diff_gain, layer)
            xf = _ffn(xf, n2, w2_gu, w2_d, layer,
                      mix=(o_gdn.reshape(m, gdn_w), o_diff.reshape(m, diff_w), w_out_bf))
            ks.append(k_rows.reshape(b, l, n_diff, 2 * HEAD_DIM))
            vs.append(v_rows.reshape(b, l, n_diff, 2 * HEAD_DIM))
            ss.append(s_new)
            cs.append(conv_new)
        return xf.reshape(b, l, d_model), jnp.stack(ks), jnp.stack(vs), jnp.stack(ss), jnp.stack(cs)

    y_p, k_p, v_p, s_p, c_p = run(x_prompt, False)
    y_s, k_s, v_s, s_s, c_s = run(x_sample, True)
    return (y_p, y_s, k_p, v_p, s_p, c_p, k_s, v_s, s_s, c_s)
```

```python
import functools
import math

import jax
import jax.numpy as jnp
from jax import lax
from jax.experimental import pallas as pl
from jax.experimental.pallas import tpu as pltpu

F32 = jnp.float32
BF16 = jnp.bfloat16

HEAD_DIM = 64
GDN_CHUNK = 64
MAX_DISTANCE = 128
RMS_EPS = 1e-6
L2_EPS = 1e-6
LANES = 128
MXU_DIM = 256
VMEM_LIMIT = 52 * 1024 * 1024
NEG_BIG = -1e30
LOG2E = 1.4426950408889634


def _dot(a, b):
    return jnp.dot(a, b, preferred_element_type=F32)


def _dot_nt(a, b):
    return lax.dot_general(a, b, (((1,), (1,)), ((), ())), preferred_element_type=F32)


def _dot_tn(a, b):
    return lax.dot_general(a, b, (((0,), (0,)), ((), ())), preferred_element_type=F32)


def _split3(x):
    hi = x.astype(BF16)
    r = x - hi.astype(F32)
    mid = r.astype(BF16)
    lo = (r - mid.astype(F32)).astype(BF16)
    return hi, mid, lo


def _dot_exact_lhs(a_bf, b):
    b0, b1, b2 = _split3(b)
    return _dot(a_bf, b0) + _dot(a_bf, b1) + _dot(a_bf, b2)


def _group_sumsq(x, gmat):
    sq = x * x
    hi = sq.astype(BF16)
    lo = (sq - hi.astype(F32)).astype(BF16)
    return _dot(hi, gmat) + _dot(lo, gmat)


def _group_matrix(width):
    r = lax.broadcasted_iota(jnp.int32, (width, width), 0) // HEAD_DIM
    c = lax.broadcasted_iota(jnp.int32, (width, width), 1) // HEAD_DIM
    return jnp.where(r == c, 1.0, 0.0).astype(BF16)


def _ffn_kernel(has_mix, n_f, *refs):
    if has_mix:
        (x_ref, og_ref, od_ref, wo_ref, nw_ref, wg_ref, wu_ref, wd_ref,
         o_ref, h_s, acc_s, x_s) = refs
    else:
        x_ref, nw_ref, wg_ref, wu_ref, wd_ref, o_ref, h_s, acc_s, x_s = refs
    f = pl.program_id(1)

    @pl.when(f == 0)
    def _():
        x = x_ref[...]
        if has_mix:
            half = og_ref.shape[-1]
            x = x + _dot(og_ref[...], wo_ref[0:half, :]) + _dot(od_ref[...], wo_ref[half:, :])
        x_s[...] = x
        ms = jnp.mean(x * x, axis=-1, keepdims=True)
        h_s[...] = (x * lax.rsqrt(ms + RMS_EPS) * nw_ref[...]).astype(BF16)
        acc_s[...] = jnp.zeros_like(acc_s)

    h = h_s[...]
    g = _dot(h, wg_ref[...])
    u = _dot(h, wu_ref[...])
    a = (g * jax.nn.sigmoid(g)) * u
    acc_s[...] += _dot(a.astype(BF16), wd_ref[...])

    @pl.when(f == n_f - 1)
    def _():
        o_ref[...] = x_s[...] + 0.5 * acc_s[...]


def _ffn_tiles(m, d_ff):
    tm = 512 if m % 512 == 0 else m
    tf = d_ff // 2 if (d_ff // 2) % LANES == 0 else d_ff
    return tm, tf


def _ffn(x, norm_w, w_gu, w_down, layer, mix=None):
    m, d = x.shape
    d_ff = w_down.shape[1]
    tm, tf = _ffn_tiles(m, d_ff)
    n_f = d_ff // tf
    row = lambda i, f: (i, 0)
    in_specs = [pl.BlockSpec((tm, d), row)]
    args = [x]
    if mix is not None:
        og, od, w_out = mix
        in_specs += [pl.BlockSpec((tm, og.shape[1]), row), pl.BlockSpec((tm, od.shape[1]), row),
                     pl.BlockSpec((None,) + w_out.shape[1:], lambda i, f: (layer, 0, 0))]
        args += [og, od, w_out]
    in_specs += [
        pl.BlockSpec((None, 1, d), lambda i, f: (layer, 0, 0)),
        pl.BlockSpec((None, d, tf), lambda i, f: (layer, 0, f)),
        pl.BlockSpec((None, d, tf), lambda i, f: (layer, 0, f + n_f)),
        pl.BlockSpec((None, tf, d), lambda i, f: (layer, f, 0)),
    ]
    args += [norm_w, w_gu, w_gu, w_down]
    return pl.pallas_call(
        functools.partial(_ffn_kernel, mix is not None, n_f),
        grid=(m // tm, n_f),
        in_specs=in_specs,
        out_specs=pl.BlockSpec((tm, d), row),
        out_shape=jax.ShapeDtypeStruct((m, d), F32),
        scratch_shapes=[pltpu.VMEM((tm, d), BF16), pltpu.VMEM((tm, d), F32), pltpu.VMEM((tm, d), F32)],
        compiler_params=pltpu.CompilerParams(
            dimension_semantics=("parallel", "arbitrary"), vmem_limit_bytes=VMEM_LIMIT),
        name="ffn_mix" if mix is not None else "ffn",
    )(*args)


def _inproj_kernel(widths, x_ref, nw_ref, w_ref, qg_ref, kg_ref,
                   qkv_ref, gate_ref, ba_ref, q_ref, k_ref, v_ref, kb_ref, vb_ref):
    c_qkv, c_gate, c_d, c_ba = widths
    x = x_ref[...]
    ms = jnp.mean(x * x, axis=-1, keepdims=True)
    h = (x * lax.rsqrt(ms + RMS_EPS) * nw_ref[...]).astype(BF16)
    o = 0
    qkv_ref[...] = _dot(h, w_ref[:, o:o + c_qkv]); o += c_qkv
    gate_ref[...] = _dot(h, w_ref[:, o:o + c_gate]); o += c_gate
    dq = _dot(h, w_ref[:, o:o + c_d]); o += c_d
    dk = _dot(h, w_ref[:, o:o + c_d]); o += c_d
    dv = _dot(h, w_ref[:, o:o + c_d]); o += c_d
    ba_ref[...] = _dot(h, w_ref[:, o:o + c_ba])
    gmat = _group_matrix(c_d)
    inv_hd = 1.0 / HEAD_DIM
    qn = dq * lax.rsqrt(_group_sumsq(dq, gmat) * inv_hd + RMS_EPS) * qg_ref[...]
    kn = dk * lax.rsqrt(_group_sumsq(dk, gmat) * inv_hd + RMS_EPS) * kg_ref[...]
    q_ref[...] = qn * (HEAD_DIM ** -0.5 * LOG2E)
    k_ref[...] = kn
    v_ref[...] = dv
    kb_ref[...] = kn.astype(BF16)
    vb_ref[...] = dv.astype(BF16)


def _inproj(x, norm_w, w_in, q_gain, k_gain, widths, layer):
    m, d = x.shape
    c_qkv, c_gate, c_d, c_ba = widths
    tm = 512 if m % 512 == 0 else m
    row = lambda i: (i, 0)
    lay3 = lambda i: (layer, 0, 0)
    outs = [(c_qkv, F32), (c_gate, F32), (c_ba, F32), (c_d, F32), (c_d, F32), (c_d, F32), (c_d, BF16), (c_d, BF16)]
    return pl.pallas_call(
        functools.partial(_inproj_kernel, widths),
        grid=(m // tm,),
        in_specs=[pl.BlockSpec((tm, d), row),
                  pl.BlockSpec((None, 1, d), lay3),
                  pl.BlockSpec((None,) + w_in.shape[1:], lay3),
                  pl.BlockSpec((None, 1, c_d), lay3),
                  pl.BlockSpec((None, 1, c_d), lay3)],
        out_specs=[pl.BlockSpec((tm, c), row) for c, _ in outs],
        out_shape=[jax.ShapeDtypeStruct((m, c), dt) for c, dt in outs],
        compiler_params=pltpu.CompilerParams(
            dimension_semantics=("parallel",), vmem_limit_bytes=VMEM_LIMIT),
        name="inproj",
    )(x, norm_w, w_in, q_gain, k_gain)


def _block_mask(rows, cols, row_group, col_group):
    r = lax.broadcasted_iota(jnp.int32, (rows, cols), 0) // row_group
    c = lax.broadcasted_iota(jnp.int32, (rows, cols), 1) // col_group
    return r == c


def _block_diag(x, mask):
    reps = mask.shape[0] // x.shape[0]
    return jnp.where(mask, jnp.concatenate([x] * reps, axis=0), 0.0).astype(BF16)


def _dot_exact_rhs(a, b_bf):
    a0, a1, a2 = _split3(a)
    return _dot(a0, b_bf) + _dot(a1, b_bf) + _dot(a2, b_bf)


def _gdn_kernel(n_heads, chunk, gh, qkv_ref, gate_ref, ba_ref, cbuf_ref, s0_ref, cw_ref, alog_ref, dtb_ref,
                gain_ref, o_ref, snew_ref, cnew_ref, xp_s, s_s):
    bb = qkv_ref.shape[0]
    c = chunk
    d = HEAD_DIM
    width = n_heads * d
    n_grp = n_heads // gh
    tw = gh * c
    dw = gh * d
    n_tail = cbuf_ref.shape[1]
    n_taps = n_tail + 1
    top = 8 - n_tail
    t = pl.program_id(1)
    n_t = pl.num_programs(1)

    @pl.when(t == 0)
    def _():
        xp_s[:, top:8, :] = cbuf_ref[...]
        s_s[...] = jnp.zeros_like(s_s)
        for b in range(bb):
            for h in range(n_heads):
                o = (h % gh) * d
                s_s[b, h // gh, o:o + d, o:o + d] = s0_ref[b, h]

    xp_s[:, 8:8 + c, :] = qkv_ref[...]

    cw = cw_ref[...]
    gmat = _group_matrix(width)
    ri = lax.broadcasted_iota(jnp.int32, (c, c), 0)
    ci = lax.broadcasted_iota(jnp.int32, (c, c), 1)
    tril_bf = jnp.where(ri >= ci, 1.0, 0.0).astype(BF16)
    row_t = lax.broadcasted_iota(jnp.int32, (c, tw), 0)
    col_t = lax.broadcasted_iota(jnp.int32, (c, tw), 1) % c
    causal = row_t >= col_t
    strict = row_t > col_t
    eye = jnp.where(row_t == col_t, 1.0, 0.0).astype(F32)
    bd_tt = _block_mask(tw, tw, c, c)
    bd_td = _block_mask(tw, dw, c, d)
    bd_dd = _block_mask(dw, dw, d, d)
    gmat_g = jnp.where(bd_dd, 1.0, 0.0).astype(BF16)
    expand_d = jnp.where(_block_mask(LANES, width, 1, d), 1.0, 0.0).astype(BF16)
    expand_t = jnp.where(_block_mask(LANES, n_heads * c, 1, c), 1.0, 0.0).astype(BF16)
    neg_a = -jnp.exp(alog_ref[...])
    dtb = dtb_ref[...]
    gain = gain_ref[...]

    chains = []
    for b in range(bb):
        y = xp_s[b, top:top + c, :] * cw[0:1, :]
        for i in range(1, n_taps):
            y = y + xp_s[b, top + i:top + i + c, :] * cw[i:i + 1, :]
        qkv = y * jax.nn.sigmoid(y)
        q_all = qkv[:, 0:width]
        k_all = qkv[:, width:2 * width]
        v_all = qkv[:, 2 * width:3 * width]
        q_all = q_all * lax.rsqrt(_group_sumsq(q_all, gmat) + L2_EPS) * (HEAD_DIM ** -0.5)
        k_all = k_all * lax.rsqrt(_group_sumsq(k_all, gmat) + L2_EPS)
        ba = ba_ref[b]
        beta_all = jax.nn.sigmoid(ba[:, 0:LANES])
        z = ba[:, LANES:2 * LANES] + dtb
        softplus = jnp.maximum(z, 0.0) + jnp.log1p(jnp.exp(-jnp.abs(z)))
        g_all = neg_a * softplus
        gc_all = _dot_exact_lhs(tril_bf, g_all)
        gate_all = gate_ref[b]
        gate_all = gate_all * jax.nn.sigmoid(gate_all)
        beta_e = _dot_exact_rhs(beta_all, expand_d)
        gc_e = _dot_exact_rhs(gc_all, expand_d)
        g_t = _dot_exact_rhs(g_all, expand_t)
        egc_e = jnp.exp(gc_e)
        glast_e = gc_e[c - 1:c, :]
        kend_e = jnp.exp(glast_e - gc_e)
        sdec_e = jnp.exp(glast_e)

        for gi in range(n_grp):
            sl = slice(gi * dw, (gi + 1) * dw)
            k = k_all[:, sl]
            beta = beta_e[:, sl]
            chains.append(dict(b=b, gi=gi, sl=sl, q=q_all[:, sl], k=k, kb=k * beta, vb=v_all[:, sl] * beta,
                               egc=egc_e[:, sl], g_t=g_t[:, gi * tw:(gi + 1) * tw], kend=kend_e[:, sl],
                               sdec=sdec_e[:, sl], gate=gate_all[:, sl]))

    for ch in chains:
        dm = _dot_exact_lhs(tril_bf, jnp.where(strict, ch["g_t"], 0.0))
        ch["decay"] = jnp.where(causal, jnp.exp(dm), 0.0)
        ch["k_bd"] = _block_diag(ch["k"], bd_td)
    for ch in chains:
        ch["a"] = jnp.where(strict, _dot_nt(ch["kb"].astype(BF16), ch["k_bd"]) * ch["decay"], 0.0)
        ch["tinv"] = eye - jnp.where(row_t // 2 == col_t // 2, ch["a"], 0.0)
    size = 2
    while size < c:
        lower_left = ((row_t // (2 * size) == col_t // (2 * size))
                      & (row_t % (2 * size) >= size) & (col_t % (2 * size) < size))
        for ch in chains:
            a21 = _block_diag(jnp.where(lower_left, ch["a"], 0.0), bd_tt)
            ch["t2_a21"] = _dot(ch["tinv"].astype(BF16), a21)
        for ch in chains:
            ch["tinv"] = ch["tinv"] - _dot(ch["t2_a21"].astype(BF16), _block_diag(ch["tinv"], bd_tt))
        size *= 2
    for ch in chains:
        tinv_bf = ch["tinv"].astype(BF16)
        ch["u"] = _dot(tinv_bf, _block_diag(ch["vb"], bd_td))
        ch["w"] = _dot(tinv_bf, _block_diag(ch["kb"] * ch["egc"], bd_td))
        ch["qk"] = jnp.where(causal, _dot_nt(ch["q"].astype(BF16), ch["k_bd"]) * ch["decay"], 0.0)
    for ch in chains:
        ch["s"] = s_s[ch["b"], ch["gi"]]
        lhs = jnp.concatenate([ch["w"], ch["q"] * ch["egc"]], axis=0).astype(BF16)
        ch["ws"] = _dot(lhs, ch["s"].astype(BF16))
    for ch in chains:
        v_new = ch["u"] - ch["ws"][0:c]
        ch["o"] = ch["ws"][c:2 * c] + _dot(ch["qk"].astype(BF16), _block_diag(v_new, bd_td))
        k_end = ch["k"] * ch["kend"]
        cross = _dot_tn(k_end.astype(BF16), v_new.astype(BF16))
        s_s[ch["b"], ch["gi"]] = ch["s"] * ch["sdec"] + jnp.where(bd_dd, cross, 0.0)
    for ch in chains:
        o = ch["o"]
        ms = _group_sumsq(o, gmat_g) * (1.0 / d)
        on = o * lax.rsqrt(ms + RMS_EPS) * gain[:, ch["sl"]]
        o_ref[ch["b"], :, ch["sl"]] = (on * ch["gate"]).astype(o_ref.dtype)


    tail = xp_s[:, 8 + c - n_tail:8 + c, :]
    xp_s[:, top:8, :] = tail

    @pl.when(t == n_t - 1)
    def _():
        cnew_ref[...] = tail
        for b in range(bb):
            for h in range(n_heads):
                o = (h % gh) * d
                snew_ref[b, h] = s_s[b, h // gh, o:o + d, o:o + d]


def _gdn(qkv, gate, ba, conv_buf, s0, conv_w, a_log, dt_bias, out_gain, layer, batch_block):
    b, l, w3 = qkv.shape
    width = w3 // 3
    n_heads = width // HEAD_DIM
    chunk = min(GDN_CHUNK, l)
    assert l % chunk == 0 and chunk % 8 == 0 and chunk & (chunk - 1) == 0 and b % batch_block == 0
    n_tail = conv_buf.shape[1]
    assert n_tail <= min(8, chunk)
    bb = batch_block
    gh = min(n_heads, max(1, MXU_DIM // chunk))
    assert n_heads % gh == 0
    blk = lambda i, t: (i, t, 0)
    fix3 = lambda i, t: (i, 0, 0)
    lay3 = lambda i, t: (layer, 0, 0)
    return pl.pallas_call(
        functools.partial(_gdn_kernel, n_heads, chunk, gh),
        grid=(b // bb, l // chunk),
        in_specs=[pl.BlockSpec((bb, chunk, w3), blk),
                  pl.BlockSpec((bb, chunk, width), blk),
                  pl.BlockSpec((bb, chunk, ba.shape[2]), blk),
                  pl.BlockSpec((bb, n_tail, w3), fix3),
                  pl.BlockSpec((bb, n_heads, HEAD_DIM, HEAD_DIM), lambda i, t: (i, 0, 0, 0)),
                  pl.BlockSpec((None,) + conv_w.shape[1:], lay3),
                  pl.BlockSpec((None, 1, LANES), lay3),
                  pl.BlockSpec((None, 1, LANES), lay3),
                  pl.BlockSpec((None, 1, width), lay3)],
        out_specs=[pl.BlockSpec((bb, chunk, width), blk),
                   pl.BlockSpec((bb, n_heads, HEAD_DIM, HEAD_DIM), lambda i, t: (i, 0, 0, 0)),
                   pl.BlockSpec((bb, n_tail, w3), fix3)],
        out_shape=[jax.ShapeDtypeStruct((b, l, width), BF16),
                   jax.ShapeDtypeStruct(s0.shape, F32),
                   jax.ShapeDtypeStruct(conv_buf.shape, F32)],
        scratch_shapes=[pltpu.VMEM((bb, 8 + chunk, w3), F32),
                        pltpu.VMEM((bb, n_heads // gh, gh * HEAD_DIM, gh * HEAD_DIM), F32)],
        compiler_params=pltpu.CompilerParams(
            dimension_semantics=("parallel", "arbitrary"), vmem_limit_bytes=VMEM_LIMIT),
        name="gdn",
    )(qkv, gate, ba, conv_buf, s0, conv_w, a_log, dt_bias, out_gain)


def _rel_bucket(rel, n_buckets):
    n = jnp.maximum(rel, 0)
    max_exact = n_buckets // 2
    nf = jnp.maximum(n, 1).astype(F32)
    large = max_exact + (jnp.log(nf / max_exact) / math.log(MAX_DISTANCE / max_exact)
                         * (n_buckets - max_exact)).astype(jnp.int32)
    large = jnp.minimum(large, n_buckets - 1)
    return jnp.where(n < max_exact, n, large)


def _bias_tile(rel_bias, rel):
    n_buckets, n_heads = rel_bias.shape
    tab = rel_bias.astype(F32) - rel_bias[n_buckets - 1].astype(F32)[None, :]
    bucket = _rel_bucket(rel, n_buckets)[None]
    bias = jnp.zeros((n_heads,) + rel.shape, F32)
    for n in range(n_buckets):
        bias = jnp.where(bucket == n, tab[n].reshape((n_heads,) + (1,) * rel.ndim), bias)
    return jnp.where((rel >= 0)[None], bias * LOG2E, -jnp.inf)


def _lambda(lp_ref, li_ref):
    lp = lp_ref[...]
    e1 = jnp.exp(jnp.sum(lp[0:1, :] * lp[1:2, :], axis=-1, keepdims=True))
    e2 = jnp.exp(jnp.sum(lp[2:3, :] * lp[3:4, :], axis=-1, keepdims=True))
    return e1 - e2 + li_ref[:, 0:1]


def _attn_kernel(tk, sub, q_ref, k_ref, v_ref, bias_ref, lp_ref, li_ref, gain_ref, o_ref,
                 q_s, m_s, acc_s, vx_s, s0_s, s1_s):
    tq = q_ref.shape[0]
    dv = 2 * HEAD_DIM
    qi = pl.program_id(2)

    @pl.when(qi == 0)
    def _():
        vx_s[:, 0:dv] = v_ref[...]
        vx_s[:, dv:2 * dv] = jnp.ones((vx_s.shape[0], dv), BF16)

    q = q_ref[...]
    lane = lax.broadcasted_iota(jnp.int32, q.shape, 1)
    q_s[0:tq, :] = jnp.where(lane < HEAD_DIM, q, 0.0).astype(BF16)
    q_s[tq:2 * tq, :] = jnp.where(lane >= HEAD_DIM, q, 0.0).astype(BF16)
    m_s[...] = jnp.full_like(m_s, NEG_BIG)
    acc_s[...] = jnp.zeros_like(acc_s)

    def scores(buf, j):
        ks = pl.multiple_of(j * tk, tk)
        buf[...] = _dot_nt(q_s[...], k_ref[pl.ds(ks, tk), :])

    def update(buf, j):
        ks = pl.multiple_of(j * tk, tk)
        m_prev = m_s[...]
        m_new = jnp.maximum(m_prev, jnp.max(buf[...], axis=-1, keepdims=True))
        p = jnp.exp2(buf[...] - jnp.concatenate([m_new] * (tk // LANES), axis=1)).astype(BF16)
        alpha = jnp.exp2(m_prev - m_new)
        pv = _dot(p, vx_s[pl.ds(ks, tk), :])
        acc_s[...] = jnp.concatenate([alpha] * (2 * dv // LANES), axis=1) * acc_s[...] + pv
        m_s[...] = m_new

    def below_diagonal_bias(buf):
        for m in range(2):
            buf[m * tq:m * tq + sub, tk - sub:tk] += bias_ref[1]

    def diagonal_bias(buf):
        for m in range(2):
            for r in range(tq // sub):
                rows = slice(m * tq + r * sub, m * tq + (r + 1) * sub)
                if r >= 1:
                    buf[rows, (r - 1) * sub:r * sub] += bias_ref[1]
                buf[rows, r * sub:(r + 1) * sub] += bias_ref[0]
                if (r + 1) * sub < tk:
                    buf[rows, (r + 1) * sub:tk] = jnp.full((sub, tk - (r + 1) * sub), -jnp.inf, F32)

    n_pairs = jnp.maximum(qi - 1, 0) // 2
    scores(s0_s, 0)

    def pair(i, carry):
        scores(s1_s, 2 * i + 1)
        update(s0_s, 2 * i)
        scores(s0_s, 2 * i + 2)
        update(s1_s, 2 * i + 1)
        return carry

    lax.fori_loop(0, n_pairs, pair, 0)

    @pl.when(qi == 0)
    def _():
        diagonal_bias(s0_s)
        update(s0_s, 0)

    @pl.when(qi % 2 == 1)
    def _():
        scores(s1_s, qi)
        below_diagonal_bias(s0_s)
        diagonal_bias(s1_s)
        update(s0_s, qi - 1)
        update(s1_s, qi)

    @pl.when((qi % 2 == 0) & (qi >= 2))
    def _():
        scores(s1_s, qi - 1)
        below_diagonal_bias(s1_s)
        update(s0_s, qi - 2)
        scores(s0_s, qi)
        diagonal_bias(s0_s)
        update(s1_s, qi - 1)
        update(s0_s, qi)

    lam = _lambda(lp_ref, li_ref)
    acc = acc_s[...]
    on = acc[:, 0:dv] * (1.0 / acc[:, dv:2 * dv])
    o = on[0:tq] - lam * on[tq:2 * tq]
    ms = jnp.mean(o * o, axis=-1, keepdims=True)
    o_ref[...] = (o * lax.rsqrt(ms + RMS_EPS) * gain_ref[...] * (1.0 - li_ref[:, 0:1])).astype(o_ref.dtype)


def _attn_tile(l):
    return 512 if l % 512 == 0 else l


def _attn(q, k_bf, v_bf, bias, lam_params, lam_init, out_gain, layer):
    b, l, width = q.shape
    dv = 2 * HEAD_DIM
    assert dv == LANES
    n_heads = width // dv
    tq = tk = _attn_tile(l)
    sub = bias.shape[-1]
    lay3 = lambda bi, h, qi: (layer, 0, 0)
    return pl.pallas_call(
        functools.partial(_attn_kernel, tk, sub),
        grid=(b, n_heads, l // tq),
        in_specs=[pl.BlockSpec((None, tq, dv), lambda bi, h, qi: (bi, qi, h)),
                  pl.BlockSpec((None, l, dv), lambda bi, h, qi: (bi, 0, h)),
                  pl.BlockSpec((None, l, dv), lambda bi, h, qi: (bi, 0, h)),
                  pl.BlockSpec((None, 2, sub, sub), lambda bi, h, qi: (h, 0, 0, 0)),
                  pl.BlockSpec((None,) + lam_params.shape[1:], lay3),
                  pl.BlockSpec((None, 1, LANES), lay3),
                  pl.BlockSpec((None, 1, dv), lay3)],
        out_specs=pl.BlockSpec((None, tq, dv), lambda bi, h, qi: (bi, qi, h)),
        out_shape=jax.ShapeDtypeStruct((b, l, width), BF16),
        scratch_shapes=[pltpu.VMEM((2 * tq, dv), BF16),
                        pltpu.VMEM((2 * tq, LANES), F32),
                        pltpu.VMEM((2 * tq, 2 * dv), F32),
                        pltpu.VMEM((l, 2 * dv), BF16),
                        pltpu.VMEM((2 * tq, tk), F32),
                        pltpu.VMEM((2 * tq, tk), F32)],
        compiler_params=pltpu.CompilerParams(
            dimension_semantics=("parallel", "parallel", "arbitrary"), vmem_limit_bytes=VMEM_LIMIT),
        name="attn",
    )(q, k_bf, v_bf, bias, lam_params, lam_init, out_gain)


def _decode_kernel(n_pp, n_heads, pt_ref, q_ref, kn_ref, vn_ref, bias_ref, lp_ref, li_ref, gain_ref, *rest):
    k_refs = rest[:n_pp]
    v_refs = rest[n_pp:2 * n_pp]
    o_ref, q_s, m_s, l_s, acc_s, kpad_s, vpad_s = rest[2 * n_pp:]
    t_new = q_ref.shape[0]
    page = k_refs[0].shape[0] // n_heads
    dv = 2 * HEAD_DIM
    rows = 2 * t_new
    j = pl.program_id(1)
    n_j = pl.num_programs(1)

    @pl.when(j == 0)
    def _():
        m_s[...] = jnp.full_like(m_s, NEG_BIG)
        l_s[...] = jnp.zeros_like(l_s)
        acc_s[...] = jnp.zeros_like(acc_s)
        kpad_s[...] = jnp.zeros_like(kpad_s)
        vpad_s[...] = jnp.zeros_like(vpad_s)
        lane = lax.broadcasted_iota(jnp.int32, (t_new, dv), 1)
        for h in range(n_heads):
            cs = slice(h * dv, (h + 1) * dv)
            qh = q_ref[:, cs]
            q_s[h, 0:t_new, :] = jnp.where(lane < HEAD_DIM, qh, 0.0)
            q_s[h, t_new:rows, :] = jnp.where(lane >= HEAD_DIM, qh, 0.0)
            kpad_s[h, 0:t_new, :] = kn_ref[:, cs]
            vpad_s[h, 0:t_new, :] = vn_ref[:, cs]

    def head_rows(ref, h):
        return ref[pl.ds(h, page, stride=n_heads), :]

    def step(k_get, v_get, bias_list):
        heads = range(n_heads)
        s_all, m_all, p_all = [], [], []
        for h in heads:
            qh = q_s[h]
            s_list = []
            for i, bb in enumerate(bias_list):
                s = _dot_nt(qh, k_get(i, h))
                s_list.append(s if bb is None else s + bb[h * rows:(h + 1) * rows, :])
            s_all.append(s_list)
        for h in heads:
            m_cur = s_all[h][0]
            for s in s_all[h][1:]:
                m_cur = jnp.maximum(m_cur, s)
            m_all.append(jnp.maximum(m_s[h], m_cur.max(axis=-1, keepdims=True)))
        for h in heads:
            m_keys = m_all[h] if page == LANES else m_all[h][:, 0:1]
            p_all.append([jnp.exp2(s - m_keys) for s in s_all[h]])
        for h in heads:
            alpha = jnp.exp2(m_s[h] - m_all[h])
            psum = p_all[h][0]
            pv = _dot(p_all[h][0], v_get(0, h))
            for i in range(1, len(p_all[h])):
                psum = psum + p_all[h][i]
                pv = pv + _dot(p_all[h][i], v_get(i, h))
            l_s[h] = alpha * l_s[h] + jnp.sum(psum, axis=-1, keepdims=True)
            acc_s[h] = alpha * acc_s[h] + pv
            m_s[h] = m_all[h]

    @pl.when(j < n_j - 1)
    def _():
        step(lambda i, h: head_rows(k_refs[i], h), lambda i, h: head_rows(v_refs[i], h), [None] * n_pp)

    @pl.when(j == n_j - 1)
    def _():
        k_get = lambda i, h: head_rows(k_refs[i], h) if i < n_pp else kpad_s[h]
        v_get = lambda i, h: head_rows(v_refs[i], h) if i < n_pp else vpad_s[h]
        step(k_get, v_get, [None] * (n_pp - 1) + [bias_ref[0], bias_ref[1]])
        lam = _lambda(lp_ref, li_ref)
        scale = 1.0 - li_ref[:, 0:1]
        for h in range(n_heads):
            on = acc_s[h] / l_s[h]
            o = on[0:t_new] - lam * on[t_new:rows]
            ms = jnp.mean(o * o, axis=-1, keepdims=True)
            o_ref[:, h * dv:(h + 1) * dv] = (o * lax.rsqrt(ms + RMS_EPS) * gain_ref[...] * scale).astype(o_ref.dtype)


def _decode(q, k_new, v_new, cache_k, cache_v, page_table, bias, lam_params, lam_init, out_gain, layer,
            pages_per_step):
    b, t_new, width = q.shape
    dv = 2 * HEAD_DIM
    n_heads = width // dv
    page = cache_k.shape[2] // n_heads
    n_pages = page_table.shape[1]
    n_pp = pages_per_step
    assert n_pages % n_pp == 0 and t_new <= page
    rows = 2 * t_new
    seq = lambda bi, j, pt: (bi, 0, 0)
    lay3 = lambda bi, j, pt: (layer, 0, 0)

    def page_spec(i):
        return pl.BlockSpec((None, None, page * n_heads, dv),
                            lambda bi, j, pt: (layer, pt[bi, j * n_pp + i], 0, 0))

    grid_spec = pltpu.PrefetchScalarGridSpec(
        num_scalar_prefetch=1,
        grid=(b, n_pages // n_pp),
        in_specs=[pl.BlockSpec((None, t_new, width), seq),
                  pl.BlockSpec((None, t_new, width), seq),
                  pl.BlockSpec((None, t_new, width), seq),
                  pl.BlockSpec(bias.shape, lambda bi, j, pt: (0, 0, 0)),
                  pl.BlockSpec((None,) + lam_params.shape[1:], lay3),
                  pl.BlockSpec((None, 1, LANES), lay3),
                  pl.BlockSpec((None, 1, dv), lay3)]
                 + [page_spec(i) for i in range(n_pp)] * 2,
        out_specs=pl.BlockSpec((None, t_new, width), seq),
        scratch_shapes=[pltpu.VMEM((n_heads, rows, dv), F32),
                        pltpu.VMEM((n_heads, rows, LANES), F32),
                        pltpu.VMEM((n_heads, rows, LANES), F32),
                        pltpu.VMEM((n_heads, rows, dv), F32),
                        pltpu.VMEM((n_heads, page, dv), F32),
                        pltpu.VMEM((n_heads, page, dv), F32)])
    return pl.pallas_call(
        functools.partial(_decode_kernel, n_pp, n_heads),
        grid_spec=grid_spec,
        out_shape=jax.ShapeDtypeStruct((b, t_new, width), BF16),
        compiler_params=pltpu.CompilerParams(
            dimension_semantics=("parallel", "arbitrary"), vmem_limit_bytes=VMEM_LIMIT),
        name="decode",
    )(page_table, q, k_new, v_new, bias, lam_params, lam_init, out_gain,
      *([cache_k] * n_pp), *([cache_v] * n_pp))


def _pad_lanes(a, width=LANES):
    return jnp.pad(a, [(0, 0)] * (a.ndim - 1) + [(0, width - a.shape[-1])])


def kernel(x_prompt, x_sample, cache_k, cache_v, state_gdn, state_conv, page_table, ffn1_norm, ffn1_w_gate_up, ffn1_w_down, mix_norm, w_in, conv_w, gdn_a_log, gdn_dt_bias, gdn_out_norm, diff_q_norm, diff_k_norm, diff_lambda, diff_out_norm, rel_bias, w_out, ffn2_norm, ffn2_w_gate_up, ffn2_w_down):
    depth, d_model, _ = w_in.shape
    n_gdn = gdn_a_log.shape[1]
    n_diff = rel_bias.shape[1]
    gdn_w = n_gdn * HEAD_DIM
    diff_w = n_diff * 2 * HEAD_DIM
    conv_dim = conv_w.shape[2]
    assert conv_dim == 3 * gdn_w and w_out.shape[1] == gdn_w + diff_w
    page = cache_k.shape[2]
    assert page >= MAX_DISTANCE
    past_len = page_table.shape[1] * page

    o = 0
    cols = {}
    for name, wd in (("qkv", conv_dim), ("gate", gdn_w), ("b", n_gdn), ("a", n_gdn),
                     ("dq", diff_w), ("dk", diff_w), ("dv", diff_w)):
        cols[name] = w_in[:, :, o:o + wd]
        o += wd
    w_in_r = jnp.concatenate([cols["qkv"], cols["gate"], cols["dq"], cols["dk"], cols["dv"],
                              _pad_lanes(cols["b"]), _pad_lanes(cols["a"])], axis=-1).astype(BF16)
    widths = (conv_dim, gdn_w, diff_w, 2 * LANES)
    bf = lambda a: a.astype(BF16)
    w1_gu, w1_d, w2_gu, w2_d, w_out_bf = bf(ffn1_w_gate_up), bf(ffn1_w_down), bf(ffn2_w_gate_up), bf(ffn2_w_down), bf(w_out)
    row3 = lambda a: a.reshape(depth, 1, -1)
    n1, n2, nm = row3(ffn1_norm), row3(ffn2_norm), row3(mix_norm)
    q_gain = row3(jnp.tile(diff_q_norm, (1, diff_w // HEAD_DIM)))
    k_gain = row3(jnp.tile(diff_k_norm, (1, diff_w // HEAD_DIM)))
    a_log = row3(_pad_lanes(gdn_a_log))
    dt_bias = row3(_pad_lanes(gdn_dt_bias))
    gdn_gain = row3(jnp.tile(gdn_out_norm, (1, n_gdn)))
    diff_gain = row3(diff_out_norm)
    lam_init = jnp.asarray([0.8 - 0.6 * math.exp(-0.3 * l) for l in range(depth)], F32)
    lam_init = jnp.broadcast_to(lam_init[:, None, None], (depth, 1, LANES))
    cache_k2 = cache_k.reshape(cache_k.shape[:2] + (page * n_diff, 2 * HEAD_DIM))
    cache_v2 = cache_v.reshape(cache_v.shape[:2] + (page * n_diff, 2 * HEAD_DIM))

    def bias_prompt(l):
        t = _attn_tile(l)
        sub = min(t, MAX_DISTANCE)
        assert t % sub == 0 and (l == t or sub == MAX_DISTANCE)
        i = jnp.arange(sub, dtype=jnp.int32)
        rel = i[:, None] - i[None, :]
        return jnp.stack([_bias_tile(rel_bias, rel), _bias_tile(rel_bias, rel + sub)], axis=1)

    def bias_sample(t_new):
        tok = jnp.arange(t_new, dtype=jnp.int32)
        lane = jnp.arange(page, dtype=jnp.int32)
        rel_last = (page + tok)[:, None] - lane[None, :]
        rel_new = jnp.where(lane[None, :] < t_new, tok[:, None] - lane[None, :], -1)
        tiles = jnp.stack([_bias_tile(rel_bias, rel_last), _bias_tile(rel_bias, rel_new)], axis=0)
        tiles = jnp.broadcast_to(tiles[:, :, None], (2, n_diff, 2, t_new, page))
        return tiles.reshape(2, n_diff * 2 * t_new, page)

    def run(x, paged):
        b, l, _ = x.shape
        m = b * l
        xf = x.reshape(m, d_model)
        ks, vs, ss, cs = [], [], [], []
        if paged:
            bias = bias_sample(l)
        else:
            bias = bias_prompt(l)
        for layer in range(depth):
            xf = _ffn(xf, n1, w1_gu, w1_d, layer)
            qkv, gate, ba, q, k_rows, v_rows, k_bf, v_bf = _inproj(xf, nm, w_in_r, q_gain, k_gain, widths, layer)
            r3 = lambda a: a.reshape(b, l, a.shape[-1])
            if paged:
                conv_buf, s0, bb = state_conv[layer], state_gdn[layer], 4 if b % 4 == 0 else 1
            else:
                conv_buf = jnp.zeros((b, conv_w.shape[1] - 1, conv_dim), F32)
                s0 = jnp.zeros((b, n_gdn, HEAD_DIM, HEAD_DIM), F32)
                bb = b
            o_gdn, s_new, conv_new = _gdn(r3(qkv), r3(gate), r3(ba), conv_buf, s0, conv_w, a_log, dt_bias,
                                          gdn_gain, layer, bb)
            if paged:
                o_diff = _decode(r3(q), r3(k_rows), r3(v_rows), cache_k2, cache_v2, page_table, bias,
                                 diff_lambda, lam_init, diff_gain, layer, 8 if page_table.shape[1] % 8 == 0 else 1)
            else:
                o_diff = _attn(r3(q), r3(k_bf), r3(v_bf), bias, diff_lambda, lam_init, diff_gain, layer)
            xf = _ffn(xf, n2, w2_gu, w2_d, layer,
                      mix=(o_gdn.reshape(m, gdn_w), o_diff.reshape(m, diff_w), w_out_bf))
            ks.append(k_rows.reshape(b, l, n_diff, 2 * HEAD_DIM))
            vs.append(v_rows.reshape(b, l, n_diff, 2 * HEAD_DIM))
            ss.append(s_new)
            cs.append(conv_new)
        return xf.reshape(b, l, d_model), jnp.stack(ks), jnp.stack(vs), jnp.stack(ss), jnp.stack(cs)

    y_p, k_p, v_p, s_p, c_p = run(x_prompt, False)
    y_s, k_s, v_s, s_s, c_s = run(x_sample, True)
    return (y_p, y_s, k_p, v_p, s_p, c_p, k_s, v_s, s_s, c_s)
```

```python
import functools
import math

import jax
import jax.numpy as jnp
from jax import lax
from jax.experimental import pallas as pl
from jax.experimental.pallas import tpu as pltpu

F32 = jnp.float32
BF16 = jnp.bfloat16

HEAD_DIM = 64
GDN_CHUNK = 64
MAX_DISTANCE = 128
RMS_EPS = 1e-6
L2_EPS = 1e-6
LANES = 128
MXU_DIM = 256
VMEM_LIMIT = 52 * 1024 * 1024
NEG_BIG = -1e30
LOG2E = 1.4426950408889634


def _dot(a, b):
    return jnp.dot(a, b, preferred_element_type=F32)


def _dot_nt(a, b):
    return lax.dot_general(a, b, (((1,), (1,)), ((), ())), preferred_element_type=F32)


def _dot_tn(a, b):
    return lax.dot_general(a, b, (((0,), (0,)), ((), ())), preferred_element_type=F32)


def _split3(x):
    hi = x.astype(BF16)
    r = x - hi.astype(F32)
    mid = r.astype(BF16)
    lo = (r - mid.astype(F32)).astype(BF16)
    return hi, mid, lo


def _dot_exact_lhs(a_bf, b):
    b0, b1, b2 = _split3(b)
    return _dot(a_bf, b0) + _dot(a_bf, b1) + _dot(a_bf, b2)


def _group_sumsq(x, gmat):
    sq = x * x
    hi = sq.astype(BF16)
    lo = (sq - hi.astype(F32)).astype(BF16)
    return _dot(hi, gmat) + _dot(lo, gmat)


def _group_matrix(width):
    r = lax.broadcasted_iota(jnp.int32, (width, width), 0) // HEAD_DIM
    c = lax.broadcasted_iota(jnp.int32, (width, width), 1) // HEAD_DIM
    return jnp.where(r == c, 1.0, 0.0).astype(BF16)


def _ffn_kernel(has_mix, n_f, *refs):
    if has_mix:
        (x_ref, og_ref, od_ref, wo_ref, nw_ref, wg_ref, wu_ref, wd_ref,
         o_ref, h_s, acc_s, x_s) = refs
    else:
        x_ref, nw_ref, wg_ref, wu_ref, wd_ref, o_ref, h_s, acc_s, x_s = refs
    f = pl.program_id(1)

    @pl.when(f == 0)
    def _():
        x = x_ref[...]
        if has_mix:
            half = og_ref.shape[-1]
            x = x + _dot(og_ref[...], wo_ref[0:half, :]) + _dot(od_ref[...], wo_ref[half:, :])
        x_s[...] = x
        ms = jnp.mean(x * x, axis=-1, keepdims=True)
        h_s[...] = (x * lax.rsqrt(ms + RMS_EPS) * nw_ref[...]).astype(BF16)
        acc_s[...] = jnp.zeros_like(acc_s)

    h = h_s[...]
    g = _dot(h, wg_ref[...])
    u = _dot(h, wu_ref[...])
    a = (g * jax.nn.sigmoid(g)) * u
    acc_s[...] += _dot(a.astype(BF16), wd_ref[...])

    @pl.when(f == n_f - 1)
    def _():
        o_ref[...] = x_s[...] + 0.5 * acc_s[...]


def _ffn_tiles(m, d_ff):
    tm = 512 if m % 512 == 0 else m
    tf = d_ff // 2 if (d_ff // 2) % LANES == 0 else d_ff
    return tm, tf


def _ffn(x, norm_w, w_gu, w_down, layer, mix=None):
    m, d = x.shape
    d_ff = w_down.shape[1]
    tm, tf = _ffn_tiles(m, d_ff)
    n_f = d_ff // tf
    row = lambda i, f: (i, 0)
    in_specs = [pl.BlockSpec((tm, d), row)]
    args = [x]
    if mix is not None:
        og, od, w_out = mix
        in_specs += [pl.BlockSpec((tm, og.shape[1]), row), pl.BlockSpec((tm, od.shape[1]), row),
                     pl.BlockSpec((None,) + w_out.shape[1:], lambda i, f: (layer, 0, 0))]
        args += [og, od, w_out]
    in_specs += [
        pl.BlockSpec((None, 1, d), lambda i, f: (layer, 0, 0)),
        pl.BlockSpec((None, d, tf), lambda i, f: (layer, 0, f)),
        pl.BlockSpec((None, d, tf), lambda i, f: (layer, 0, f + n_f)),
        pl.BlockSpec((None, tf, d), lambda i, f: (layer, f, 0)),
    ]
    args += [norm_w, w_gu, w_gu, w_down]
    return pl.pallas_call(
        functools.partial(_ffn_kernel, mix is not None, n_f),
        grid=(m // tm, n_f),
        in_specs=in_specs,
        out_specs=pl.BlockSpec((tm, d), row),
        out_shape=jax.ShapeDtypeStruct((m, d), F32),
        scratch_shapes=[pltpu.VMEM((tm, d), BF16), pltpu.VMEM((tm, d), F32), pltpu.VMEM((tm, d), F32)],
        compiler_params=pltpu.CompilerParams(
            dimension_semantics=("parallel", "arbitrary"), vmem_limit_bytes=VMEM_LIMIT),
        name="ffn_mix" if mix is not None else "ffn",
    )(*args)


def _inproj_kernel(widths, x_ref, nw_ref, w_ref, qg_ref, kg_ref, k_prev_ref, v_prev_ref,
                   qkv_ref, gate_ref, ba_ref, q_ref, k_ref, v_ref, kb_ref, vb_ref):
    del k_prev_ref, v_prev_ref
    c_qkv, c_gate, c_d, c_ba = widths
    x = x_ref[...]
    ms = jnp.mean(x * x, axis=-1, keepdims=True)
    h = (x * lax.rsqrt(ms + RMS_EPS) * nw_ref[...]).astype(BF16)
    o = 0
    qkv_ref[...] = _dot(h, w_ref[:, o:o + c_qkv]); o += c_qkv
    gate_ref[...] = _dot(h, w_ref[:, o:o + c_gate]); o += c_gate
    dq = _dot(h, w_ref[:, o:o + c_d]); o += c_d
    dk = _dot(h, w_ref[:, o:o + c_d]); o += c_d
    dv = _dot(h, w_ref[:, o:o + c_d]); o += c_d
    ba_ref[...] = _dot(h, w_ref[:, o:o + c_ba])
    gmat = _group_matrix(c_d)
    inv_hd = 1.0 / HEAD_DIM
    qn = dq * lax.rsqrt(_group_sumsq(dq, gmat) * inv_hd + RMS_EPS) * qg_ref[...]
    kn = dk * lax.rsqrt(_group_sumsq(dk, gmat) * inv_hd + RMS_EPS) * kg_ref[...]
    q_ref[...] = qn * (HEAD_DIM ** -0.5 * LOG2E)
    tm = kn.shape[0]
    dv_w = 2 * HEAD_DIM
    n_heads = c_d // dv_w
    for h in range(n_heads):
        k_ref[pl.ds(h, tm, stride=n_heads), :] = kn[:, h * dv_w:(h + 1) * dv_w]
        v_ref[pl.ds(h, tm, stride=n_heads), :] = dv[:, h * dv_w:(h + 1) * dv_w]
    kb_ref[...] = kn.astype(BF16)
    vb_ref[...] = dv.astype(BF16)


def _inproj(x, norm_w, w_in, q_gain, k_gain, widths, layer, k_all, v_all):
    m, d = x.shape
    c_qkv, c_gate, c_d, c_ba = widths
    dv_w = 2 * HEAD_DIM
    n_heads = c_d // dv_w
    tm = 512 if m % 512 == 0 else m
    row = lambda i: (i, 0)
    lay3 = lambda i: (layer, 0, 0)
    slab = pl.BlockSpec((None, tm * n_heads, dv_w), lambda i: (layer, i, 0))
    outs = [(c_qkv, F32), (c_gate, F32), (c_ba, F32), (c_d, F32), None, None, (c_d, BF16), (c_d, BF16)]
    return pl.pallas_call(
        functools.partial(_inproj_kernel, widths),
        grid=(m // tm,),
        in_specs=[pl.BlockSpec((tm, d), row),
                  pl.BlockSpec((None, 1, d), lay3),
                  pl.BlockSpec((None,) + w_in.shape[1:], lay3),
                  pl.BlockSpec((None, 1, c_d), lay3),
                  pl.BlockSpec((None, 1, c_d), lay3),
                  pl.BlockSpec(memory_space=pl.ANY),
                  pl.BlockSpec(memory_space=pl.ANY)],
        out_specs=[slab if o is None else pl.BlockSpec((tm, o[0]), row) for o in outs],
        out_shape=[jax.ShapeDtypeStruct(k_all.shape, F32) if o is None else jax.ShapeDtypeStruct((m, o[0]), o[1])
                   for o in outs],
        input_output_aliases={5: 4, 6: 5},
        compiler_params=pltpu.CompilerParams(
            dimension_semantics=("parallel",), vmem_limit_bytes=VMEM_LIMIT),
        name="inproj",
    )(x, norm_w, w_in, q_gain, k_gain, k_all, v_all)


def _block_mask(rows, cols, row_group, col_group):
    r = lax.broadcasted_iota(jnp.int32, (rows, cols), 0) // row_group
    c = lax.broadcasted_iota(jnp.int32, (rows, cols), 1) // col_group
    return r == c


def _block_diag(x, mask):
    reps = mask.shape[0] // x.shape[0]
    return jnp.where(mask, jnp.concatenate([x] * reps, axis=0), 0.0).astype(BF16)


def _dot_exact_rhs(a, b_bf):
    a0, a1, a2 = _split3(a)
    return _dot(a0, b_bf) + _dot(a1, b_bf) + _dot(a2, b_bf)


def _gdn_kernel(n_heads, chunk, gh, nc, qkv_ref, gate_ref, ba_ref, cbuf_ref, s0_ref, cw_ref, alog_ref, dtb_ref,
                gain_ref, o_ref, snew_ref, cnew_ref, xp_s, s_s):
    bb = qkv_ref.shape[0]
    c = chunk
    tb = nc * c
    d = HEAD_DIM
    width = n_heads * d
    n_grp = n_heads // gh
    tw = gh * c
    dw = gh * d
    n_tail = cbuf_ref.shape[1]
    n_taps = n_tail + 1
    top = 8 - n_tail
    t = pl.program_id(1)
    n_t = pl.num_programs(1)

    @pl.when(t == 0)
    def _():
        xp_s[:, top:8, :] = cbuf_ref[...]
        s_s[...] = jnp.zeros_like(s_s)
        for b in range(bb):
            for h in range(n_heads):
                o = (h % gh) * d
                s_s[b, h // gh, o:o + d, o:o + d] = s0_ref[b, h]

    xp_s[:, 8:8 + tb, :] = qkv_ref[...]

    cw = cw_ref[...]
    gmat = _group_matrix(width)
    ri = lax.broadcasted_iota(jnp.int32, (c, c), 0)
    ci = lax.broadcasted_iota(jnp.int32, (c, c), 1)
    tril_bf = jnp.where(ri >= ci, 1.0, 0.0).astype(BF16)
    rb = lax.broadcasted_iota(jnp.int32, (tb, tb), 0)
    cb = lax.broadcasted_iota(jnp.int32, (tb, tb), 1)
    tril_chunks = jnp.where((rb >= cb) & (rb // c == cb // c), 1.0, 0.0).astype(BF16)
    row_t = lax.broadcasted_iota(jnp.int32, (c, tw), 0)
    col_t = lax.broadcasted_iota(jnp.int32, (c, tw), 1) % c
    causal = row_t >= col_t
    strict = row_t > col_t
    eye = jnp.where(row_t == col_t, 1.0, 0.0).astype(F32)
    bd_tt = _block_mask(tw, tw, c, c)
    bd_td = _block_mask(tw, dw, c, d)
    bd_dd = _block_mask(dw, dw, d, d)
    gmat_g = jnp.where(bd_dd, 1.0, 0.0).astype(BF16)
    expand_d = jnp.where(_block_mask(LANES, width, 1, d), 1.0, 0.0).astype(BF16)
    expand_t = jnp.where(_block_mask(LANES, n_heads * c, 1, c), 1.0, 0.0).astype(BF16)
    neg_a = -jnp.exp(alog_ref[...])
    dtb = dtb_ref[...]
    gain = gain_ref[...]

    chains = []
    for b in range(bb):
        y = xp_s[b, top:top + tb, :] * cw[0:1, :]
        for i in range(1, n_taps):
            y = y + xp_s[b, top + i:top + i + tb, :] * cw[i:i + 1, :]
        qkv = y * jax.nn.sigmoid(y)
        q_all = qkv[:, 0:width]
        k_all = qkv[:, width:2 * width]
        v_all = qkv[:, 2 * width:3 * width]
        q_all = q_all * lax.rsqrt(_group_sumsq(q_all, gmat) + L2_EPS) * (HEAD_DIM ** -0.5)
        k_all = k_all * lax.rsqrt(_group_sumsq(k_all, gmat) + L2_EPS)
        ba = ba_ref[b]
        beta_all = jax.nn.sigmoid(ba[:, 0:LANES])
        z = ba[:, LANES:2 * LANES] + dtb
        softplus = jnp.maximum(z, 0.0) + jnp.log1p(jnp.exp(-jnp.abs(z)))
        g_all = neg_a * softplus
        gc_all = _dot_exact_lhs(tril_chunks, g_all)
        gate_all = gate_ref[b]
        gate_all = gate_all * jax.nn.sigmoid(gate_all)
        beta_e = _dot_exact_rhs(beta_all, expand_d)
        gc_e = _dot_exact_rhs(gc_all, expand_d)
        g_t = _dot_exact_rhs(g_all, expand_t)
        egc_e = jnp.exp(gc_e)

        for cc in range(nc):
            rows = slice(cc * c, (cc + 1) * c)
            glast = gc_e[(cc + 1) * c - 1:(cc + 1) * c, :]
            kend_e = jnp.exp(glast - gc_e[rows])
            sdec_e = jnp.exp(glast)
            for gi in range(n_grp):
                sl = slice(gi * dw, (gi + 1) * dw)
                k = k_all[rows, sl]
                beta = beta_e[rows, sl]
                chains.append(dict(b=b, cc=cc, gi=gi, rows=rows, sl=sl, q=q_all[rows, sl], k=k, kb=k * beta,
                                   vb=v_all[rows, sl] * beta, egc=egc_e[rows, sl],
                                   g_t=g_t[rows, gi * tw:(gi + 1) * tw], kend=kend_e[:, sl], sdec=sdec_e[:, sl],
                                   gate=gate_all[rows, sl]))

    for ch in chains:
        dm = _dot_exact_lhs(tril_bf, jnp.where(strict, ch["g_t"], 0.0))
        ch["decay"] = jnp.where(causal, jnp.exp(dm), 0.0)
        ch["k_bd"] = _block_diag(ch["k"], bd_td)
    for ch in chains:
        ch["a"] = jnp.where(strict, _dot_nt(ch["kb"].astype(BF16), ch["k_bd"]) * ch["decay"], 0.0)
        ch["tinv"] = eye - jnp.where(row_t // 2 == col_t // 2, ch["a"], 0.0)
    size = 2
    while size < c:
        lower_left = ((row_t // (2 * size) == col_t // (2 * size))
                      & (row_t % (2 * size) >= size) & (col_t % (2 * size) < size))
        for ch in chains:
            a21 = _block_diag(jnp.where(lower_left, ch["a"], 0.0), bd_tt)
            ch["t2_a21"] = _dot(ch["tinv"].astype(BF16), a21)
        for ch in chains:
            ch["tinv"] = ch["tinv"] - _dot(ch["t2_a21"].astype(BF16), _block_diag(ch["tinv"], bd_tt))
        size *= 2
    for ch in chains:
        tinv_bf = ch["tinv"].astype(BF16)
        ch["u"] = _dot(tinv_bf, _block_diag(ch["vb"], bd_td))
        ch["w"] = _dot(tinv_bf, _block_diag(ch["kb"] * ch["egc"], bd_td))
        ch["qk"] = jnp.where(causal, _dot_nt(ch["q"].astype(BF16), ch["k_bd"]) * ch["decay"], 0.0)
    for cc in range(nc):
        now = [ch for ch in chains if ch["cc"] == cc]
        for ch in now:
            ch["s"] = s_s[ch["b"], ch["gi"]]
            lhs = jnp.concatenate([ch["w"], ch["q"] * ch["egc"]], axis=0).astype(BF16)
            ch["ws"] = _dot(lhs, ch["s"].astype(BF16))
        for ch in now:
            v_new = ch["u"] - ch["ws"][0:c]
            ch["o"] = ch["ws"][c:2 * c] + _dot(ch["qk"].astype(BF16), _block_diag(v_new, bd_td))
            k_end = ch["k"] * ch["kend"]
            cross = _dot_tn(k_end.astype(BF16), v_new.astype(BF16))
            s_s[ch["b"], ch["gi"]] = ch["s"] * ch["sdec"] + jnp.where(bd_dd, cross, 0.0)
    for ch in chains:
        o = ch["o"]
        ms = _group_sumsq(o, gmat_g) * (1.0 / d)
        on = o * lax.rsqrt(ms + RMS_EPS) * gain[:, ch["sl"]]
        o_ref[ch["b"], ch["rows"], ch["sl"]] = (on * ch["gate"]).astype(o_ref.dtype)

    tail = xp_s[:, 8 + tb - n_tail:8 + tb, :]
    xp_s[:, top:8, :] = tail

    @pl.when(t == n_t - 1)
    def _():
        cnew_ref[...] = tail
        for b in range(bb):
            for h in range(n_heads):
                o = (h % gh) * d
                snew_ref[b, h] = s_s[b, h // gh, o:o + d, o:o + d]


def _gdn(qkv, gate, ba, conv_buf, s0, conv_w, a_log, dt_bias, out_gain, layer, batch_block):
    b, l, w3 = qkv.shape
    width = w3 // 3
    n_heads = width // HEAD_DIM
    chunk = min(GDN_CHUNK, l)
    assert l % chunk == 0 and chunk % 8 == 0 and chunk & (chunk - 1) == 0 and b % batch_block == 0
    n_tail = conv_buf.shape[1]
    assert n_tail <= min(8, chunk)
    bb = batch_block
    gh = min(n_heads, max(1, MXU_DIM // chunk))
    assert n_heads % gh == 0
    nc = 2 if (l // chunk) % 2 == 0 else 1
    tb = nc * chunk
    blk = lambda i, t: (i, t, 0)
    fix3 = lambda i, t: (i, 0, 0)
    lay3 = lambda i, t: (layer, 0, 0)
    return pl.pallas_call(
        functools.partial(_gdn_kernel, n_heads, chunk, gh, nc),
        grid=(b // bb, l // tb),
        in_specs=[pl.BlockSpec((bb, tb, w3), blk),
                  pl.BlockSpec((bb, tb, width), blk),
                  pl.BlockSpec((bb, tb, ba.shape[2]), blk),
                  pl.BlockSpec((bb, n_tail, w3), fix3),
                  pl.BlockSpec((bb, n_heads, HEAD_DIM, HEAD_DIM), lambda i, t: (i, 0, 0, 0)),
                  pl.BlockSpec((None,) + conv_w.shape[1:], lay3),
                  pl.BlockSpec((None, 1, LANES), lay3),
                  pl.BlockSpec((None, 1, LANES), lay3),
                  pl.BlockSpec((None, 1, width), lay3)],
        out_specs=[pl.BlockSpec((bb, tb, width), blk),
                   pl.BlockSpec((bb, n_heads, HEAD_DIM, HEAD_DIM), lambda i, t: (i, 0, 0, 0)),
                   pl.BlockSpec((bb, n_tail, w3), fix3)],
        out_shape=[jax.ShapeDtypeStruct((b, l, width), BF16),
                   jax.ShapeDtypeStruct(s0.shape, F32),
                   jax.ShapeDtypeStruct(conv_buf.shape, F32)],
        scratch_shapes=[pltpu.VMEM((bb, 8 + tb, w3), F32),
                        pltpu.VMEM((bb, n_heads // gh, gh * HEAD_DIM, gh * HEAD_DIM), F32)],
        compiler_params=pltpu.CompilerParams(
            dimension_semantics=("parallel", "arbitrary"), vmem_limit_bytes=VMEM_LIMIT),
        name="gdn",
    )(qkv, gate, ba, conv_buf, s0, conv_w, a_log, dt_bias, out_gain)


def _rel_bucket(rel, n_buckets):
    n = jnp.maximum(rel, 0)
    max_exact = n_buckets // 2
    nf = jnp.maximum(n, 1).astype(F32)
    large = max_exact + (jnp.log(nf / max_exact) / math.log(MAX_DISTANCE / max_exact)
                         * (n_buckets - max_exact)).astype(jnp.int32)
    large = jnp.minimum(large, n_buckets - 1)
    return jnp.where(n < max_exact, n, large)


def _bias_tile(rel_bias, rel):
    n_buckets, n_heads = rel_bias.shape
    tab = rel_bias.astype(F32) - rel_bias[n_buckets - 1].astype(F32)[None, :]
    bucket = _rel_bucket(rel, n_buckets)[None]
    bias = jnp.zeros((n_heads,) + rel.shape, F32)
    for n in range(n_buckets):
        bias = jnp.where(bucket == n, tab[n].reshape((n_heads,) + (1,) * rel.ndim), bias)
    return jnp.where((rel >= 0)[None], bias * LOG2E, -jnp.inf)


def _lambda(lp_ref, li_ref):
    lp = lp_ref[...]
    e1 = jnp.exp(jnp.sum(lp[0:1, :] * lp[1:2, :], axis=-1, keepdims=True))
    e2 = jnp.exp(jnp.sum(lp[2:3, :] * lp[3:4, :], axis=-1, keepdims=True))
    return e1 - e2 + li_ref[:, 0:1]


def _attn_kernel(tk, sub, q_ref, k_ref, v_ref, bias_ref, lp_ref, li_ref, gain_ref, o_ref,
                 q_s, m_s, acc_s, vx_s, s0_s, s1_s):
    tq = q_ref.shape[0]
    dv = 2 * HEAD_DIM
    qi = pl.program_id(2)

    @pl.when(qi == 0)
    def _():
        vx_s[:, 0:dv] = v_ref[...]
        vx_s[:, dv:2 * dv] = jnp.ones((vx_s.shape[0], dv), BF16)

    q = q_ref[...]
    lane = lax.broadcasted_iota(jnp.int32, q.shape, 1)
    q_s[0:tq, :] = jnp.where(lane < HEAD_DIM, q, 0.0).astype(BF16)
    q_s[tq:2 * tq, :] = jnp.where(lane >= HEAD_DIM, q, 0.0).astype(BF16)
    m_s[...] = jnp.full_like(m_s, NEG_BIG)
    acc_s[...] = jnp.zeros_like(acc_s)

    def scores(buf, j):
        ks = pl.multiple_of(j * tk, tk)
        buf[...] = _dot_nt(q_s[...], k_ref[pl.ds(ks, tk), :])

    def update(buf, j):
        ks = pl.multiple_of(j * tk, tk)
        m_prev = m_s[...]
        m_new = jnp.maximum(m_prev, jnp.max(buf[...], axis=-1, keepdims=True))
        p = jnp.exp2(buf[...] - jnp.concatenate([m_new] * (tk // LANES), axis=1)).astype(BF16)
        alpha = jnp.exp2(m_prev - m_new)
        pv = _dot(p, vx_s[pl.ds(ks, tk), :])
        acc_s[...] = jnp.concatenate([alpha] * (2 * dv // LANES), axis=1) * acc_s[...] + pv
        m_s[...] = m_new

    def below_diagonal_bias(buf):
        for m in range(2):
            buf[m * tq:m * tq + sub, tk - sub:tk] += bias_ref[1]

    def diagonal_bias(buf):
        for m in range(2):
            for r in range(tq // sub):
                rows = slice(m * tq + r * sub, m * tq + (r + 1) * sub)
                if r >= 1:
                    buf[rows, (r - 1) * sub:r * sub] += bias_ref[1]
                buf[rows, r * sub:(r + 1) * sub] += bias_ref[0]
                if (r + 1) * sub < tk:
                    buf[rows, (r + 1) * sub:tk] = jnp.full((sub, tk - (r + 1) * sub), -jnp.inf, F32)

    n_pairs = jnp.maximum(qi - 1, 0) // 2
    scores(s0_s, 0)

    def pair(i, carry):
        scores(s1_s, 2 * i + 1)
        update(s0_s, 2 * i)
        scores(s0_s, 2 * i + 2)
        update(s1_s, 2 * i + 1)
        return carry

    lax.fori_loop(0, n_pairs, pair, 0)

    @pl.when(qi == 0)
    def _():
        diagonal_bias(s0_s)
        update(s0_s, 0)

    @pl.when(qi % 2 == 1)
    def _():
        scores(s1_s, qi)
        below_diagonal_bias(s0_s)
        diagonal_bias(s1_s)
        update(s0_s, qi - 1)
        update(s1_s, qi)

    @pl.when((qi % 2 == 0) & (qi >= 2))
    def _():
        scores(s1_s, qi - 1)
        below_diagonal_bias(s1_s)
        update(s0_s, qi - 2)
        scores(s0_s, qi)
        diagonal_bias(s0_s)
        update(s1_s, qi - 1)
        update(s0_s, qi)

    lam = _lambda(lp_ref, li_ref)
    acc = acc_s[...]
    on = acc[:, 0:dv] * (1.0 / acc[:, dv:2 * dv])
    o = on[0:tq] - lam * on[tq:2 * tq]
    ms = jnp.mean(o * o, axis=-1, keepdims=True)
    o_ref[...] = (o * lax.rsqrt(ms + RMS_EPS) * gain_ref[...] * (1.0 - li_ref[:, 0:1])).astype(o_ref.dtype)


def _attn_tile(l):
    return 512 if l % 512 == 0 else l


def _attn(q, k_bf, v_bf, bias, lam_params, lam_init, out_gain, layer):
    b, l, width = q.shape
    dv = 2 * HEAD_DIM
    assert dv == LANES
    n_heads = width // dv
    tq = tk = _attn_tile(l)
    sub = bias.shape[-1]
    lay3 = lambda bi, h, qi: (layer, 0, 0)
    return pl.pallas_call(
        functools.partial(_attn_kernel, tk, sub),
        grid=(b, n_heads, l // tq),
        in_specs=[pl.BlockSpec((None, tq, dv), lambda bi, h, qi: (bi, qi, h)),
                  pl.BlockSpec((None, l, dv), lambda bi, h, qi: (bi, 0, h)),
                  pl.BlockSpec((None, l, dv), lambda bi, h, qi: (bi, 0, h)),
                  pl.BlockSpec((None, 2, sub, sub), lambda bi, h, qi: (h, 0, 0, 0)),
                  pl.BlockSpec((None,) + lam_params.shape[1:], lay3),
                  pl.BlockSpec((None, 1, LANES), lay3),
                  pl.BlockSpec((None, 1, dv), lay3)],
        out_specs=pl.BlockSpec((None, tq, dv), lambda bi, h, qi: (bi, qi, h)),
        out_shape=jax.ShapeDtypeStruct((b, l, width), BF16),
        scratch_shapes=[pltpu.VMEM((2 * tq, dv), BF16),
                        pltpu.VMEM((2 * tq, LANES), F32),
                        pltpu.VMEM((2 * tq, 2 * dv), F32),
                        pltpu.VMEM((l, 2 * dv), BF16),
                        pltpu.VMEM((2 * tq, tk), F32),
                        pltpu.VMEM((2 * tq, tk), F32)],
        compiler_params=pltpu.CompilerParams(
            dimension_semantics=("parallel", "parallel", "arbitrary"), vmem_limit_bytes=VMEM_LIMIT),
        name="attn",
    )(q, k_bf, v_bf, bias, lam_params, lam_init, out_gain)


def _decode_kernel(n_pp, n_heads, pt_ref, q_ref, kn_ref, vn_ref, bias_ref, lp_ref, li_ref, gain_ref, *rest):
    k_refs = rest[:n_pp]
    v_refs = rest[n_pp:2 * n_pp]
    o_ref, q_s, m_s, l_s, acc_s, kpad_s, vpad_s = rest[2 * n_pp:]
    t_new = q_ref.shape[0]
    page = k_refs[0].shape[0] // n_heads
    dv = 2 * HEAD_DIM
    rows = 2 * t_new
    j = pl.program_id(1)
    n_j = pl.num_programs(1)

    @pl.when(j == 0)
    def _():
        m_s[...] = jnp.full_like(m_s, NEG_BIG)
        l_s[...] = jnp.zeros_like(l_s)
        acc_s[...] = jnp.zeros_like(acc_s)
        kpad_s[...] = jnp.zeros_like(kpad_s)
        vpad_s[...] = jnp.zeros_like(vpad_s)
        lane = lax.broadcasted_iota(jnp.int32, (t_new, dv), 1)
        for h in range(n_heads):
            cs = slice(h * dv, (h + 1) * dv)
            qh = q_ref[:, cs]
            q_s[h, 0:t_new, :] = jnp.where(lane < HEAD_DIM, qh, 0.0)
            q_s[h, t_new:rows, :] = jnp.where(lane >= HEAD_DIM, qh, 0.0)
            kpad_s[h, 0:t_new, :] = kn_ref[pl.ds(h, t_new, stride=n_heads), :]
            vpad_s[h, 0:t_new, :] = vn_ref[pl.ds(h, t_new, stride=n_heads), :]

    def head_rows(ref, h):
        return ref[pl.ds(h, page, stride=n_heads), :]

    def step(k_get, v_get, bias_list):
        heads = range(n_heads)
        s_all, m_all, p_all = [], [], []
        for h in heads:
            qh = q_s[h]
            s_list = []
            for i, bb in enumerate(bias_list):
                s = _dot_nt(qh, k_get(i, h))
                s_list.append(s if bb is None else s + bb[h * rows:(h + 1) * rows, :])
            s_all.append(s_list)
        for h in heads:
            m_cur = s_all[h][0]
            for s in s_all[h][1:]:
                m_cur = jnp.maximum(m_cur, s)
            m_all.append(jnp.maximum(m_s[h], m_cur.max(axis=-1, keepdims=True)))
        for h in heads:
            m_keys = m_all[h] if page == LANES else m_all[h][:, 0:1]
            p_all.append([jnp.exp2(s - m_keys) for s in s_all[h]])
        for h in heads:
            alpha = jnp.exp2(m_s[h] - m_all[h])
            psum = p_all[h][0]
            pv = _dot(p_all[h][0], v_get(0, h))
            for i in range(1, len(p_all[h])):
                psum = psum + p_all[h][i]
                pv = pv + _dot(p_all[h][i], v_get(i, h))
            l_s[h] = alpha * l_s[h] + jnp.sum(psum, axis=-1, keepdims=True)
            acc_s[h] = alpha * acc_s[h] + pv
            m_s[h] = m_all[h]

    @pl.when(j < n_j - 1)
    def _():
        step(lambda i, h: head_rows(k_refs[i], h), lambda i, h: head_rows(v_refs[i], h), [None] * n_pp)

    @pl.when(j == n_j - 1)
    def _():
        k_get = lambda i, h: head_rows(k_refs[i], h) if i < n_pp else kpad_s[h]
        v_get = lambda i, h: head_rows(v_refs[i], h) if i < n_pp else vpad_s[h]
        step(k_get, v_get, [None] * (n_pp - 1) + [bias_ref[0], bias_ref[1]])
        lam = _lambda(lp_ref, li_ref)
        scale = 1.0 - li_ref[:, 0:1]
        for h in range(n_heads):
            on = acc_s[h] / l_s[h]
            o = on[0:t_new] - lam * on[t_new:rows]
            ms = jnp.mean(o * o, axis=-1, keepdims=True)
            o_ref[:, h * dv:(h + 1) * dv] = (o * lax.rsqrt(ms + RMS_EPS) * gain_ref[...] * scale).astype(o_ref.dtype)


def _decode(q, k_new, v_new, cache_k, cache_v, page_table, bias, lam_params, lam_init, out_gain, layer,
            pages_per_step):
    b, t_new, width = q.shape
    dv = 2 * HEAD_DIM
    n_heads = width // dv
    page = cache_k.shape[2] // n_heads
    n_pages = page_table.shape[1]
    n_pp = pages_per_step
    assert n_pages % n_pp == 0 and t_new <= page
    rows = 2 * t_new
    seq = lambda bi, j, pt: (bi, 0, 0)
    lay3 = lambda bi, j, pt: (layer, 0, 0)

    def page_spec(i):
        return pl.BlockSpec((None, None, page * n_heads, dv),
                            lambda bi, j, pt: (layer, pt[bi, j * n_pp + i], 0, 0))

    grid_spec = pltpu.PrefetchScalarGridSpec(
        num_scalar_prefetch=1,
        grid=(b, n_pages // n_pp),
        in_specs=[pl.BlockSpec((None, t_new, width), seq),
                  pl.BlockSpec((None, t_new * n_heads, dv), lambda bi, j, pt: (layer, bi, 0)),
                  pl.BlockSpec((None, t_new * n_heads, dv), lambda bi, j, pt: (layer, bi, 0)),
                  pl.BlockSpec(bias.shape, lambda bi, j, pt: (0, 0, 0)),
                  pl.BlockSpec((None,) + lam_params.shape[1:], lay3),
                  pl.BlockSpec((None, 1, LANES), lay3),
                  pl.BlockSpec((None, 1, dv), lay3)]
                 + [page_spec(i) for i in range(n_pp)] * 2,
        out_specs=pl.BlockSpec((None, t_new, width), seq),
        scratch_shapes=[pltpu.VMEM((n_heads, rows, dv), F32),
                        pltpu.VMEM((n_heads, rows, LANES), F32),
                        pltpu.VMEM((n_heads, rows, LANES), F32),
                        pltpu.VMEM((n_heads, rows, dv), F32),
                        pltpu.VMEM((n_heads, page, dv), F32),
                        pltpu.VMEM((n_heads, page, dv), F32)])
    return pl.pallas_call(
        functools.partial(_decode_kernel, n_pp, n_heads),
        grid_spec=grid_spec,
        out_shape=jax.ShapeDtypeStruct((b, t_new, width), BF16),
        compiler_params=pltpu.CompilerParams(
            dimension_semantics=("parallel", "arbitrary"), vmem_limit_bytes=VMEM_LIMIT),
        name="decode",
    )(page_table, q, k_new, v_new, bias, lam_params, lam_init, out_gain,
      *([cache_k] * n_pp), *([cache_v] * n_pp))


def _pad_lanes(a, width=LANES):
    return jnp.pad(a, [(0, 0)] * (a.ndim - 1) + [(0, width - a.shape[-1])])


def kernel(x_prompt, x_sample, cache_k, cache_v, state_gdn, state_conv, page_table, ffn1_norm, ffn1_w_gate_up, ffn1_w_down, mix_norm, w_in, conv_w, gdn_a_log, gdn_dt_bias, gdn_out_norm, diff_q_norm, diff_k_norm, diff_lambda, diff_out_norm, rel_bias, w_out, ffn2_norm, ffn2_w_gate_up, ffn2_w_down):
    depth, d_model, _ = w_in.shape
    n_gdn = gdn_a_log.shape[1]
    n_diff = rel_bias.shape[1]
    gdn_w = n_gdn * HEAD_DIM
    diff_w = n_diff * 2 * HEAD_DIM
    conv_dim = conv_w.shape[2]
    assert conv_dim == 3 * gdn_w and w_out.shape[1] == gdn_w + diff_w
    page = cache_k.shape[2]
    assert page >= MAX_DISTANCE
    past_len = page_table.shape[1] * page

    o = 0
    cols = {}
    for name, wd in (("qkv", conv_dim), ("gate", gdn_w), ("b", n_gdn), ("a", n_gdn),
                     ("dq", diff_w), ("dk", diff_w), ("dv", diff_w)):
        cols[name] = w_in[:, :, o:o + wd]
        o += wd
    w_in_r = jnp.concatenate([cols["qkv"], cols["gate"], cols["dq"], cols["dk"], cols["dv"],
                              _pad_lanes(cols["b"]), _pad_lanes(cols["a"])], axis=-1).astype(BF16)
    widths = (conv_dim, gdn_w, diff_w, 2 * LANES)
    bf = lambda a: a.astype(BF16)
    w1_gu, w1_d, w2_gu, w2_d, w_out_bf = bf(ffn1_w_gate_up), bf(ffn1_w_down), bf(ffn2_w_gate_up), bf(ffn2_w_down), bf(w_out)
    row3 = lambda a: a.reshape(depth, 1, -1)
    n1, n2, nm = row3(ffn1_norm), row3(ffn2_norm), row3(mix_norm)
    q_gain = row3(jnp.tile(diff_q_norm, (1, diff_w // HEAD_DIM)))
    k_gain = row3(jnp.tile(diff_k_norm, (1, diff_w // HEAD_DIM)))
    a_log = row3(_pad_lanes(gdn_a_log))
    dt_bias = row3(_pad_lanes(gdn_dt_bias))
    gdn_gain = row3(jnp.tile(gdn_out_norm, (1, n_gdn)))
    diff_gain = row3(diff_out_norm)
    lam_init = jnp.asarray([0.8 - 0.6 * math.exp(-0.3 * l) for l in range(depth)], F32)
    lam_init = jnp.broadcast_to(lam_init[:, None, None], (depth, 1, LANES))
    cache_k2 = cache_k.reshape(cache_k.shape[:2] + (page * n_diff, 2 * HEAD_DIM))
    cache_v2 = cache_v.reshape(cache_v.shape[:2] + (page * n_diff, 2 * HEAD_DIM))

    def bias_prompt(l):
        t = _attn_tile(l)
        sub = min(t, MAX_DISTANCE)
        assert t % sub == 0 and (l == t or sub == MAX_DISTANCE)
        i = jnp.arange(sub, dtype=jnp.int32)
        rel = i[:, None] - i[None, :]
        return jnp.stack([_bias_tile(rel_bias, rel), _bias_tile(rel_bias, rel + sub)], axis=1)

    def bias_sample(t_new):
        tok = jnp.arange(t_new, dtype=jnp.int32)
        lane = jnp.arange(page, dtype=jnp.int32)
        rel_last = (page + tok)[:, None] - lane[None, :]
        rel_new = jnp.where(lane[None, :] < t_new, tok[:, None] - lane[None, :], -1)
        tiles = jnp.stack([_bias_tile(rel_bias, rel_last), _bias_tile(rel_bias, rel_new)], axis=0)
        tiles = jnp.broadcast_to(tiles[:, :, None], (2, n_diff, 2, t_new, page))
        return tiles.reshape(2, n_diff * 2 * t_new, page)

    def run(x, paged):
        b, l, _ = x.shape
        m = b * l
        xf = x.reshape(m, d_model)
        ss, cs = [], []
        k_all = jnp.zeros((depth, m * n_diff, 2 * HEAD_DIM), F32)
        v_all = jnp.zeros((depth, m * n_diff, 2 * HEAD_DIM), F32)
        if paged:
            bias = bias_sample(l)
        else:
            bias = bias_prompt(l)
        for layer in range(depth):
            xf = _ffn(xf, n1, w1_gu, w1_d, layer)
            qkv, gate, ba, q, k_all, v_all, k_bf, v_bf = _inproj(xf, nm, w_in_r, q_gain, k_gain, widths, layer,
                                                                 k_all, v_all)
            r3 = lambda a: a.reshape(b, l, a.shape[-1])
            if paged:
                conv_buf, s0, bb = state_conv[layer], state_gdn[layer], 4 if b % 4 == 0 else 1
            else:
                conv_buf = jnp.zeros((b, conv_w.shape[1] - 1, conv_dim), F32)
                s0 = jnp.zeros((b, n_gdn, HEAD_DIM, HEAD_DIM), F32)
                bb = b
            o_gdn, s_new, conv_new = _gdn(r3(qkv), r3(gate), r3(ba), conv_buf, s0, conv_w, a_log, dt_bias,
                                          gdn_gain, layer, bb)
            if paged:
                n_pages = page_table.shape[1]
                o_diff = _decode(r3(q), k_all, v_all, cache_k2, cache_v2, page_table, bias, diff_lambda, lam_init,
                                 diff_gain, layer, next(n for n in (16, 8, 4, 2, 1) if n_pages % n == 0))
            else:
                o_diff = _attn(r3(q), r3(k_bf), r3(v_bf), bias, diff_lambda, lam_init, diff_gain, layer)
            xf = _ffn(xf, n2, w2_gu, w2_d, layer,
                      mix=(o_gdn.reshape(m, gdn_w), o_diff.reshape(m, diff_w), w_out_bf))
            ss.append(s_new)
            cs.append(conv_new)
        kv_shape = (depth, b, l, n_diff, 2 * HEAD_DIM)
        return (xf.reshape(b, l, d_model), k_all.reshape(kv_shape), v_all.reshape(kv_shape),
                jnp.stack(ss), jnp.stack(cs))

    y_p, k_p, v_p, s_p, c_p = run(x_prompt, False)
    y_s, k_s, v_s, s_s, c_s = run(x_sample, True)
    return (y_p, y_s, k_p, v_p, s_p, c_p, k_s, v_s, s_s, c_s)
```

```python
import functools
import math

import jax
import jax.numpy as jnp
from jax import lax
from jax.experimental import pallas as pl
from jax.experimental.pallas import tpu as pltpu

F32 = jnp.float32
BF16 = jnp.bfloat16

HEAD_DIM = 64
GDN_CHUNK = 64
MAX_DISTANCE = 128
RMS_EPS = 1e-6
L2_EPS = 1e-6
LANES = 128
MXU_DIM = 256
VMEM_LIMIT = 52 * 1024 * 1024
NEG_BIG = -1e30
LOG2E = 1.4426950408889634


def _dot(a, b):
    return jnp.dot(a, b, preferred_element_type=F32)


def _dot_nt(a, b):
    return lax.dot_general(a, b, (((1,), (1,)), ((), ())), preferred_element_type=F32)


def _dot_tn(a, b):
    return lax.dot_general(a, b, (((0,), (0,)), ((), ())), preferred_element_type=F32)


def _split3(x):
    hi = x.astype(BF16)
    r = x - hi.astype(F32)
    mid = r.astype(BF16)
    lo = (r - mid.astype(F32)).astype(BF16)
    return hi, mid, lo


def _dot_exact_lhs(a_bf, b):
    b0, b1, b2 = _split3(b)
    return _dot(a_bf, b0) + _dot(a_bf, b1) + _dot(a_bf, b2)


def _group_sumsq(x, gmat):
    sq = x * x
    hi = sq.astype(BF16)
    lo = (sq - hi.astype(F32)).astype(BF16)
    return _dot(hi, gmat) + _dot(lo, gmat)


def _group_matrix(width):
    r = lax.broadcasted_iota(jnp.int32, (width, width), 0) // HEAD_DIM
    c = lax.broadcasted_iota(jnp.int32, (width, width), 1) // HEAD_DIM
    return jnp.where(r == c, 1.0, 0.0).astype(BF16)


def _ffn_kernel(has_mix, cf, *refs):
    if has_mix:
        x_ref, og_ref, od_ref, wo_ref, nw_ref, wgu_ref, wd_ref, o_ref, h_s, a_s = refs
    else:
        x_ref, nw_ref, wgu_ref, wd_ref, o_ref, h_s, a_s = refs
    d_ff = wd_ref.shape[0]
    x = x_ref[...]
    if has_mix:
        half = og_ref.shape[-1]
        x = x + _dot(og_ref[...], wo_ref[0:half, :]) + _dot(od_ref[...], wo_ref[half:, :])
    ms = jnp.mean(x * x, axis=-1, keepdims=True)
    h_s[...] = (x * lax.rsqrt(ms + RMS_EPS) * nw_ref[...]).astype(BF16)
    for c0 in range(0, d_ff, cf):
        h = h_s[...]
        g = _dot(h, wgu_ref[:, c0:c0 + cf])
        u = _dot(h, wgu_ref[:, d_ff + c0:d_ff + c0 + cf])
        a_s[:, c0:c0 + cf] = ((g * jax.nn.sigmoid(g)) * u).astype(BF16)
    o_ref[...] = x + 0.5 * _dot(a_s[...], wd_ref[...])


def _ffn_tiles(m, d_ff):
    tm = 512 if m % 512 == 0 else m
    cf = 2 * LANES if d_ff % (2 * LANES) == 0 else d_ff
    return tm, cf


def _ffn(x, norm_w, w_gu, w_down, layer, mix=None):
    m, d = x.shape
    d_ff = w_down.shape[1]
    tm, cf = _ffn_tiles(m, d_ff)
    row = lambda i: (i, 0)
    lay3 = lambda i: (layer, 0, 0)
    once = pl.Buffered(1)
    in_specs = [pl.BlockSpec((tm, d), row)]
    args = [x]
    if mix is not None:
        og, od, w_out = mix
        in_specs += [pl.BlockSpec((tm, og.shape[1]), row), pl.BlockSpec((tm, od.shape[1]), row),
                     pl.BlockSpec((None,) + w_out.shape[1:], lay3, pipeline_mode=once)]
        args += [og, od, w_out]
    in_specs += [
        pl.BlockSpec((None, 1, d), lay3),
        pl.BlockSpec((None, d, 2 * d_ff), lay3, pipeline_mode=once),
        pl.BlockSpec((None, d_ff, d), lay3, pipeline_mode=once),
    ]
    args += [norm_w, w_gu, w_down]
    return pl.pallas_call(
        functools.partial(_ffn_kernel, mix is not None, cf),
        grid=(m // tm,),
        in_specs=in_specs,
        out_specs=pl.BlockSpec((tm, d), row),
        out_shape=jax.ShapeDtypeStruct((m, d), F32),
        scratch_shapes=[pltpu.VMEM((tm, d), BF16), pltpu.VMEM((tm, d_ff), BF16)],
        compiler_params=pltpu.CompilerParams(
            dimension_semantics=("parallel",), vmem_limit_bytes=VMEM_LIMIT),
        name="ffn_mix" if mix is not None else "ffn",
    )(*args)


def _inproj_kernel(widths, x_ref, nw_ref, w_ref, qg_ref, kg_ref, k_prev_ref, v_prev_ref,
                   qkv_ref, gate_ref, ba_ref, q_ref, k_ref, v_ref, kb_ref, vb_ref):
    del k_prev_ref, v_prev_ref
    c_qkv, c_gate, c_d, c_ba = widths
    x = x_ref[...]
    ms = jnp.mean(x * x, axis=-1, keepdims=True)
    h = (x * lax.rsqrt(ms + RMS_EPS) * nw_ref[...]).astype(BF16)
    o = 0
    qkv_ref[...] = _dot(h, w_ref[:, o:o + c_qkv]); o += c_qkv
    gate_ref[...] = _dot(h, w_ref[:, o:o + c_gate]); o += c_gate
    dq = _dot(h, w_ref[:, o:o + c_d]); o += c_d
    dk = _dot(h, w_ref[:, o:o + c_d]); o += c_d
    dv = _dot(h, w_ref[:, o:o + c_d]); o += c_d
    ba_ref[...] = _dot(h, w_ref[:, o:o + c_ba])
    gmat = _group_matrix(c_d)
    inv_hd = 1.0 / HEAD_DIM
    qn = dq * lax.rsqrt(_group_sumsq(dq, gmat) * inv_hd + RMS_EPS) * qg_ref[...]
    kn = dk * lax.rsqrt(_group_sumsq(dk, gmat) * inv_hd + RMS_EPS) * kg_ref[...]
    q_ref[...] = qn * (HEAD_DIM ** -0.5 * LOG2E)
    tm = kn.shape[0]
    dv_w = 2 * HEAD_DIM
    n_heads = c_d // dv_w
    for h in range(n_heads):
        k_ref[pl.ds(h, tm, stride=n_heads), :] = kn[:, h * dv_w:(h + 1) * dv_w]
        v_ref[pl.ds(h, tm, stride=n_heads), :] = dv[:, h * dv_w:(h + 1) * dv_w]
    kb_ref[...] = kn.astype(BF16)
    vb_ref[...] = dv.astype(BF16)


def _inproj(x, norm_w, w_in, q_gain, k_gain, widths, layer, k_all, v_all):
    m, d = x.shape
    c_qkv, c_gate, c_d, c_ba = widths
    dv_w = 2 * HEAD_DIM
    n_heads = c_d // dv_w
    tm = 512 if m % 512 == 0 else m
    row = lambda i: (i, 0)
    lay3 = lambda i: (layer, 0, 0)
    slab = pl.BlockSpec((None, tm * n_heads, dv_w), lambda i: (layer, i, 0))
    outs = [(c_qkv, F32), (c_gate, F32), (c_ba, F32), (c_d, F32), None, None, (c_d, BF16), (c_d, BF16)]
    return pl.pallas_call(
        functools.partial(_inproj_kernel, widths),
        grid=(m // tm,),
        in_specs=[pl.BlockSpec((tm, d), row),
                  pl.BlockSpec((None, 1, d), lay3),
                  pl.BlockSpec((None,) + w_in.shape[1:], lay3),
                  pl.BlockSpec((None, 1, c_d), lay3),
                  pl.BlockSpec((None, 1, c_d), lay3),
                  pl.BlockSpec(memory_space=pl.ANY),
                  pl.BlockSpec(memory_space=pl.ANY)],
        out_specs=[slab if o is None else pl.BlockSpec((tm, o[0]), row) for o in outs],
        out_shape=[jax.ShapeDtypeStruct(k_all.shape, F32) if o is None else jax.ShapeDtypeStruct((m, o[0]), o[1])
                   for o in outs],
        input_output_aliases={5: 4, 6: 5},
        compiler_params=pltpu.CompilerParams(
            dimension_semantics=("parallel",), vmem_limit_bytes=VMEM_LIMIT),
        name="inproj",
    )(x, norm_w, w_in, q_gain, k_gain, k_all, v_all)


def _block_mask(rows, cols, row_group, col_group):
    r = lax.broadcasted_iota(jnp.int32, (rows, cols), 0) // row_group
    c = lax.broadcasted_iota(jnp.int32, (rows, cols), 1) // col_group
    return r == c


def _block_diag(x, mask):
    reps = mask.shape[0] // x.shape[0]
    return jnp.where(mask, jnp.concatenate([x] * reps, axis=0), 0.0).astype(BF16)


def _dot_exact_rhs(a, b_bf):
    a0, a1, a2 = _split3(a)
    return _dot(a0, b_bf) + _dot(a1, b_bf) + _dot(a2, b_bf)


def _gdn_kernel(n_heads, chunk, gh, nc, qkv_ref, gate_ref, ba_ref, cbuf_ref, s0_ref, cw_ref, alog_ref, dtb_ref,
                gain_ref, o_ref, snew_ref, cnew_ref, xp_s, s_s):
    bb = qkv_ref.shape[0]
    c = chunk
    tb = nc * c
    d = HEAD_DIM
    width = n_heads * d
    n_grp = n_heads // gh
    tw = gh * c
    dw = gh * d
    n_tail = cbuf_ref.shape[1]
    n_taps = n_tail + 1
    top = 8 - n_tail
    t = pl.program_id(1)
    n_t = pl.num_programs(1)

    @pl.when(t == 0)
    def _():
        xp_s[:, top:8, :] = cbuf_ref[...]
        s_s[...] = jnp.zeros_like(s_s)
        for b in range(bb):
            for h in range(n_heads):
                o = (h % gh) * d
                s_s[b, h // gh, o:o + d, o:o + d] = s0_ref[b, h]

    xp_s[:, 8:8 + tb, :] = qkv_ref[...]

    cw = cw_ref[...]
    gmat = _group_matrix(width)
    ri = lax.broadcasted_iota(jnp.int32, (c, c), 0)
    ci = lax.broadcasted_iota(jnp.int32, (c, c), 1)
    tril_bf = jnp.where(ri >= ci, 1.0, 0.0).astype(BF16)
    rb = lax.broadcasted_iota(jnp.int32, (tb, tb), 0)
    cb = lax.broadcasted_iota(jnp.int32, (tb, tb), 1)
    tril_chunks = jnp.where((rb >= cb) & (rb // c == cb // c), 1.0, 0.0).astype(BF16)
    row_t = lax.broadcasted_iota(jnp.int32, (c, tw), 0)
    col_t = lax.broadcasted_iota(jnp.int32, (c, tw), 1) % c
    causal = row_t >= col_t
    strict = row_t > col_t
    eye = jnp.where(row_t == col_t, 1.0, 0.0).astype(F32)
    bd_tt = _block_mask(tw, tw, c, c)
    bd_td = _block_mask(tw, dw, c, d)
    bd_dd = _block_mask(dw, dw, d, d)
    gmat_g = jnp.where(bd_dd, 1.0, 0.0).astype(BF16)
    expand_d = jnp.where(_block_mask(LANES, width, 1, d), 1.0, 0.0).astype(BF16)
    expand_t = jnp.where(_block_mask(LANES, n_heads * c, 1, c), 1.0, 0.0).astype(BF16)
    neg_a = -jnp.exp(alog_ref[...])
    dtb = dtb_ref[...]
    gain = gain_ref[...]

    chains = []
    for b in range(bb):
        y = xp_s[b, top:top + tb, :] * cw[0:1, :]
        for i in range(1, n_taps):
            y = y + xp_s[b, top + i:top + i + tb, :] * cw[i:i + 1, :]
        qkv = y * jax.nn.sigmoid(y)
        q_all = qkv[:, 0:width]
        k_all = qkv[:, width:2 * width]
        v_all = qkv[:, 2 * width:3 * width]
        q_all = q_all * lax.rsqrt(_group_sumsq(q_all, gmat) + L2_EPS) * (HEAD_DIM ** -0.5)
        k_all = k_all * lax.rsqrt(_group_sumsq(k_all, gmat) + L2_EPS)
        ba = ba_ref[b]
        beta_all = jax.nn.sigmoid(ba[:, 0:LANES])
        z = ba[:, LANES:2 * LANES] + dtb
        softplus = jnp.maximum(z, 0.0) + jnp.log1p(jnp.exp(-jnp.abs(z)))
        g_all = neg_a * softplus
        gc_all = _dot_exact_lhs(tril_chunks, g_all)
        gate_all = gate_ref[b]
        gate_all = gate_all * jax.nn.sigmoid(gate_all)
        beta_e = _dot_exact_rhs(beta_all, expand_d)
        gc_e = _dot_exact_rhs(gc_all, expand_d)
        g_t = _dot_exact_rhs(g_all, expand_t)
        egc_e = jnp.exp(gc_e)

        for cc in range(nc):
            rows = slice(cc * c, (cc + 1) * c)
            glast = gc_e[(cc + 1) * c - 1:(cc + 1) * c, :]
            kend_e = jnp.exp(glast - gc_e[rows])
            sdec_e = jnp.exp(glast)
            for gi in range(n_grp):
                sl = slice(gi * dw, (gi + 1) * dw)
                k = k_all[rows, sl]
                beta = beta_e[rows, sl]
                chains.append(dict(b=b, cc=cc, gi=gi, rows=rows, sl=sl, q=q_all[rows, sl], k=k, kb=k * beta,
                                   vb=v_all[rows, sl] * beta, egc=egc_e[rows, sl],
                                   g_t=g_t[rows, gi * tw:(gi + 1) * tw], kend=kend_e[:, sl], sdec=sdec_e[:, sl],
                                   gate=gate_all[rows, sl]))

    for ch in chains:
        dm = _dot_exact_lhs(tril_bf, jnp.where(strict, ch["g_t"], 0.0))
        ch["decay"] = jnp.where(causal, jnp.exp(dm), 0.0)
        ch["k_bd"] = _block_diag(ch["k"], bd_td)
    for ch in chains:
        ch["a"] = jnp.where(strict, _dot_nt(ch["kb"].astype(BF16), ch["k_bd"]) * ch["decay"], 0.0)
        ch["tinv"] = eye - jnp.where(row_t // 2 == col_t // 2, ch["a"], 0.0)
    size = 2
    while size < c:
        lower_left = ((row_t // (2 * size) == col_t // (2 * size))
                      & (row_t % (2 * size) >= size) & (col_t % (2 * size) < size))
        for ch in chains:
            a21 = _block_diag(jnp.where(lower_left, ch["a"], 0.0), bd_tt)
            ch["t2_a21"] = _dot(ch["tinv"].astype(BF16), a21)
        for ch in chains:
            ch["tinv"] = ch["tinv"] - _dot(ch["t2_a21"].astype(BF16), _block_diag(ch["tinv"], bd_tt))
        size *= 2
    for ch in chains:
        tinv_bf = ch["tinv"].astype(BF16)
        ch["u"] = _dot(tinv_bf, _block_diag(ch["vb"], bd_td))
        ch["w"] = _dot(tinv_bf, _block_diag(ch["kb"] * ch["egc"], bd_td))
        ch["qk"] = jnp.where(causal, _dot_nt(ch["q"].astype(BF16), ch["k_bd"]) * ch["decay"], 0.0)
    for cc in range(nc):
        now = [ch for ch in chains if ch["cc"] == cc]
        for ch in now:
            ch["s"] = s_s[ch["b"], ch["gi"]]
            lhs = jnp.concatenate([ch["w"], ch["q"] * ch["egc"]], axis=0).astype(BF16)
            ch["ws"] = _dot(lhs, ch["s"].astype(BF16))
        for ch in now:
            v_new = ch["u"] - ch["ws"][0:c]
            ch["o"] = ch["ws"][c:2 * c] + _dot(ch["qk"].astype(BF16), _block_diag(v_new, bd_td))
            k_end = ch["k"] * ch["kend"]
            cross = _dot_tn(k_end.astype(BF16), v_new.astype(BF16))
            s_s[ch["b"], ch["gi"]] = ch["s"] * ch["sdec"] + jnp.where(bd_dd, cross, 0.0)
    for ch in chains:
        o = ch["o"]
        ms = _group_sumsq(o, gmat_g) * (1.0 / d)
        on = o * lax.rsqrt(ms + RMS_EPS) * gain[:, ch["sl"]]
        o_ref[ch["b"], ch["rows"], ch["sl"]] = (on * ch["gate"]).astype(o_ref.dtype)

    tail = xp_s[:, 8 + tb - n_tail:8 + tb, :]
    xp_s[:, top:8, :] = tail

    @pl.when(t == n_t - 1)
    def _():
        cnew_ref[...] = tail
        for b in range(bb):
            for h in range(n_heads):
                o = (h % gh) * d
                snew_ref[b, h] = s_s[b, h // gh, o:o + d, o:o + d]


def _gdn(qkv, gate, ba, conv_buf, s0, conv_w, a_log, dt_bias, out_gain, layer, batch_block):
    b, l, w3 = qkv.shape
    width = w3 // 3
    n_heads = width // HEAD_DIM
    chunk = min(GDN_CHUNK, l)
    assert l % chunk == 0 and chunk % 8 == 0 and chunk & (chunk - 1) == 0 and b % batch_block == 0
    n_tail = conv_buf.shape[1]
    assert n_tail <= min(8, chunk)
    bb = batch_block
    gh = min(n_heads, max(1, MXU_DIM // chunk))
    assert n_heads % gh == 0
    nc = next(n for n in (4, 2, 1) if (l // chunk) % n == 0)
    tb = nc * chunk
    blk = lambda i, t: (i, t, 0)
    fix3 = lambda i, t: (i, 0, 0)
    lay3 = lambda i, t: (layer, 0, 0)
    return pl.pallas_call(
        functools.partial(_gdn_kernel, n_heads, chunk, gh, nc),
        grid=(b // bb, l // tb),
        in_specs=[pl.BlockSpec((bb, tb, w3), blk),
                  pl.BlockSpec((bb, tb, width), blk),
                  pl.BlockSpec((bb, tb, ba.shape[2]), blk),
                  pl.BlockSpec((bb, n_tail, w3), fix3),
                  pl.BlockSpec((bb, n_heads, HEAD_DIM, HEAD_DIM), lambda i, t: (i, 0, 0, 0)),
                  pl.BlockSpec((None,) + conv_w.shape[1:], lay3),
                  pl.BlockSpec((None, 1, LANES), lay3),
                  pl.BlockSpec((None, 1, LANES), lay3),
                  pl.BlockSpec((None, 1, width), lay3)],
        out_specs=[pl.BlockSpec((bb, tb, width), blk),
                   pl.BlockSpec((bb, n_heads, HEAD_DIM, HEAD_DIM), lambda i, t: (i, 0, 0, 0)),
                   pl.BlockSpec((bb, n_tail, w3), fix3)],
        out_shape=[jax.ShapeDtypeStruct((b, l, width), BF16),
                   jax.ShapeDtypeStruct(s0.shape, F32),
                   jax.ShapeDtypeStruct(conv_buf.shape, F32)],
        scratch_shapes=[pltpu.VMEM((bb, 8 + tb, w3), F32),
                        pltpu.VMEM((bb, n_heads // gh, gh * HEAD_DIM, gh * HEAD_DIM), F32)],
        compiler_params=pltpu.CompilerParams(
            dimension_semantics=("parallel", "arbitrary"), vmem_limit_bytes=VMEM_LIMIT),
        name="gdn",
    )(qkv, gate, ba, conv_buf, s0, conv_w, a_log, dt_bias, out_gain)


def _rel_bucket(rel, n_buckets):
    n = jnp.maximum(rel, 0)
    max_exact = n_buckets // 2
    nf = jnp.maximum(n, 1).astype(F32)
    large = max_exact + (jnp.log(nf / max_exact) / math.log(MAX_DISTANCE / max_exact)
                         * (n_buckets - max_exact)).astype(jnp.int32)
    large = jnp.minimum(large, n_buckets - 1)
    return jnp.where(n < max_exact, n, large)


def _bias_tile(rel_bias, rel):
    n_buckets, n_heads = rel_bias.shape
    tab = rel_bias.astype(F32) - rel_bias[n_buckets - 1].astype(F32)[None, :]
    bucket = _rel_bucket(rel, n_buckets)[None]
    bias = jnp.zeros((n_heads,) + rel.shape, F32)
    for n in range(n_buckets):
        bias = jnp.where(bucket == n, tab[n].reshape((n_heads,) + (1,) * rel.ndim), bias)
    return jnp.where((rel >= 0)[None], bias * LOG2E, -jnp.inf)


def _lambda(lp_ref, li_ref):
    lp = lp_ref[...]
    e1 = jnp.exp(jnp.sum(lp[0:1, :] * lp[1:2, :], axis=-1, keepdims=True))
    e2 = jnp.exp(jnp.sum(lp[2:3, :] * lp[3:4, :], axis=-1, keepdims=True))
    return e1 - e2 + li_ref[:, 0:1]


def _attn_kernel(tk, sub, q_ref, k_ref, v_ref, bias_ref, lp_ref, li_ref, gain_ref, o_ref,
                 q_s, m_s, acc_s, vx_s, s0_s, s1_s):
    tq = q_ref.shape[0]
    dv = 2 * HEAD_DIM
    qi = pl.program_id(2)

    @pl.when(qi == 0)
    def _():
        vx_s[:, 0:dv] = v_ref[...]
        vx_s[:, dv:2 * dv] = jnp.ones((vx_s.shape[0], dv), BF16)

    q = q_ref[...]
    lane = lax.broadcasted_iota(jnp.int32, q.shape, 1)
    q_s[0:tq, :] = jnp.where(lane < HEAD_DIM, q, 0.0).astype(BF16)
    q_s[tq:2 * tq, :] = jnp.where(lane >= HEAD_DIM, q, 0.0).astype(BF16)
    m_s[...] = jnp.full_like(m_s, NEG_BIG)
    acc_s[...] = jnp.zeros_like(acc_s)

    def scores(buf, j):
        ks = pl.multiple_of(j * tk, tk)
        buf[...] = _dot_nt(q_s[...], k_ref[pl.ds(ks, tk), :])

    def update(buf, j):
        ks = pl.multiple_of(j * tk, tk)
        m_prev = m_s[...]
        m_new = jnp.maximum(m_prev, jnp.max(buf[...], axis=-1, keepdims=True))
        p = jnp.exp2(buf[...] - jnp.concatenate([m_new] * (tk // LANES), axis=1)).astype(BF16)
        alpha = jnp.exp2(m_prev - m_new)
        pv = _dot(p, vx_s[pl.ds(ks, tk), :])
        acc_s[...] = jnp.concatenate([alpha] * (2 * dv // LANES), axis=1) * acc_s[...] + pv
        m_s[...] = m_new

    def below_diagonal_bias(buf):
        for m in range(2):
            buf[m * tq:m * tq + sub, tk - sub:tk] += bias_ref[1]

    def diagonal_bias(buf):
        for m in range(2):
            for r in range(tq // sub):
                rows = slice(m * tq + r * sub, m * tq + (r + 1) * sub)
                if r >= 1:
                    buf[rows, (r - 1) * sub:r * sub] += bias_ref[1]
                buf[rows, r * sub:(r + 1) * sub] += bias_ref[0]
                if (r + 1) * sub < tk:
                    buf[rows, (r + 1) * sub:tk] = jnp.full((sub, tk - (r + 1) * sub), -jnp.inf, F32)

    n_pairs = jnp.maximum(qi - 1, 0) // 2
    scores(s0_s, 0)

    def pair(i, carry):
        scores(s1_s, 2 * i + 1)
        update(s0_s, 2 * i)
        scores(s0_s, 2 * i + 2)
        update(s1_s, 2 * i + 1)
        return carry

    lax.fori_loop(0, n_pairs, pair, 0)

    @pl.when(qi == 0)
    def _():
        diagonal_bias(s0_s)
        update(s0_s, 0)

    @pl.when(qi % 2 == 1)
    def _():
        scores(s1_s, qi)
        below_diagonal_bias(s0_s)
        diagonal_bias(s1_s)
        update(s0_s, qi - 1)
        update(s1_s, qi)

    @pl.when((qi % 2 == 0) & (qi >= 2))
    def _():
        scores(s1_s, qi - 1)
        below_diagonal_bias(s1_s)
        update(s0_s, qi - 2)
        scores(s0_s, qi)
        diagonal_bias(s0_s)
        update(s1_s, qi - 1)
        update(s0_s, qi)

    lam = _lambda(lp_ref, li_ref)
    acc = acc_s[...]
    on = acc[:, 0:dv] * (1.0 / acc[:, dv:2 * dv])
    o = on[0:tq] - lam * on[tq:2 * tq]
    ms = jnp.mean(o * o, axis=-1, keepdims=True)
    o_ref[...] = (o * lax.rsqrt(ms + RMS_EPS) * gain_ref[...] * (1.0 - li_ref[:, 0:1])).astype(o_ref.dtype)


def _attn_tile(l):
    return 512 if l % 512 == 0 else l


def _attn(q, k_bf, v_bf, bias, lam_params, lam_init, out_gain, layer):
    b, l, width = q.shape
    dv = 2 * HEAD_DIM
    assert dv == LANES
    n_heads = width // dv
    tq = tk = _attn_tile(l)
    sub = bias.shape[-1]
    lay3 = lambda bi, h, qi: (layer, 0, 0)
    return pl.pallas_call(
        functools.partial(_attn_kernel, tk, sub),
        grid=(b, n_heads, l // tq),
        in_specs=[pl.BlockSpec((None, tq, dv), lambda bi, h, qi: (bi, qi, h)),
                  pl.BlockSpec((None, l, dv), lambda bi, h, qi: (bi, 0, h)),
                  pl.BlockSpec((None, l, dv), lambda bi, h, qi: (bi, 0, h)),
                  pl.BlockSpec((None, 2, sub, sub), lambda bi, h, qi: (h, 0, 0, 0)),
                  pl.BlockSpec((None,) + lam_params.shape[1:], lay3),
                  pl.BlockSpec((None, 1, LANES), lay3),
                  pl.BlockSpec((None, 1, dv), lay3)],
        out_specs=pl.BlockSpec((None, tq, dv), lambda bi, h, qi: (bi, qi, h)),
        out_shape=jax.ShapeDtypeStruct((b, l, width), BF16),
        scratch_shapes=[pltpu.VMEM((2 * tq, dv), BF16),
                        pltpu.VMEM((2 * tq, LANES), F32),
                        pltpu.VMEM((2 * tq, 2 * dv), F32),
                        pltpu.VMEM((l, 2 * dv), BF16),
                        pltpu.VMEM((2 * tq, tk), F32),
                        pltpu.VMEM((2 * tq, tk), F32)],
        compiler_params=pltpu.CompilerParams(
            dimension_semantics=("parallel", "parallel", "arbitrary"), vmem_limit_bytes=VMEM_LIMIT),
        name="attn",
    )(q, k_bf, v_bf, bias, lam_params, lam_init, out_gain)


def _decode_kernel(n_pp, n_heads, pt_ref, q_ref, kn_ref, vn_ref, bias_ref, lp_ref, li_ref, gain_ref, *rest):
    k_refs = rest[:n_pp]
    v_refs = rest[n_pp:2 * n_pp]
    o_ref, q_s, m_s, l_s, acc_s, kpad_s, vpad_s = rest[2 * n_pp:]
    t_new = q_ref.shape[0]
    page = k_refs[0].shape[0] // n_heads
    dv = 2 * HEAD_DIM
    rows = 2 * t_new
    j = pl.program_id(1)
    n_j = pl.num_programs(1)

    @pl.when(j == 0)
    def _():
        m_s[...] = jnp.full_like(m_s, NEG_BIG)
        l_s[...] = jnp.zeros_like(l_s)
        acc_s[...] = jnp.zeros_like(acc_s)
        kpad_s[...] = jnp.zeros_like(kpad_s)
        vpad_s[...] = jnp.zeros_like(vpad_s)
        lane = lax.broadcasted_iota(jnp.int32, (t_new, dv), 1)
        for h in range(n_heads):
            cs = slice(h * dv, (h + 1) * dv)
            qh = q_ref[:, cs]
            q_s[h, 0:t_new, :] = jnp.where(lane < HEAD_DIM, qh, 0.0)
            q_s[h, t_new:rows, :] = jnp.where(lane >= HEAD_DIM, qh, 0.0)
            kpad_s[h, 0:t_new, :] = kn_ref[pl.ds(h, t_new, stride=n_heads), :]
            vpad_s[h, 0:t_new, :] = vn_ref[pl.ds(h, t_new, stride=n_heads), :]

    def head_rows(ref, h):
        return ref[pl.ds(h, page, stride=n_heads), :]

    def step(k_get, v_get, bias_list):
        heads = range(n_heads)
        s_all, m_all, p_all = [], [], []
        for h in heads:
            qh = q_s[h]
            s_list = []
            for i, bb in enumerate(bias_list):
                s = _dot_nt(qh, k_get(i, h))
                s_list.append(s if bb is None else s + bb[h * rows:(h + 1) * rows, :])
            s_all.append(s_list)
        for h in heads:
            m_cur = s_all[h][0]
            for s in s_all[h][1:]:
                m_cur = jnp.maximum(m_cur, s)
            m_all.append(jnp.maximum(m_s[h], m_cur.max(axis=-1, keepdims=True)))
        for h in heads:
            m_keys = m_all[h] if page == LANES else m_all[h][:, 0:1]
            p_all.append([jnp.exp2(s - m_keys) for s in s_all[h]])
        for h in heads:
            alpha = jnp.exp2(m_s[h] - m_all[h])
            psum = p_all[h][0]
            pv = _dot(p_all[h][0], v_get(0, h))
            for i in range(1, len(p_all[h])):
                psum = psum + p_all[h][i]
                pv = pv + _dot(p_all[h][i], v_get(i, h))
            l_s[h] = alpha * l_s[h] + jnp.sum(psum, axis=-1, keepdims=True)
            acc_s[h] = alpha * acc_s[h] + pv
            m_s[h] = m_all[h]

    @pl.when(j < n_j - 1)
    def _():
        step(lambda i, h: head_rows(k_refs[i], h), lambda i, h: head_rows(v_refs[i], h), [None] * n_pp)

    @pl.when(j == n_j - 1)
    def _():
        k_get = lambda i, h: head_rows(k_refs[i], h) if i < n_pp else kpad_s[h]
        v_get = lambda i, h: head_rows(v_refs[i], h) if i < n_pp else vpad_s[h]
        step(k_get, v_get, [None] * (n_pp - 1) + [bias_ref[0], bias_ref[1]])
        lam = _lambda(lp_ref, li_ref)
        scale = 1.0 - li_ref[:, 0:1]
        for h in range(n_heads):
            on = acc_s[h] / l_s[h]
            o = on[0:t_new] - lam * on[t_new:rows]
            ms = jnp.mean(o * o, axis=-1, keepdims=True)
            o_ref[:, h * dv:(h + 1) * dv] = (o * lax.rsqrt(ms + RMS_EPS) * gain_ref[...] * scale).astype(o_ref.dtype)


def _decode(q, k_new, v_new, cache_k, cache_v, page_table, bias, lam_params, lam_init, out_gain, layer,
            pages_per_step):
    b, t_new, width = q.shape
    dv = 2 * HEAD_DIM
    n_heads = width // dv
    page = cache_k.shape[2] // n_heads
    n_pages = page_table.shape[1]
    n_pp = pages_per_step
    assert n_pages % n_pp == 0 and t_new <= page
    rows = 2 * t_new
    seq = lambda bi, j, pt: (bi, 0, 0)
    lay3 = lambda bi, j, pt: (layer, 0, 0)

    def page_spec(i):
        return pl.BlockSpec((None, None, page * n_heads, dv),
                            lambda bi, j, pt: (layer, pt[bi, j * n_pp + i], 0, 0))

    grid_spec = pltpu.PrefetchScalarGridSpec(
        num_scalar_prefetch=1,
        grid=(b, n_pages // n_pp),
        in_specs=[pl.BlockSpec((None, t_new, width), seq),
                  pl.BlockSpec((None, t_new * n_heads, dv), lambda bi, j, pt: (layer, bi, 0)),
                  pl.BlockSpec((None, t_new * n_heads, dv), lambda bi, j, pt: (layer, bi, 0)),
                  pl.BlockSpec(bias.shape, lambda bi, j, pt: (0, 0, 0)),
                  pl.BlockSpec((None,) + lam_params.shape[1:], lay3),
                  pl.BlockSpec((None, 1, LANES), lay3),
                  pl.BlockSpec((None, 1, dv), lay3)]
                 + [page_spec(i) for i in range(n_pp)] * 2,
        out_specs=pl.BlockSpec((None, t_new, width), seq),
        scratch_shapes=[pltpu.VMEM((n_heads, rows, dv), F32),
                        pltpu.VMEM((n_heads, rows, LANES), F32),
                        pltpu.VMEM((n_heads, rows, LANES), F32),
                        pltpu.VMEM((n_heads, rows, dv), F32),
                        pltpu.VMEM((n_heads, page, dv), F32),
                        pltpu.VMEM((n_heads, page, dv), F32)])
    return pl.pallas_call(
        functools.partial(_decode_kernel, n_pp, n_heads),
        grid_spec=grid_spec,
        out_shape=jax.ShapeDtypeStruct((b, t_new, width), BF16),
        compiler_params=pltpu.CompilerParams(
            dimension_semantics=("parallel", "arbitrary"), vmem_limit_bytes=VMEM_LIMIT),
        name="decode",
    )(page_table, q, k_new, v_new, bias, lam_params, lam_init, out_gain,
      *([cache_k] * n_pp), *([cache_v] * n_pp))


def _pad_lanes(a, width=LANES):
    return jnp.pad(a, [(0, 0)] * (a.ndim - 1) + [(0, width - a.shape[-1])])


def kernel(x_prompt, x_sample, cache_k, cache_v, state_gdn, state_conv, page_table, ffn1_norm, ffn1_w_gate_up, ffn1_w_down, mix_norm, w_in, conv_w, gdn_a_log, gdn_dt_bias, gdn_out_norm, diff_q_norm, diff_k_norm, diff_lambda, diff_out_norm, rel_bias, w_out, ffn2_norm, ffn2_w_gate_up, ffn2_w_down):
    depth, d_model, _ = w_in.shape
    n_gdn = gdn_a_log.shape[1]
    n_diff = rel_bias.shape[1]
    gdn_w = n_gdn * HEAD_DIM
    diff_w = n_diff * 2 * HEAD_DIM
    conv_dim = conv_w.shape[2]
    assert conv_dim == 3 * gdn_w and w_out.shape[1] == gdn_w + diff_w
    page = cache_k.shape[2]
    assert page >= MAX_DISTANCE
    past_len = page_table.shape[1] * page

    o = 0
    cols = {}
    for name, wd in (("qkv", conv_dim), ("gate", gdn_w), ("b", n_gdn), ("a", n_gdn),
                     ("dq", diff_w), ("dk", diff_w), ("dv", diff_w)):
        cols[name] = w_in[:, :, o:o + wd]
        o += wd
    w_in_r = jnp.concatenate([cols["qkv"], cols["gate"], cols["dq"], cols["dk"], cols["dv"],
                              _pad_lanes(cols["b"]), _pad_lanes(cols["a"])], axis=-1).astype(BF16)
    widths = (conv_dim, gdn_w, diff_w, 2 * LANES)
    bf = lambda a: a.astype(BF16)
    w1_gu, w1_d, w2_gu, w2_d, w_out_bf = bf(ffn1_w_gate_up), bf(ffn1_w_down), bf(ffn2_w_gate_up), bf(ffn2_w_down), bf(w_out)
    row3 = lambda a: a.reshape(depth, 1, -1)
    n1, n2, nm = row3(ffn1_norm), row3(ffn2_norm), row3(mix_norm)
    q_gain = row3(jnp.tile(diff_q_norm, (1, diff_w // HEAD_DIM)))
    k_gain = row3(jnp.tile(diff_k_norm, (1, diff_w // HEAD_DIM)))
    a_log = row3(_pad_lanes(gdn_a_log))
    dt_bias = row3(_pad_lanes(gdn_dt_bias))
    gdn_gain = row3(jnp.tile(gdn_out_norm, (1, n_gdn)))
    diff_gain = row3(diff_out_norm)
    lam_init = jnp.asarray([0.8 - 0.6 * math.exp(-0.3 * l) for l in range(depth)], F32)
    lam_init = jnp.broadcast_to(lam_init[:, None, None], (depth, 1, LANES))
    cache_k2 = cache_k.reshape(cache_k.shape[:2] + (page * n_diff, 2 * HEAD_DIM))
    cache_v2 = cache_v.reshape(cache_v.shape[:2] + (page * n_diff, 2 * HEAD_DIM))

    def bias_prompt(l):
        t = _attn_tile(l)
        sub = min(t, MAX_DISTANCE)
        assert t % sub == 0 and (l == t or sub == MAX_DISTANCE)
        i = jnp.arange(sub, dtype=jnp.int32)
        rel = i[:, None] - i[None, :]
        return jnp.stack([_bias_tile(rel_bias, rel), _bias_tile(rel_bias, rel + sub)], axis=1)

    def bias_sample(t_new):
        tok = jnp.arange(t_new, dtype=jnp.int32)
        lane = jnp.arange(page, dtype=jnp.int32)
        rel_last = (page + tok)[:, None] - lane[None, :]
        rel_new = jnp.where(lane[None, :] < t_new, tok[:, None] - lane[None, :], -1)
        tiles = jnp.stack([_bias_tile(rel_bias, rel_last), _bias_tile(rel_bias, rel_new)], axis=0)
        tiles = jnp.broadcast_to(tiles[:, :, None], (2, n_diff, 2, t_new, page))
        return tiles.reshape(2, n_diff * 2 * t_new, page)

    def run(x, paged):
        b, l, _ = x.shape
        m = b * l
        xf = x.reshape(m, d_model)
        ss, cs = [], []
        k_all = jnp.zeros((depth, m * n_diff, 2 * HEAD_DIM), F32)
        v_all = jnp.zeros((depth, m * n_diff, 2 * HEAD_DIM), F32)
        if paged:
            bias = bias_sample(l)
        else:
            bias = bias_prompt(l)
        for layer in range(depth):
            xf = _ffn(xf, n1, w1_gu, w1_d, layer)
            qkv, gate, ba, q, k_all, v_all, k_bf, v_bf = _inproj(xf, nm, w_in_r, q_gain, k_gain, widths, layer,
                                                                 k_all, v_all)
            r3 = lambda a: a.reshape(b, l, a.shape[-1])
            if paged:
                conv_buf, s0, bb = state_conv[layer], state_gdn[layer], 4 if b % 4 == 0 else 1
            else:
                conv_buf = jnp.zeros((b, conv_w.shape[1] - 1, conv_dim), F32)
                s0 = jnp.zeros((b, n_gdn, HEAD_DIM, HEAD_DIM), F32)
                bb = b
            o_gdn, s_new, conv_new = _gdn(r3(qkv), r3(gate), r3(ba), conv_buf, s0, conv_w, a_log, dt_bias,
                                          gdn_gain, layer, bb)
            if paged:
                n_pages = page_table.shape[1]
                o_diff = _decode(r3(q), k_all, v_all, cache_k2, cache_v2, page_table, bias, diff_lambda, lam_init,
                                 diff_gain, layer, next(n for n in (16, 8, 4, 2, 1) if n_pages % n == 0))
            else:
                o_diff = _attn(r3(q), r3(k_bf), r3(v_bf), bias, diff_lambda, lam_init, diff_gain, layer)
            xf = _ffn(xf, n2, w2_gu, w2_d, layer,
                      mix=(o_gdn.reshape(m, gdn_w), o_diff.reshape(m, diff_w), w_out_bf))
            ss.append(s_new)
            cs.append(conv_new)
        kv_shape = (depth, b, l, n_diff, 2 * HEAD_DIM)
        return (xf.reshape(b, l, d_model), k_all.reshape(kv_shape), v_all.reshape(kv_shape),
                jnp.stack(ss), jnp.stack(cs))

    y_p, k_p, v_p, s_p, c_p = run(x_prompt, False)
    y_s, k_s, v_s, s_s, c_s = run(x_sample, True)
    return (y_p, y_s, k_p, v_p, s_p, c_p, k_s, v_s, s_s, c_s)
```

```python
import functools
import math

import jax
import jax.numpy as jnp
from jax import lax
from jax.experimental import pallas as pl
from jax.experimental.pallas import tpu as pltpu

F32 = jnp.float32
BF16 = jnp.bfloat16

HEAD_DIM = 64
GDN_CHUNK = 64
MAX_DISTANCE = 128
RMS_EPS = 1e-6
L2_EPS = 1e-6
LANES = 128
MXU_DIM = 256
VMEM_LIMIT = 52 * 1024 * 1024
NEG_BIG = -1e30
LOG2E = 1.4426950408889634


def _dot(a, b):
    return jnp.dot(a, b, preferred_element_type=F32)


def _dot_nt(a, b):
    return lax.dot_general(a, b, (((1,), (1,)), ((), ())), preferred_element_type=F32)


def _dot_tn(a, b):
    return lax.dot_general(a, b, (((0,), (0,)), ((), ())), preferred_element_type=F32)


def _split3(x):
    hi = x.astype(BF16)
    r = x - hi.astype(F32)
    mid = r.astype(BF16)
    lo = (r - mid.astype(F32)).astype(BF16)
    return hi, mid, lo


def _dot_exact_lhs(a_bf, b):
    b0, b1, b2 = _split3(b)
    return _dot(a_bf, b0) + _dot(a_bf, b1) + _dot(a_bf, b2)


def _group_sumsq(x, gmat):
    sq = x * x
    hi = sq.astype(BF16)
    lo = (sq - hi.astype(F32)).astype(BF16)
    return _dot(hi, gmat) + _dot(lo, gmat)


def _group_matrix(width):
    r = lax.broadcasted_iota(jnp.int32, (width, width), 0) // HEAD_DIM
    c = lax.broadcasted_iota(jnp.int32, (width, width), 1) // HEAD_DIM
    return jnp.where(r == c, 1.0, 0.0).astype(BF16)


def _ffn_kernel(has_mix, cf, *refs):
    if has_mix:
        x_ref, og_ref, od_ref, wo_ref, nw_ref, wgu_ref, wd_ref, o_ref, h_s, a_s = refs
    else:
        x_ref, nw_ref, wgu_ref, wd_ref, o_ref, h_s, a_s = refs
    d_ff = wd_ref.shape[0]
    x = x_ref[...]
    if has_mix:
        half = og_ref.shape[-1]
        x = x + _dot(og_ref[...], wo_ref[0:half, :]) + _dot(od_ref[...], wo_ref[half:, :])
    ms = jnp.mean(x * x, axis=-1, keepdims=True)
    h_s[...] = (x * lax.rsqrt(ms + RMS_EPS) * nw_ref[...]).astype(BF16)
    for c0 in range(0, d_ff, cf):
        h = h_s[...]
        g = _dot(h, wgu_ref[:, c0:c0 + cf])
        u = _dot(h, wgu_ref[:, d_ff + c0:d_ff + c0 + cf])
        a_s[:, c0:c0 + cf] = ((g * jax.nn.sigmoid(g)) * u).astype(BF16)
    o_ref[...] = x + 0.5 * _dot(a_s[...], wd_ref[...])


def _ffn_tiles(m, d_ff):
    tm = 512 if m % 512 == 0 else m
    cf = 2 * LANES if d_ff % (2 * LANES) == 0 else d_ff
    return tm, cf


def _ffn(x, norm_w, w_gu, w_down, layer, mix=None):
    m, d = x.shape
    d_ff = w_down.shape[1]
    tm, cf = _ffn_tiles(m, d_ff)
    row = lambda i: (i, 0)
    lay3 = lambda i: (layer, 0, 0)
    once = pl.Buffered(1)
    in_specs = [pl.BlockSpec((tm, d), row)]
    args = [x]
    if mix is not None:
        og, od, w_out = mix
        in_specs += [pl.BlockSpec((tm, og.shape[1]), row), pl.BlockSpec((tm, od.shape[1]), row),
                     pl.BlockSpec((None,) + w_out.shape[1:], lay3, pipeline_mode=once)]
        args += [og, od, w_out]
    in_specs += [
        pl.BlockSpec((None, 1, d), lay3),
        pl.BlockSpec((None, d, 2 * d_ff), lay3, pipeline_mode=once),
        pl.BlockSpec((None, d_ff, d), lay3, pipeline_mode=once),
    ]
    args += [norm_w, w_gu, w_down]
    return pl.pallas_call(
        functools.partial(_ffn_kernel, mix is not None, cf),
        grid=(m // tm,),
        in_specs=in_specs,
        out_specs=pl.BlockSpec((tm, d), row),
        out_shape=jax.ShapeDtypeStruct((m, d), F32),
        scratch_shapes=[pltpu.VMEM((tm, d), BF16), pltpu.VMEM((tm, d_ff), BF16)],
        compiler_params=pltpu.CompilerParams(
            dimension_semantics=("parallel",), vmem_limit_bytes=VMEM_LIMIT),
        name="ffn_mix" if mix is not None else "ffn",
    )(*args)


def _inproj_kernel(widths, x_ref, nw_ref, w_ref, qg_ref, kg_ref, k_prev_ref, v_prev_ref,
                   qkv_ref, gate_ref, ba_ref, q_ref, k_ref, v_ref, kb_ref, vb_ref):
    del k_prev_ref, v_prev_ref
    c_qkv, c_gate, c_d, c_ba = widths
    x = x_ref[...]
    ms = jnp.mean(x * x, axis=-1, keepdims=True)
    h = (x * lax.rsqrt(ms + RMS_EPS) * nw_ref[...]).astype(BF16)
    o = 0
    qkv_ref[...] = _dot(h, w_ref[:, o:o + c_qkv]); o += c_qkv
    gate_ref[...] = _dot(h, w_ref[:, o:o + c_gate]); o += c_gate
    dq = _dot(h, w_ref[:, o:o + c_d]); o += c_d
    dk = _dot(h, w_ref[:, o:o + c_d]); o += c_d
    dv = _dot(h, w_ref[:, o:o + c_d]); o += c_d
    ba_ref[...] = _dot(h, w_ref[:, o:o + c_ba])
    gmat = _group_matrix(c_d)
    inv_hd = 1.0 / HEAD_DIM
    qn = dq * lax.rsqrt(_group_sumsq(dq, gmat) * inv_hd + RMS_EPS) * qg_ref[...]
    kn = dk * lax.rsqrt(_group_sumsq(dk, gmat) * inv_hd + RMS_EPS) * kg_ref[...]
    q_ref[...] = qn * (HEAD_DIM ** -0.5 * LOG2E)
    tm = kn.shape[0]
    dv_w = 2 * HEAD_DIM
    n_heads = c_d // dv_w
    for h in range(n_heads):
        k_ref[pl.ds(h, tm, stride=n_heads), :] = kn[:, h * dv_w:(h + 1) * dv_w]
        v_ref[pl.ds(h, tm, stride=n_heads), :] = dv[:, h * dv_w:(h + 1) * dv_w]
    kb_ref[...] = kn.astype(BF16)
    vb_ref[...] = dv.astype(BF16)


def _inproj(x, norm_w, w_in, q_gain, k_gain, widths, layer, k_all, v_all):
    m, d = x.shape
    c_qkv, c_gate, c_d, c_ba = widths
    dv_w = 2 * HEAD_DIM
    n_heads = c_d // dv_w
    tm = 512 if m % 512 == 0 else m
    row = lambda i: (i, 0)
    lay3 = lambda i: (layer, 0, 0)
    slab = pl.BlockSpec((None, tm * n_heads, dv_w), lambda i: (layer, i, 0))
    outs = [(c_qkv, F32), (c_gate, F32), (c_ba, F32), (c_d, F32), None, None, (c_d, BF16), (c_d, BF16)]
    return pl.pallas_call(
        functools.partial(_inproj_kernel, widths),
        grid=(m // tm,),
        in_specs=[pl.BlockSpec((tm, d), row),
                  pl.BlockSpec((None, 1, d), lay3),
                  pl.BlockSpec((None,) + w_in.shape[1:], lay3),
                  pl.BlockSpec((None, 1, c_d), lay3),
                  pl.BlockSpec((None, 1, c_d), lay3),
                  pl.BlockSpec(memory_space=pl.ANY),
                  pl.BlockSpec(memory_space=pl.ANY)],
        out_specs=[slab if o is None else pl.BlockSpec((tm, o[0]), row) for o in outs],
        out_shape=[jax.ShapeDtypeStruct(k_all.shape, F32) if o is None else jax.ShapeDtypeStruct((m, o[0]), o[1])
                   for o in outs],
        input_output_aliases={5: 4, 6: 5},
        compiler_params=pltpu.CompilerParams(
            dimension_semantics=("parallel",), vmem_limit_bytes=VMEM_LIMIT),
        name="inproj",
    )(x, norm_w, w_in, q_gain, k_gain, k_all, v_all)


def _block_mask(rows, cols, row_group, col_group):
    r = lax.broadcasted_iota(jnp.int32, (rows, cols), 0) // row_group
    c = lax.broadcasted_iota(jnp.int32, (rows, cols), 1) // col_group
    return r == c


def _block_diag(x, mask):
    reps = mask.shape[0] // x.shape[0]
    return jnp.where(mask, jnp.concatenate([x] * reps, axis=0), 0.0).astype(BF16)


def _dot_exact_rhs(a, b_bf):
    a0, a1, a2 = _split3(a)
    return _dot(a0, b_bf) + _dot(a1, b_bf) + _dot(a2, b_bf)


def _gdn_kernel(n_heads, chunk, gh, nc, qkv_ref, gate_ref, ba_ref, cbuf_ref, s0_ref, cw_ref, alog_ref, dtb_ref,
                gain_ref, o_ref, snew_ref, cnew_ref, xp_s, s_s):
    bb = qkv_ref.shape[0]
    c = chunk
    tb = nc * c
    d = HEAD_DIM
    width = n_heads * d
    n_grp = n_heads // gh
    tw = gh * c
    dw = gh * d
    n_tail = cbuf_ref.shape[1]
    n_taps = n_tail + 1
    top = 8 - n_tail
    t = pl.program_id(1)
    n_t = pl.num_programs(1)

    @pl.when(t == 0)
    def _():
        xp_s[:, top:8, :] = cbuf_ref[...]
        s_s[...] = jnp.zeros_like(s_s)
        for b in range(bb):
            for h in range(n_heads):
                o = (h % gh) * d
                s_s[b, h // gh, o:o + d, o:o + d] = s0_ref[b, h]

    xp_s[:, 8:8 + tb, :] = qkv_ref[...]

    cw = cw_ref[...]
    gmat = _group_matrix(width)
    ri = lax.broadcasted_iota(jnp.int32, (c, c), 0)
    ci = lax.broadcasted_iota(jnp.int32, (c, c), 1)
    tril_bf = jnp.where(ri >= ci, 1.0, 0.0).astype(BF16)
    rb = lax.broadcasted_iota(jnp.int32, (tb, tb), 0)
    cb = lax.broadcasted_iota(jnp.int32, (tb, tb), 1)
    tril_chunks = jnp.where((rb >= cb) & (rb // c == cb // c), 1.0, 0.0).astype(BF16)
    row_t = lax.broadcasted_iota(jnp.int32, (c, tw), 0)
    col_t = lax.broadcasted_iota(jnp.int32, (c, tw), 1) % c
    causal = row_t >= col_t
    strict = row_t > col_t
    eye = jnp.where(row_t == col_t, 1.0, 0.0).astype(F32)
    bd_tt = _block_mask(tw, tw, c, c)
    bd_td = _block_mask(tw, dw, c, d)
    bd_dd = _block_mask(dw, dw, d, d)
    gmat_g = jnp.where(bd_dd, 1.0, 0.0).astype(BF16)
    expand_d = jnp.where(_block_mask(LANES, width, 1, d), 1.0, 0.0).astype(BF16)
    expand_t = jnp.where(_block_mask(LANES, n_heads * c, 1, c), 1.0, 0.0).astype(BF16)
    neg_a = -jnp.exp(alog_ref[...])
    dtb = dtb_ref[...]
    gain = gain_ref[...]

    chains = []
    for b in range(bb):
        y = xp_s[b, top:top + tb, :] * cw[0:1, :]
        for i in range(1, n_taps):
            y = y + xp_s[b, top + i:top + i + tb, :] * cw[i:i + 1, :]
        qkv = y * jax.nn.sigmoid(y)
        q_all = qkv[:, 0:width]
        k_all = qkv[:, width:2 * width]
        v_all = qkv[:, 2 * width:3 * width]
        q_all = q_all * lax.rsqrt(_group_sumsq(q_all, gmat) + L2_EPS) * (HEAD_DIM ** -0.5)
        k_all = k_all * lax.rsqrt(_group_sumsq(k_all, gmat) + L2_EPS)
        ba = ba_ref[b]
        beta_all = jax.nn.sigmoid(ba[:, 0:LANES])
        z = ba[:, LANES:2 * LANES] + dtb
        softplus = jnp.maximum(z, 0.0) + jnp.log1p(jnp.exp(-jnp.abs(z)))
        g_all = neg_a * softplus
        gc_all = _dot_exact_lhs(tril_chunks, g_all)
        gate_all = gate_ref[b]
        gate_all = gate_all * jax.nn.sigmoid(gate_all)
        beta_e = _dot_exact_rhs(beta_all, expand_d)
        gc_e = _dot_exact_rhs(gc_all, expand_d)
        g_t = _dot_exact_rhs(g_all, expand_t)
        egc_e = jnp.exp(gc_e)

        for cc in range(nc):
            rows = slice(cc * c, (cc + 1) * c)
            glast = gc_e[(cc + 1) * c - 1:(cc + 1) * c, :]
            kend_e = jnp.exp(glast - gc_e[rows])
            sdec_e = jnp.exp(glast)
            for gi in range(n_grp):
                sl = slice(gi * dw, (gi + 1) * dw)
                k = k_all[rows, sl]
                beta = beta_e[rows, sl]
                chains.append(dict(b=b, cc=cc, gi=gi, rows=rows, sl=sl, q=q_all[rows, sl], k=k, kb=k * beta,
                                   vb=v_all[rows, sl] * beta, egc=egc_e[rows, sl],
                                   g_t=g_t[rows, gi * tw:(gi + 1) * tw], kend=kend_e[:, sl], sdec=sdec_e[:, sl],
                                   gate=gate_all[rows, sl]))

    for ch in chains:
        dm = _dot_exact_lhs(tril_bf, jnp.where(strict, ch["g_t"], 0.0))
        ch["decay"] = jnp.where(causal, jnp.exp(dm), 0.0)
        ch["k_bd"] = _block_diag(ch["k"], bd_td)
    for ch in chains:
        ch["a"] = jnp.where(strict, _dot_nt(ch["kb"].astype(BF16), ch["k_bd"]) * ch["decay"], 0.0)
        ch["tinv"] = eye - jnp.where(row_t // 2 == col_t // 2, ch["a"], 0.0)
    size = 2
    while size < c:
        lower_left = ((row_t // (2 * size) == col_t // (2 * size))
                      & (row_t % (2 * size) >= size) & (col_t % (2 * size) < size))
        for ch in chains:
            a21 = _block_diag(jnp.where(lower_left, ch["a"], 0.0), bd_tt)
            ch["t2_a21"] = _dot(ch["tinv"].astype(BF16), a21)
        for ch in chains:
            ch["tinv"] = ch["tinv"] - _dot(ch["t2_a21"].astype(BF16), _block_diag(ch["tinv"], bd_tt))
        size *= 2
    for ch in chains:
        tinv_bf = ch["tinv"].astype(BF16)
        ch["u"] = _dot(tinv_bf, _block_diag(ch["vb"], bd_td))
        ch["w"] = _dot(tinv_bf, _block_diag(ch["kb"] * ch["egc"], bd_td))
        ch["qk"] = jnp.where(causal, _dot_nt(ch["q"].astype(BF16), ch["k_bd"]) * ch["decay"], 0.0)
    for cc in range(nc):
        now = [ch for ch in chains if ch["cc"] == cc]
        for ch in now:
            ch["s"] = s_s[ch["b"], ch["gi"]]
            lhs = jnp.concatenate([ch["w"], ch["q"] * ch["egc"]], axis=0).astype(BF16)
            ch["ws"] = _dot(lhs, ch["s"].astype(BF16))
        for ch in now:
            v_new = ch["u"] - ch["ws"][0:c]
            ch["o"] = ch["ws"][c:2 * c] + _dot(ch["qk"].astype(BF16), _block_diag(v_new, bd_td))
            k_end = ch["k"] * ch["kend"]
            cross = _dot_tn(k_end.astype(BF16), v_new.astype(BF16))
            s_s[ch["b"], ch["gi"]] = ch["s"] * ch["sdec"] + jnp.where(bd_dd, cross, 0.0)
    for ch in chains:
        o = ch["o"]
        ms = _group_sumsq(o, gmat_g) * (1.0 / d)
        on = o * lax.rsqrt(ms + RMS_EPS) * gain[:, ch["sl"]]
        o_ref[ch["b"], ch["rows"], ch["sl"]] = (on * ch["gate"]).astype(o_ref.dtype)

    tail = xp_s[:, 8 + tb - n_tail:8 + tb, :]
    xp_s[:, top:8, :] = tail

    @pl.when(t == n_t - 1)
    def _():
        cnew_ref[...] = tail
        for b in range(bb):
            for h in range(n_heads):
                o = (h % gh) * d
                snew_ref[b, h] = s_s[b, h // gh, o:o + d, o:o + d]


def _gdn(qkv, gate, ba, conv_buf, s0, conv_w, a_log, dt_bias, out_gain, layer, batch_block):
    b, l, w3 = qkv.shape
    width = w3 // 3
    n_heads = width // HEAD_DIM
    chunk = min(GDN_CHUNK, l)
    assert l % chunk == 0 and chunk % 8 == 0 and chunk & (chunk - 1) == 0 and b % batch_block == 0
    n_tail = conv_buf.shape[1]
    assert n_tail <= min(8, chunk)
    bb = batch_block
    gh = min(n_heads, max(1, MXU_DIM // chunk))
    assert n_heads % gh == 0
    nc = next(n for n in (4, 2, 1) if (l // chunk) % n == 0)
    tb = nc * chunk
    blk = lambda i, t: (i, t, 0)
    fix3 = lambda i, t: (i, 0, 0)
    lay3 = lambda i, t: (layer, 0, 0)
    return pl.pallas_call(
        functools.partial(_gdn_kernel, n_heads, chunk, gh, nc),
        grid=(b // bb, l // tb),
        in_specs=[pl.BlockSpec((bb, tb, w3), blk),
                  pl.BlockSpec((bb, tb, width), blk),
                  pl.BlockSpec((bb, tb, ba.shape[2]), blk),
                  pl.BlockSpec((bb, n_tail, w3), fix3),
                  pl.BlockSpec((bb, n_heads, HEAD_DIM, HEAD_DIM), lambda i, t: (i, 0, 0, 0)),
                  pl.BlockSpec((None,) + conv_w.shape[1:], lay3),
                  pl.BlockSpec((None, 1, LANES), lay3),
                  pl.BlockSpec((None, 1, LANES), lay3),
                  pl.BlockSpec((None, 1, width), lay3)],
        out_specs=[pl.BlockSpec((bb, tb, width), blk),
                   pl.BlockSpec((bb, n_heads, HEAD_DIM, HEAD_DIM), lambda i, t: (i, 0, 0, 0)),
                   pl.BlockSpec((bb, n_tail, w3), fix3)],
        out_shape=[jax.ShapeDtypeStruct((b, l, width), BF16),
                   jax.ShapeDtypeStruct(s0.shape, F32),
                   jax.ShapeDtypeStruct(conv_buf.shape, F32)],
        scratch_shapes=[pltpu.VMEM((bb, 8 + tb, w3), F32),
                        pltpu.VMEM((bb, n_heads // gh, gh * HEAD_DIM, gh * HEAD_DIM), F32)],
        compiler_params=pltpu.CompilerParams(
            dimension_semantics=("parallel", "arbitrary"), vmem_limit_bytes=VMEM_LIMIT),
        name="gdn",
    )(qkv, gate, ba, conv_buf, s0, conv_w, a_log, dt_bias, out_gain)


def _rel_bucket(rel, n_buckets):
    n = jnp.maximum(rel, 0)
    max_exact = n_buckets // 2
    nf = jnp.maximum(n, 1).astype(F32)
    large = max_exact + (jnp.log(nf / max_exact) / math.log(MAX_DISTANCE / max_exact)
                         * (n_buckets - max_exact)).astype(jnp.int32)
    large = jnp.minimum(large, n_buckets - 1)
    return jnp.where(n < max_exact, n, large)


def _bias_tile(rel_bias, rel):
    n_buckets, n_heads = rel_bias.shape
    tab = rel_bias.astype(F32) - rel_bias[n_buckets - 1].astype(F32)[None, :]
    bucket = _rel_bucket(rel, n_buckets)[None]
    bias = jnp.zeros((n_heads,) + rel.shape, F32)
    for n in range(n_buckets):
        bias = jnp.where(bucket == n, tab[n].reshape((n_heads,) + (1,) * rel.ndim), bias)
    return jnp.where((rel >= 0)[None], bias * LOG2E, -jnp.inf)


def _lambda(lp_ref, li_ref):
    lp = lp_ref[...]
    e1 = jnp.exp(jnp.sum(lp[0:1, :] * lp[1:2, :], axis=-1, keepdims=True))
    e2 = jnp.exp(jnp.sum(lp[2:3, :] * lp[3:4, :], axis=-1, keepdims=True))
    return e1 - e2 + li_ref[:, 0:1]


def _attn_kernel(tk, sub, dec, *refs):
    if dec is None:
        q_ref, k_ref, v_ref, bias_ref, lp_ref, li_ref, gain_ref, o_ref, q_s, m_s, acc_s, vx_s, s0_s, s1_s = refs
    else:
        n_pp, n_dec_heads, n_groups = dec
        q_ref, k_ref, v_ref, bias_ref, lp_ref, li_ref, gain_ref, dq_ref, kn_ref, vn_ref, dbias_ref = refs[1:12]
        k_pages = refs[12:12 + n_pp]
        v_pages = refs[12 + n_pp:12 + 2 * n_pp]
        o_ref, od_ref, q_s, m_s, acc_s, vx_s, s0_s, s1_s = refs[12 + 2 * n_pp:20 + 2 * n_pp]
        step_id = ((pl.program_id(0) * pl.num_programs(1) + pl.program_id(1)) * pl.num_programs(2)
                   + pl.program_id(2))
        _decode_step(n_dec_heads, step_id % n_groups, n_groups, dq_ref, kn_ref, vn_ref, dbias_ref, lp_ref, li_ref,
                     gain_ref, k_pages, v_pages, od_ref, *refs[20 + 2 * n_pp:])
    tq = q_ref.shape[0]
    dv = 2 * HEAD_DIM
    qi = pl.program_id(2)

    @pl.when(qi == 0)
    def _():
        vx_s[:, 0:dv] = v_ref[...]
        vx_s[:, dv:2 * dv] = jnp.ones((vx_s.shape[0], dv), BF16)

    q = q_ref[...]
    lane = lax.broadcasted_iota(jnp.int32, q.shape, 1)
    q_s[0:tq, :] = jnp.where(lane < HEAD_DIM, q, 0.0).astype(BF16)
    q_s[tq:2 * tq, :] = jnp.where(lane >= HEAD_DIM, q, 0.0).astype(BF16)
    m_s[...] = jnp.full_like(m_s, NEG_BIG)
    acc_s[...] = jnp.zeros_like(acc_s)

    def scores(buf, j):
        ks = pl.multiple_of(j * tk, tk)
        buf[...] = _dot_nt(q_s[...], k_ref[pl.ds(ks, tk), :])

    def update(buf, j):
        ks = pl.multiple_of(j * tk, tk)
        m_prev = m_s[...]
        m_new = jnp.maximum(m_prev, jnp.max(buf[...], axis=-1, keepdims=True))
        p = jnp.exp2(buf[...] - jnp.concatenate([m_new] * (tk // LANES), axis=1)).astype(BF16)
        alpha = jnp.exp2(m_prev - m_new)
        pv = _dot(p, vx_s[pl.ds(ks, tk), :])
        acc_s[...] = jnp.concatenate([alpha] * (2 * dv // LANES), axis=1) * acc_s[...] + pv
        m_s[...] = m_new

    def below_diagonal_bias(buf):
        for m in range(2):
            buf[m * tq:m * tq + sub, tk - sub:tk] += bias_ref[1]

    def diagonal_bias(buf):
        for m in range(2):
            for r in range(tq // sub):
                rows = slice(m * tq + r * sub, m * tq + (r + 1) * sub)
                if r >= 1:
                    buf[rows, (r - 1) * sub:r * sub] += bias_ref[1]
                buf[rows, r * sub:(r + 1) * sub] += bias_ref[0]
                if (r + 1) * sub < tk:
                    buf[rows, (r + 1) * sub:tk] = jnp.full((sub, tk - (r + 1) * sub), -jnp.inf, F32)

    n_pairs = jnp.maximum(qi - 1, 0) // 2
    scores(s0_s, 0)

    def pair(i, carry):
        scores(s1_s, 2 * i + 1)
        update(s0_s, 2 * i)
        scores(s0_s, 2 * i + 2)
        update(s1_s, 2 * i + 1)
        return carry

    lax.fori_loop(0, n_pairs, pair, 0)

    @pl.when(qi == 0)
    def _():
        diagonal_bias(s0_s)
        update(s0_s, 0)

    @pl.when(qi % 2 == 1)
    def _():
        scores(s1_s, qi)
        below_diagonal_bias(s0_s)
        diagonal_bias(s1_s)
        update(s0_s, qi - 1)
        update(s1_s, qi)

    @pl.when((qi % 2 == 0) & (qi >= 2))
    def _():
        scores(s1_s, qi - 1)
        below_diagonal_bias(s1_s)
        update(s0_s, qi - 2)
        scores(s0_s, qi)
        diagonal_bias(s0_s)
        update(s1_s, qi - 1)
        update(s0_s, qi)

    lam = _lambda(lp_ref, li_ref)
    acc = acc_s[...]
    on = acc[:, 0:dv] * (1.0 / acc[:, dv:2 * dv])
    o = on[0:tq] - lam * on[tq:2 * tq]
    ms = jnp.mean(o * o, axis=-1, keepdims=True)
    o_ref[...] = (o * lax.rsqrt(ms + RMS_EPS) * gain_ref[...] * (1.0 - li_ref[:, 0:1])).astype(o_ref.dtype)


def _attn_tile(l):
    return 512 if l % 512 == 0 else l


def _decode_pages_per_attn_step(attn_steps, n_seq, n_pages):
    total = n_seq * n_pages
    if total % attn_steps:
        return None
    n_pp = total // attn_steps
    return n_pp if (0 < n_pp <= 16 and n_pages % n_pp == 0) else None


def _attn(q, k_bf, v_bf, bias, lam_params, lam_init, out_gain, layer, dec=None):
    b, l, width = q.shape
    dv = 2 * HEAD_DIM
    assert dv == LANES
    n_heads = width // dv
    tq = tk = _attn_tile(l)
    nq = l // tq
    sub = bias.shape[-1]
    lay3 = lambda bi, h, qi, *_: (layer, 0, 0)
    in_specs = [pl.BlockSpec((None, tq, dv), lambda bi, h, qi, *_: (bi, qi, h)),
                pl.BlockSpec((None, l, dv), lambda bi, h, qi, *_: (bi, 0, h)),
                pl.BlockSpec((None, l, dv), lambda bi, h, qi, *_: (bi, 0, h)),
                pl.BlockSpec((None, 2, sub, sub), lambda bi, h, qi, *_: (h, 0, 0, 0)),
                pl.BlockSpec((None,) + lam_params.shape[1:], lay3),
                pl.BlockSpec((None, 1, LANES), lay3),
                pl.BlockSpec((None, 1, dv), lay3)]
    out_specs = pl.BlockSpec((None, tq, dv), lambda bi, h, qi, *_: (bi, qi, h))
    out_shape = jax.ShapeDtypeStruct((b, l, width), BF16)
    scratch = [pltpu.VMEM((2 * tq, dv), BF16),
               pltpu.VMEM((2 * tq, LANES), F32),
               pltpu.VMEM((2 * tq, 2 * dv), F32),
               pltpu.VMEM((l, 2 * dv), BF16),
               pltpu.VMEM((2 * tq, tk), F32),
               pltpu.VMEM((2 * tq, tk), F32)]
    args = [q, k_bf, v_bf, bias, lam_params, lam_init, out_gain]
    if dec is None:
        return pl.pallas_call(
            functools.partial(_attn_kernel, tk, sub, None),
            grid=(b, n_heads, nq), in_specs=in_specs, out_specs=out_specs, out_shape=out_shape,
            scratch_shapes=scratch,
            compiler_params=pltpu.CompilerParams(
                dimension_semantics=("parallel", "parallel", "arbitrary"), vmem_limit_bytes=VMEM_LIMIT),
            name="attn",
        )(*args)

    dq, k_new, v_new, cache_k, cache_v, page_table, dbias, n_pp = dec
    n_seq, t_new, dwidth = dq.shape
    dh = dwidth // dv
    page = cache_k.shape[2] // dh
    n_groups = page_table.shape[1] // n_pp
    assert b * n_heads * nq == n_seq * n_groups and t_new <= page
    rows = 2 * t_new

    def step_id(bi, h, qi):
        return (bi * n_heads + h) * nq + qi

    seq3 = lambda bi, h, qi, pt: (step_id(bi, h, qi) // n_groups, 0, 0)
    new3 = lambda bi, h, qi, pt: (layer, step_id(bi, h, qi) // n_groups, 0)

    def page_spec(i):
        def index(bi, h, qi, pt):
            sid = step_id(bi, h, qi)
            return (layer, pt[sid // n_groups, (sid % n_groups) * n_pp + i], 0, 0)
        return pl.BlockSpec((None, None, page * dh, dv), index)

    in_specs += [pl.BlockSpec((None, t_new, dwidth), seq3),
                 pl.BlockSpec((None, t_new * dh, dv), new3),
                 pl.BlockSpec((None, t_new * dh, dv), new3),
                 pl.BlockSpec(dbias.shape, lambda bi, h, qi, pt: (0, 0, 0))]
    in_specs += [page_spec(i) for i in range(n_pp)] * 2
    scratch += [pltpu.VMEM((dh, rows, dv), F32),
                pltpu.VMEM((dh, rows, LANES), F32),
                pltpu.VMEM((dh, rows, LANES), F32),
                pltpu.VMEM((dh, rows, dv), F32),
                pltpu.VMEM((dh, page, dv), F32),
                pltpu.VMEM((dh, page, dv), F32)]
    grid_spec = pltpu.PrefetchScalarGridSpec(
        num_scalar_prefetch=1, grid=(b, n_heads, nq), in_specs=in_specs,
        out_specs=[out_specs, pl.BlockSpec((None, t_new, dwidth), seq3)], scratch_shapes=scratch)
    return pl.pallas_call(
        functools.partial(_attn_kernel, tk, sub, (n_pp, dh, n_groups)),
        grid_spec=grid_spec,
        out_shape=[out_shape, jax.ShapeDtypeStruct((n_seq, t_new, dwidth), BF16)],
        compiler_params=pltpu.CompilerParams(
            dimension_semantics=("arbitrary", "arbitrary", "arbitrary"), vmem_limit_bytes=VMEM_LIMIT),
        name="attn_decode",
    )(page_table, *args, dq, k_new, v_new, dbias, *([cache_k] * n_pp), *([cache_v] * n_pp))


def _decode_kernel(n_pp, n_heads, pt_ref, q_ref, kn_ref, vn_ref, bias_ref, lp_ref, li_ref, gain_ref, *rest):
    del pt_ref
    _decode_step(n_heads, pl.program_id(1), pl.num_programs(1), q_ref, kn_ref, vn_ref, bias_ref, lp_ref, li_ref,
                 gain_ref, rest[:n_pp], rest[n_pp:2 * n_pp], *rest[2 * n_pp:])


def _decode_step(n_heads, j, n_j, q_ref, kn_ref, vn_ref, bias_ref, lp_ref, li_ref, gain_ref, k_refs, v_refs,
                 o_ref, q_s, m_s, l_s, acc_s, kpad_s, vpad_s):
    n_pp = len(k_refs)
    t_new = q_ref.shape[0]
    page = k_refs[0].shape[0] // n_heads
    dv = 2 * HEAD_DIM
    rows = 2 * t_new

    @pl.when(j == 0)
    def _():
        m_s[...] = jnp.full_like(m_s, NEG_BIG)
        l_s[...] = jnp.zeros_like(l_s)
        acc_s[...] = jnp.zeros_like(acc_s)
        kpad_s[...] = jnp.zeros_like(kpad_s)
        vpad_s[...] = jnp.zeros_like(vpad_s)
        lane = lax.broadcasted_iota(jnp.int32, (t_new, dv), 1)
        for h in range(n_heads):
            cs = slice(h * dv, (h + 1) * dv)
            qh = q_ref[:, cs]
            q_s[h, 0:t_new, :] = jnp.where(lane < HEAD_DIM, qh, 0.0)
            q_s[h, t_new:rows, :] = jnp.where(lane >= HEAD_DIM, qh, 0.0)
            kpad_s[h, 0:t_new, :] = kn_ref[pl.ds(h, t_new, stride=n_heads), :]
            vpad_s[h, 0:t_new, :] = vn_ref[pl.ds(h, t_new, stride=n_heads), :]

    def head_rows(ref, h):
        return ref[pl.ds(h, page, stride=n_heads), :]

    def step(k_get, v_get, bias_list):
        heads = range(n_heads)
        s_all, m_all, p_all = [], [], []
        for h in heads:
            qh = q_s[h]
            s_list = []
            for i, bb in enumerate(bias_list):
                s = _dot_nt(qh, k_get(i, h))
                s_list.append(s if bb is None else s + bb[h * rows:(h + 1) * rows, :])
            s_all.append(s_list)
        for h in heads:
            m_cur = s_all[h][0]
            for s in s_all[h][1:]:
                m_cur = jnp.maximum(m_cur, s)
            m_all.append(jnp.maximum(m_s[h], m_cur.max(axis=-1, keepdims=True)))
        for h in heads:
            m_keys = m_all[h] if page == LANES else m_all[h][:, 0:1]
            p_all.append([jnp.exp2(s - m_keys) for s in s_all[h]])
        for h in heads:
            alpha = jnp.exp2(m_s[h] - m_all[h])
            psum = p_all[h][0]
            pv = _dot(p_all[h][0], v_get(0, h))
            for i in range(1, len(p_all[h])):
                psum = psum + p_all[h][i]
                pv = pv + _dot(p_all[h][i], v_get(i, h))
            l_s[h] = alpha * l_s[h] + jnp.sum(psum, axis=-1, keepdims=True)
            acc_s[h] = alpha * acc_s[h] + pv
            m_s[h] = m_all[h]

    @pl.when(j < n_j - 1)
    def _():
        step(lambda i, h: head_rows(k_refs[i], h), lambda i, h: head_rows(v_refs[i], h), [None] * n_pp)

    @pl.when(j == n_j - 1)
    def _():
        k_get = lambda i, h: head_rows(k_refs[i], h) if i < n_pp else kpad_s[h]
        v_get = lambda i, h: head_rows(v_refs[i], h) if i < n_pp else vpad_s[h]
        step(k_get, v_get, [None] * (n_pp - 1) + [bias_ref[0], bias_ref[1]])
        lam = _lambda(lp_ref, li_ref)
        scale = 1.0 - li_ref[:, 0:1]
        for h in range(n_heads):
            on = acc_s[h] / l_s[h]
            o = on[0:t_new] - lam * on[t_new:rows]
            ms = jnp.mean(o * o, axis=-1, keepdims=True)
            o_ref[:, h * dv:(h + 1) * dv] = (o * lax.rsqrt(ms + RMS_EPS) * gain_ref[...] * scale).astype(o_ref.dtype)


def _decode(q, k_new, v_new, cache_k, cache_v, page_table, bias, lam_params, lam_init, out_gain, layer,
            pages_per_step):
    b, t_new, width = q.shape
    dv = 2 * HEAD_DIM
    n_heads = width // dv
    page = cache_k.shape[2] // n_heads
    n_pages = page_table.shape[1]
    n_pp = pages_per_step
    assert n_pages % n_pp == 0 and t_new <= page
    rows = 2 * t_new
    seq = lambda bi, j, pt: (bi, 0, 0)
    lay3 = lambda bi, j, pt: (layer, 0, 0)

    def page_spec(i):
        return pl.BlockSpec((None, None, page * n_heads, dv),
                            lambda bi, j, pt: (layer, pt[bi, j * n_pp + i], 0, 0))

    grid_spec = pltpu.PrefetchScalarGridSpec(
        num_scalar_prefetch=1,
        grid=(b, n_pages // n_pp),
        in_specs=[pl.BlockSpec((None, t_new, width), seq),
                  pl.BlockSpec((None, t_new * n_heads, dv), lambda bi, j, pt: (layer, bi, 0)),
                  pl.BlockSpec((None, t_new * n_heads, dv), lambda bi, j, pt: (layer, bi, 0)),
                  pl.BlockSpec(bias.shape, lambda bi, j, pt: (0, 0, 0)),
                  pl.BlockSpec((None,) + lam_params.shape[1:], lay3),
                  pl.BlockSpec((None, 1, LANES), lay3),
                  pl.BlockSpec((None, 1, dv), lay3)]
                 + [page_spec(i) for i in range(n_pp)] * 2,
        out_specs=pl.BlockSpec((None, t_new, width), seq),
        scratch_shapes=[pltpu.VMEM((n_heads, rows, dv), F32),
                        pltpu.VMEM((n_heads, rows, LANES), F32),
                        pltpu.VMEM((n_heads, rows, LANES), F32),
                        pltpu.VMEM((n_heads, rows, dv), F32),
                        pltpu.VMEM((n_heads, page, dv), F32),
                        pltpu.VMEM((n_heads, page, dv), F32)])
    return pl.pallas_call(
        functools.partial(_decode_kernel, n_pp, n_heads),
        grid_spec=grid_spec,
        out_shape=jax.ShapeDtypeStruct((b, t_new, width), BF16),
        compiler_params=pltpu.CompilerParams(
            dimension_semantics=("parallel", "arbitrary"), vmem_limit_bytes=VMEM_LIMIT),
        name="decode",
    )(page_table, q, k_new, v_new, bias, lam_params, lam_init, out_gain,
      *([cache_k] * n_pp), *([cache_v] * n_pp))


def _pad_lanes(a, width=LANES):
    return jnp.pad(a, [(0, 0)] * (a.ndim - 1) + [(0, width - a.shape[-1])])


def kernel(x_prompt, x_sample, cache_k, cache_v, state_gdn, state_conv, page_table, ffn1_norm, ffn1_w_gate_up, ffn1_w_down, mix_norm, w_in, conv_w, gdn_a_log, gdn_dt_bias, gdn_out_norm, diff_q_norm, diff_k_norm, diff_lambda, diff_out_norm, rel_bias, w_out, ffn2_norm, ffn2_w_gate_up, ffn2_w_down):
    depth, d_model, _ = w_in.shape
    n_gdn = gdn_a_log.shape[1]
    n_diff = rel_bias.shape[1]
    gdn_w = n_gdn * HEAD_DIM
    diff_w = n_diff * 2 * HEAD_DIM
    conv_dim = conv_w.shape[2]
    assert conv_dim == 3 * gdn_w and w_out.shape[1] == gdn_w + diff_w
    page = cache_k.shape[2]
    assert page >= MAX_DISTANCE
    past_len = page_table.shape[1] * page

    o = 0
    cols = {}
    for name, wd in (("qkv", conv_dim), ("gate", gdn_w), ("b", n_gdn), ("a", n_gdn),
                     ("dq", diff_w), ("dk", diff_w), ("dv", diff_w)):
        cols[name] = w_in[:, :, o:o + wd]
        o += wd
    w_in_r = jnp.concatenate([cols["qkv"], cols["gate"], cols["dq"], cols["dk"], cols["dv"],
                              _pad_lanes(cols["b"]), _pad_lanes(cols["a"])], axis=-1).astype(BF16)
    widths = (conv_dim, gdn_w, diff_w, 2 * LANES)
    bf = lambda a: a.astype(BF16)
    w1_gu, w1_d, w2_gu, w2_d, w_out_bf = bf(ffn1_w_gate_up), bf(ffn1_w_down), bf(ffn2_w_gate_up), bf(ffn2_w_down), bf(w_out)
    row3 = lambda a: a.reshape(depth, 1, -1)
    n1, n2, nm = row3(ffn1_norm), row3(ffn2_norm), row3(mix_norm)
    q_gain = row3(jnp.tile(diff_q_norm, (1, diff_w // HEAD_DIM)))
    k_gain = row3(jnp.tile(diff_k_norm, (1, diff_w // HEAD_DIM)))
    a_log = row3(_pad_lanes(gdn_a_log))
    dt_bias = row3(_pad_lanes(gdn_dt_bias))
    gdn_gain = row3(jnp.tile(gdn_out_norm, (1, n_gdn)))
    diff_gain = row3(diff_out_norm)
    lam_init = jnp.asarray([0.8 - 0.6 * math.exp(-0.3 * l) for l in range(depth)], F32)
    lam_init = jnp.broadcast_to(lam_init[:, None, None], (depth, 1, LANES))
    cache_k2 = cache_k.reshape(cache_k.shape[:2] + (page * n_diff, 2 * HEAD_DIM))
    cache_v2 = cache_v.reshape(cache_v.shape[:2] + (page * n_diff, 2 * HEAD_DIM))

    def bias_prompt(l):
        t = _attn_tile(l)
        sub = min(t, MAX_DISTANCE)
        assert t % sub == 0 and (l == t or sub == MAX_DISTANCE)
        i = jnp.arange(sub, dtype=jnp.int32)
        rel = i[:, None] - i[None, :]
        return jnp.stack([_bias_tile(rel_bias, rel), _bias_tile(rel_bias, rel + sub)], axis=1)

    def bias_sample(t_new):
        tok = jnp.arange(t_new, dtype=jnp.int32)
        lane = jnp.arange(page, dtype=jnp.int32)
        rel_last = (page + tok)[:, None] - lane[None, :]
        rel_new = jnp.where(lane[None, :] < t_new, tok[:, None] - lane[None, :], -1)
        tiles = jnp.stack([_bias_tile(rel_bias, rel_last), _bias_tile(rel_bias, rel_new)], axis=0)
        tiles = jnp.broadcast_to(tiles[:, :, None], (2, n_diff, 2, t_new, page))
        return tiles.reshape(2, n_diff * 2 * t_new, page)

    class Group:
        def __init__(self, x, paged):
            self.b, self.l, _ = x.shape
            self.m = self.b * self.l
            self.paged = paged
            self.x = x.reshape(self.m, d_model)
            self.k_all = jnp.zeros((depth, self.m * n_diff, 2 * HEAD_DIM), F32)
            self.v_all = jnp.zeros((depth, self.m * n_diff, 2 * HEAD_DIM), F32)
            self.states, self.convs = [], []
            self.bias = bias_sample(self.l) if paged else bias_prompt(self.l)

        def r3(self, a):
            return a.reshape(self.b, self.l, a.shape[-1])

        def before_attention(self, layer):
            b = self.b
            self.x = _ffn(self.x, n1, w1_gu, w1_d, layer)
            qkv, gate, ba, self.q, self.k_all, self.v_all, self.k_bf, self.v_bf = _inproj(
                self.x, nm, w_in_r, q_gain, k_gain, widths, layer, self.k_all, self.v_all)
            if self.paged:
                conv_buf, s0, bb = state_conv[layer], state_gdn[layer], 4 if b % 4 == 0 else 1
            else:
                conv_buf = jnp.zeros((b, conv_w.shape[1] - 1, conv_dim), F32)
                s0 = jnp.zeros((b, n_gdn, HEAD_DIM, HEAD_DIM), F32)
                bb = b
            self.o_gdn, s_new, conv_new = _gdn(self.r3(qkv), self.r3(gate), self.r3(ba), conv_buf, s0, conv_w,
                                               a_log, dt_bias, gdn_gain, layer, bb)
            self.states.append(s_new)
            self.convs.append(conv_new)

        def after_attention(self, layer, o_diff):
            self.x = _ffn(self.x, n2, w2_gu, w2_d, layer,
                          mix=(self.o_gdn.reshape(self.m, gdn_w), o_diff.reshape(self.m, diff_w), w_out_bf))

        def outputs(self):
            kv_shape = (depth, self.b, self.l, n_diff, 2 * HEAD_DIM)
            return (self.x.reshape(self.b, self.l, d_model), self.k_all.reshape(kv_shape),
                    self.v_all.reshape(kv_shape), jnp.stack(self.states), jnp.stack(self.convs))

    prompt, sample = Group(x_prompt, False), Group(x_sample, True)
    n_pages = page_table.shape[1]
    attn_steps = prompt.b * n_diff * (prompt.l // _attn_tile(prompt.l))
    n_pp_fused = _decode_pages_per_attn_step(attn_steps, sample.b, n_pages)
    for layer in range(depth):
        sample.before_attention(layer)
        prompt.before_attention(layer)
        dec = (sample.r3(sample.q), sample.k_all, sample.v_all, cache_k2, cache_v2, page_table, sample.bias)
        attn_args = (prompt.r3(prompt.q), prompt.r3(prompt.k_bf), prompt.r3(prompt.v_bf), prompt.bias,
                     diff_lambda, lam_init, diff_gain, layer)
        if n_pp_fused is not None:
            o_p, o_s = _attn(*attn_args, dec=dec + (n_pp_fused,))
        else:
            o_p = _attn(*attn_args)
            o_s = _decode(*dec, diff_lambda, lam_init, diff_gain, layer,
                          next(n for n in (16, 8, 4, 2, 1) if n_pages % n == 0))
        prompt.after_attention(layer, o_p)
        sample.after_attention(layer, o_s)

    y_p, k_p, v_p, s_p, c_p = prompt.outputs()
    y_s, k_s, v_s, s_s, c_s = sample.outputs()
    return (y_p, y_s, k_p, v_p, s_p, c_p, k_s, v_s, s_s, c_s)
```

```python
import functools
import math

import jax
import jax.numpy as jnp
from jax import lax
from jax.experimental import pallas as pl
from jax.experimental.pallas import tpu as pltpu

F32 = jnp.float32
BF16 = jnp.bfloat16

HEAD_DIM = 64
GDN_CHUNK = 64
MAX_DISTANCE = 128
RMS_EPS = 1e-6
L2_EPS = 1e-6
LANES = 128
MXU_DIM = 256
VMEM_LIMIT = 52 * 1024 * 1024
NEG_BIG = -1e30
LOG2E = 1.4426950408889634


def _dot(a, b):
    return jnp.dot(a, b, preferred_element_type=F32)


def _dot_nt(a, b):
    return lax.dot_general(a, b, (((1,), (1,)), ((), ())), preferred_element_type=F32)


def _dot_tn(a, b):
    return lax.dot_general(a, b, (((0,), (0,)), ((), ())), preferred_element_type=F32)


def _split(x, pieces):
    out = []
    for _ in range(pieces - 1):
        hi = x.astype(BF16)
        out.append(hi)
        x = x - hi.astype(F32)
    out.append(x.astype(BF16))
    return out


def _dot_exact_lhs(a_bf, b, pieces=3):
    return sum(_dot(a_bf, p) for p in _split(b, pieces))


def _group_sumsq(x, gmat):
    return _dot((x * x).astype(BF16), gmat)


def _group_matrix(width):
    r = lax.broadcasted_iota(jnp.int32, (width, width), 0) // HEAD_DIM
    c = lax.broadcasted_iota(jnp.int32, (width, width), 1) // HEAD_DIM
    return jnp.where(r == c, 1.0, 0.0).astype(BF16)


def _ffn_kernel(has_mix, cf, *refs):
    if has_mix:
        x_ref, og_ref, od_ref, wo_ref, nw_ref, wgu_ref, wd_ref, o_ref, h_s, a_s = refs
    else:
        x_ref, nw_ref, wgu_ref, wd_ref, o_ref, h_s, a_s = refs
    d_ff = wd_ref.shape[0]
    x = x_ref[...]
    if has_mix:
        half = og_ref.shape[-1]
        x = x + _dot(og_ref[...], wo_ref[0:half, :]) + _dot(od_ref[...], wo_ref[half:, :])
    ms = jnp.mean(x * x, axis=-1, keepdims=True)
    h_s[...] = (x * lax.rsqrt(ms + RMS_EPS) * nw_ref[...]).astype(BF16)
    for c0 in range(0, d_ff, cf):
        h = h_s[...]
        g = _dot(h, wgu_ref[:, c0:c0 + cf])
        u = _dot(h, wgu_ref[:, d_ff + c0:d_ff + c0 + cf])
        a_s[:, c0:c0 + cf] = ((g * jax.nn.sigmoid(g)) * u).astype(BF16)
    o_ref[...] = x + 0.5 * _dot(a_s[...], wd_ref[...])


def _ffn_tiles(m, d_ff):
    tm = 512 if m % 512 == 0 else m
    cf = 2 * LANES if d_ff % (2 * LANES) == 0 else d_ff
    return tm, cf


def _ffn(x, norm_w, w_gu, w_down, layer, mix=None):
    m, d = x.shape
    d_ff = w_down.shape[1]
    tm, cf = _ffn_tiles(m, d_ff)
    row = lambda i: (i, 0)
    lay3 = lambda i: (layer, 0, 0)
    once = pl.Buffered(1)
    in_specs = [pl.BlockSpec((tm, d), row)]
    args = [x]
    if mix is not None:
        og, od, w_out = mix
        in_specs += [pl.BlockSpec((tm, og.shape[1]), row), pl.BlockSpec((tm, od.shape[1]), row),
                     pl.BlockSpec((None,) + w_out.shape[1:], lay3, pipeline_mode=once)]
        args += [og, od, w_out]
    in_specs += [
        pl.BlockSpec((None, 1, d), lay3),
        pl.BlockSpec((None, d, 2 * d_ff), lay3, pipeline_mode=once),
        pl.BlockSpec((None, d_ff, d), lay3, pipeline_mode=once),
    ]
    args += [norm_w, w_gu, w_down]
    return pl.pallas_call(
        functools.partial(_ffn_kernel, mix is not None, cf),
        grid=(m // tm,),
        in_specs=in_specs,
        out_specs=pl.BlockSpec((tm, d), row),
        out_shape=jax.ShapeDtypeStruct((m, d), F32),
        scratch_shapes=[pltpu.VMEM((tm, d), BF16), pltpu.VMEM((tm, d_ff), BF16)],
        compiler_params=pltpu.CompilerParams(
            dimension_semantics=("parallel",), vmem_limit_bytes=VMEM_LIMIT),
        name="ffn_mix" if mix is not None else "ffn",
    )(*args)


def _inproj_kernel(widths, x_ref, nw_ref, w_ref, qg_ref, kg_ref, k_prev_ref, v_prev_ref,
                   qkv_ref, gate_ref, ba_ref, q_ref, k_ref, v_ref, kb_ref, vb_ref):
    del k_prev_ref, v_prev_ref
    c_qkv, c_gate, c_d, c_ba = widths
    x = x_ref[...]
    ms = jnp.mean(x * x, axis=-1, keepdims=True)
    h = (x * lax.rsqrt(ms + RMS_EPS) * nw_ref[...]).astype(BF16)
    o = 0
    qkv_ref[...] = _dot(h, w_ref[:, o:o + c_qkv]); o += c_qkv
    gate_ref[...] = _dot(h, w_ref[:, o:o + c_gate]); o += c_gate
    dq = _dot(h, w_ref[:, o:o + c_d]); o += c_d
    dk = _dot(h, w_ref[:, o:o + c_d]); o += c_d
    dv = _dot(h, w_ref[:, o:o + c_d]); o += c_d
    ba_ref[...] = _dot(h, w_ref[:, o:o + c_ba])
    gmat = _group_matrix(c_d)
    inv_hd = 1.0 / HEAD_DIM
    qn = dq * lax.rsqrt(_group_sumsq(dq, gmat) * inv_hd + RMS_EPS) * qg_ref[...]
    kn = dk * lax.rsqrt(_group_sumsq(dk, gmat) * inv_hd + RMS_EPS) * kg_ref[...]
    q_ref[...] = qn * (HEAD_DIM ** -0.5 * LOG2E)
    tm = kn.shape[0]
    dv_w = 2 * HEAD_DIM
    n_heads = c_d // dv_w
    for h in range(n_heads):
        k_ref[pl.ds(h, tm, stride=n_heads), :] = kn[:, h * dv_w:(h + 1) * dv_w]
        v_ref[pl.ds(h, tm, stride=n_heads), :] = dv[:, h * dv_w:(h + 1) * dv_w]
    kb_ref[...] = kn.astype(BF16)
    vb_ref[...] = dv.astype(BF16)


def _inproj(x, norm_w, w_in, q_gain, k_gain, widths, layer, k_all, v_all):
    m, d = x.shape
    c_qkv, c_gate, c_d, c_ba = widths
    dv_w = 2 * HEAD_DIM
    n_heads = c_d // dv_w
    tm = 512 if m % 512 == 0 else m
    row = lambda i: (i, 0)
    lay3 = lambda i: (layer, 0, 0)
    slab = pl.BlockSpec((None, tm * n_heads, dv_w), lambda i: (layer, i, 0))
    outs = [(c_qkv, F32), (c_gate, F32), (c_ba, F32), (c_d, F32), None, None, (c_d, BF16), (c_d, BF16)]
    return pl.pallas_call(
        functools.partial(_inproj_kernel, widths),
        grid=(m // tm,),
        in_specs=[pl.BlockSpec((tm, d), row),
                  pl.BlockSpec((None, 1, d), lay3),
                  pl.BlockSpec((None,) + w_in.shape[1:], lay3),
                  pl.BlockSpec((None, 1, c_d), lay3),
                  pl.BlockSpec((None, 1, c_d), lay3),
                  pl.BlockSpec(memory_space=pl.ANY),
                  pl.BlockSpec(memory_space=pl.ANY)],
        out_specs=[slab if o is None else pl.BlockSpec((tm, o[0]), row) for o in outs],
        out_shape=[jax.ShapeDtypeStruct(k_all.shape, F32) if o is None else jax.ShapeDtypeStruct((m, o[0]), o[1])
                   for o in outs],
        input_output_aliases={5: 4, 6: 5},
        compiler_params=pltpu.CompilerParams(
            dimension_semantics=("parallel",), vmem_limit_bytes=VMEM_LIMIT),
        name="inproj",
    )(x, norm_w, w_in, q_gain, k_gain, k_all, v_all)


def _block_mask(rows, cols, row_group, col_group):
    r = lax.broadcasted_iota(jnp.int32, (rows, cols), 0) // row_group
    c = lax.broadcasted_iota(jnp.int32, (rows, cols), 1) // col_group
    return r == c


def _block_diag(x, mask):
    reps = mask.shape[0] // x.shape[0]
    return jnp.where(mask, jnp.concatenate([x] * reps, axis=0), 0.0).astype(BF16)


def _dot_exact_rhs(a, b_bf, pieces=3):
    return sum(_dot(p, b_bf) for p in _split(a, pieces))


def _gdn_kernel(n_heads, chunk, gh, nc, qkv_ref, gate_ref, ba_ref, cbuf_ref, s0_ref, cw_ref, alog_ref, dtb_ref,
                gain_ref, o_ref, snew_ref, cnew_ref, xp_s, s_s):
    bb = qkv_ref.shape[0]
    c = chunk
    tb = nc * c
    d = HEAD_DIM
    width = n_heads * d
    n_grp = n_heads // gh
    tw = gh * c
    dw = gh * d
    n_tail = cbuf_ref.shape[1]
    n_taps = n_tail + 1
    top = 8 - n_tail
    t = pl.program_id(1)
    n_t = pl.num_programs(1)

    @pl.when(t == 0)
    def _():
        xp_s[:, top:8, :] = cbuf_ref[...]
        s_s[...] = jnp.zeros_like(s_s)
        for b in range(bb):
            for h in range(n_heads):
                o = (h % gh) * d
                s_s[b, h // gh, o:o + d, o:o + d] = s0_ref[b, h]

    xp_s[:, 8:8 + tb, :] = qkv_ref[...]

    cw = cw_ref[...]
    gmat = _group_matrix(width)
    ri = lax.broadcasted_iota(jnp.int32, (c, c), 0)
    ci = lax.broadcasted_iota(jnp.int32, (c, c), 1)
    tril_bf = jnp.where(ri >= ci, 1.0, 0.0).astype(BF16)
    rb = lax.broadcasted_iota(jnp.int32, (tb, tb), 0)
    cb = lax.broadcasted_iota(jnp.int32, (tb, tb), 1)
    tril_chunks = jnp.where((rb >= cb) & (rb // c == cb // c), 1.0, 0.0).astype(BF16)
    row_t = lax.broadcasted_iota(jnp.int32, (c, tw), 0)
    col_t = lax.broadcasted_iota(jnp.int32, (c, tw), 1) % c
    causal = row_t >= col_t
    strict = row_t > col_t
    eye = jnp.where(row_t == col_t, 1.0, 0.0).astype(F32)
    bd_tt = _block_mask(tw, tw, c, c)
    bd_td = _block_mask(tw, dw, c, d)
    bd_dd = _block_mask(dw, dw, d, d)
    gmat_g = jnp.where(bd_dd, 1.0, 0.0).astype(BF16)
    expand_d = jnp.where(_block_mask(LANES, width, 1, d), 1.0, 0.0).astype(BF16)
    expand_t = jnp.where(_block_mask(LANES, n_heads * c, 1, c), 1.0, 0.0).astype(BF16)
    neg_a = -jnp.exp(alog_ref[...])
    dtb = dtb_ref[...]
    gain = gain_ref[...]

    chains = []
    for b in range(bb):
        y = xp_s[b, top:top + tb, :] * cw[0:1, :]
        for i in range(1, n_taps):
            y = y + xp_s[b, top + i:top + i + tb, :] * cw[i:i + 1, :]
        qkv = y * jax.nn.sigmoid(y)
        q_all = qkv[:, 0:width]
        k_all = qkv[:, width:2 * width]
        v_all = qkv[:, 2 * width:3 * width]
        q_all = q_all * lax.rsqrt(_group_sumsq(q_all, gmat) + L2_EPS) * (HEAD_DIM ** -0.5)
        k_all = k_all * lax.rsqrt(_group_sumsq(k_all, gmat) + L2_EPS)
        ba = ba_ref[b]
        beta_all = jax.nn.sigmoid(ba[:, 0:LANES])
        z = ba[:, LANES:2 * LANES] + dtb
        softplus = jnp.maximum(z, 0.0) + jnp.log1p(jnp.exp(-jnp.abs(z)))
        g_all = neg_a * softplus
        gc_all = _dot_exact_lhs(tril_chunks, g_all, 2)
        gate_all = gate_ref[b]
        gate_all = gate_all * jax.nn.sigmoid(gate_all)
        beta_e = _dot_exact_rhs(beta_all, expand_d, 1)
        gc_e = _dot_exact_rhs(gc_all, expand_d, 2)
        g_t = _dot_exact_rhs(g_all, expand_t, 2)
        egc_e = jnp.exp(gc_e)

        for cc in range(nc):
            rows = slice(cc * c, (cc + 1) * c)
            glast = gc_e[(cc + 1) * c - 1:(cc + 1) * c, :]
            kend_e = jnp.exp(glast - gc_e[rows])
            sdec_e = jnp.exp(glast)
            for gi in range(n_grp):
                sl = slice(gi * dw, (gi + 1) * dw)
                k = k_all[rows, sl]
                beta = beta_e[rows, sl]
                chains.append(dict(b=b, cc=cc, gi=gi, rows=rows, sl=sl, q=q_all[rows, sl], k=k, kb=k * beta,
                                   vb=v_all[rows, sl] * beta, egc=egc_e[rows, sl],
                                   g_t=g_t[rows, gi * tw:(gi + 1) * tw], kend=kend_e[:, sl], sdec=sdec_e[:, sl],
                                   gate=gate_all[rows, sl]))

    for ch in chains:
        dm = _dot_exact_lhs(tril_bf, jnp.where(strict, ch["g_t"], 0.0), 2)
        ch["decay"] = jnp.where(causal, jnp.exp(dm), 0.0)
        ch["k_bd"] = _block_diag(ch["k"], bd_td)
    for ch in chains:
        ch["a"] = jnp.where(strict, _dot_nt(ch["kb"].astype(BF16), ch["k_bd"]) * ch["decay"], 0.0)
        ch["tinv"] = eye - jnp.where(row_t // 2 == col_t // 2, ch["a"], 0.0)
    size = 2
    while size < c:
        lower_left = ((row_t // (2 * size) == col_t // (2 * size))
                      & (row_t % (2 * size) >= size) & (col_t % (2 * size) < size))
        for ch in chains:
            a21 = _block_diag(jnp.where(lower_left, ch["a"], 0.0), bd_tt)
            ch["t2_a21"] = _dot(ch["tinv"].astype(BF16), a21)
        for ch in chains:
            ch["tinv"] = ch["tinv"] - _dot(ch["t2_a21"].astype(BF16), _block_diag(ch["tinv"], bd_tt))
        size *= 2
    for ch in chains:
        tinv_bf = ch["tinv"].astype(BF16)
        ch["u"] = _dot(tinv_bf, _block_diag(ch["vb"], bd_td))
        ch["w"] = _dot(tinv_bf, _block_diag(ch["kb"] * ch["egc"], bd_td))
        ch["qk"] = jnp.where(causal, _dot_nt(ch["q"].astype(BF16), ch["k_bd"]) * ch["decay"], 0.0)
    for cc in range(nc):
        now = [ch for ch in chains if ch["cc"] == cc]
        for ch in now:
            ch["s"] = s_s[ch["b"], ch["gi"]]
            lhs = jnp.concatenate([ch["w"], ch["q"] * ch["egc"]], axis=0).astype(BF16)
            ch["ws"] = _dot(lhs, ch["s"].astype(BF16))
        for ch in now:
            v_new = ch["u"] - ch["ws"][0:c]
            ch["o"] = ch["ws"][c:2 * c] + _dot(ch["qk"].astype(BF16), _block_diag(v_new, bd_td))
            k_end = ch["k"] * ch["kend"]
            cross = _dot_tn(k_end.astype(BF16), v_new.astype(BF16))
            s_s[ch["b"], ch["gi"]] = ch["s"] * ch["sdec"] + jnp.where(bd_dd, cross, 0.0)
    for ch in chains:
        o = ch["o"]
        ms = _group_sumsq(o, gmat_g) * (1.0 / d)
        on = o * lax.rsqrt(ms + RMS_EPS) * gain[:, ch["sl"]]
        o_ref[ch["b"], ch["rows"], ch["sl"]] = (on * ch["gate"]).astype(o_ref.dtype)

    tail = xp_s[:, 8 + tb - n_tail:8 + tb, :]
    xp_s[:, top:8, :] = tail

    @pl.when(t == n_t - 1)
    def _():
        cnew_ref[...] = tail
        for b in range(bb):
            for h in range(n_heads):
                o = (h % gh) * d
                snew_ref[b, h] = s_s[b, h // gh, o:o + d, o:o + d]


def _gdn(qkv, gate, ba, conv_buf, s0, conv_w, a_log, dt_bias, out_gain, layer, batch_block):
    b, l, w3 = qkv.shape
    width = w3 // 3
    n_heads = width // HEAD_DIM
    chunk = min(GDN_CHUNK, l)
    assert l % chunk == 0 and chunk % 8 == 0 and chunk & (chunk - 1) == 0 and b % batch_block == 0
    n_tail = conv_buf.shape[1]
    assert n_tail <= min(8, chunk)
    bb = batch_block
    gh = min(n_heads, max(1, MXU_DIM // chunk))
    assert n_heads % gh == 0
    nc = next(n for n in (4, 2, 1) if (l // chunk) % n == 0)
    tb = nc * chunk
    blk = lambda i, t: (i, t, 0)
    fix3 = lambda i, t: (i, 0, 0)
    lay3 = lambda i, t: (layer, 0, 0)
    return pl.pallas_call(
        functools.partial(_gdn_kernel, n_heads, chunk, gh, nc),
        grid=(b // bb, l // tb),
        in_specs=[pl.BlockSpec((bb, tb, w3), blk),
                  pl.BlockSpec((bb, tb, width), blk),
                  pl.BlockSpec((bb, tb, ba.shape[2]), blk),
                  pl.BlockSpec((bb, n_tail, w3), fix3),
                  pl.BlockSpec((bb, n_heads, HEAD_DIM, HEAD_DIM), lambda i, t: (i, 0, 0, 0)),
                  pl.BlockSpec((None,) + conv_w.shape[1:], lay3),
                  pl.BlockSpec((None, 1, LANES), lay3),
                  pl.BlockSpec((None, 1, LANES), lay3),
                  pl.BlockSpec((None, 1, width), lay3)],
        out_specs=[pl.BlockSpec((bb, tb, width), blk),
                   pl.BlockSpec((bb, n_heads, HEAD_DIM, HEAD_DIM), lambda i, t: (i, 0, 0, 0)),
                   pl.BlockSpec((bb, n_tail, w3), fix3)],
        out_shape=[jax.ShapeDtypeStruct((b, l, width), BF16),
                   jax.ShapeDtypeStruct(s0.shape, F32),
                   jax.ShapeDtypeStruct(conv_buf.shape, F32)],
        scratch_shapes=[pltpu.VMEM((bb, 8 + tb, w3), F32),
                        pltpu.VMEM((bb, n_heads // gh, gh * HEAD_DIM, gh * HEAD_DIM), F32)],
        compiler_params=pltpu.CompilerParams(
            dimension_semantics=("parallel", "arbitrary"), vmem_limit_bytes=VMEM_LIMIT),
        name="gdn",
    )(qkv, gate, ba, conv_buf, s0, conv_w, a_log, dt_bias, out_gain)


def _rel_bucket(rel, n_buckets):
    n = jnp.maximum(rel, 0)
    max_exact = n_buckets // 2
    nf = jnp.maximum(n, 1).astype(F32)
    large = max_exact + (jnp.log(nf / max_exact) / math.log(MAX_DISTANCE / max_exact)
                         * (n_buckets - max_exact)).astype(jnp.int32)
    large = jnp.minimum(large, n_buckets - 1)
    return jnp.where(n < max_exact, n, large)


def _bias_tile(rel_bias, rel):
    n_buckets, n_heads = rel_bias.shape
    tab = rel_bias.astype(F32) - rel_bias[n_buckets - 1].astype(F32)[None, :]
    bucket = _rel_bucket(rel, n_buckets)[None]
    bias = jnp.zeros((n_heads,) + rel.shape, F32)
    for n in range(n_buckets):
        bias = jnp.where(bucket == n, tab[n].reshape((n_heads,) + (1,) * rel.ndim), bias)
    return jnp.where((rel >= 0)[None], bias * LOG2E, -jnp.inf)


def _lambda(lp_ref, li_ref):
    lp = lp_ref[...]
    e1 = jnp.exp(jnp.sum(lp[0:1, :] * lp[1:2, :], axis=-1, keepdims=True))
    e2 = jnp.exp(jnp.sum(lp[2:3, :] * lp[3:4, :], axis=-1, keepdims=True))
    return e1 - e2 + li_ref[:, 0:1]


def _attn_kernel(tk, sub, dec, *refs):
    if dec is None:
        q_ref, k_ref, v_ref, bias_ref, lp_ref, li_ref, gain_ref, o_ref, q_s, m_s, acc_s, vx_s, s0_s, s1_s = refs
    else:
        n_pp, n_dec_heads, n_groups = dec
        q_ref, k_ref, v_ref, bias_ref, lp_ref, li_ref, gain_ref, dq_ref, kn_ref, vn_ref, dbias_ref = refs[1:12]
        k_pages = refs[12:12 + n_pp]
        v_pages = refs[12 + n_pp:12 + 2 * n_pp]
        o_ref, od_ref, q_s, m_s, acc_s, vx_s, s0_s, s1_s = refs[12 + 2 * n_pp:20 + 2 * n_pp]
        step_id = ((pl.program_id(0) * pl.num_programs(1) + pl.program_id(1)) * pl.num_programs(2)
                   + pl.program_id(2))
        _decode_step(n_dec_heads, step_id % n_groups, n_groups, dq_ref, kn_ref, vn_ref, dbias_ref, lp_ref, li_ref,
                     gain_ref, k_pages, v_pages, od_ref, *refs[20 + 2 * n_pp:])
    tq = q_ref.shape[0]
    dv = 2 * HEAD_DIM
    qi = pl.program_id(2)

    @pl.when(qi == 0)
    def _():
        vx_s[:, 0:dv] = v_ref[...]
        vx_s[:, dv:2 * dv] = jnp.ones((vx_s.shape[0], dv), BF16)

    q = q_ref[...]
    lane = lax.broadcasted_iota(jnp.int32, q.shape, 1)
    q_s[0:tq, :] = jnp.where(lane < HEAD_DIM, q, 0.0).astype(BF16)
    q_s[tq:2 * tq, :] = jnp.where(lane >= HEAD_DIM, q, 0.0).astype(BF16)
    m_s[...] = jnp.full_like(m_s, NEG_BIG)
    acc_s[...] = jnp.zeros_like(acc_s)

    def scores(buf, j):
        ks = pl.multiple_of(j * tk, tk)
        buf[...] = _dot_nt(q_s[...], k_ref[pl.ds(ks, tk), :])

    def update(buf, j):
        ks = pl.multiple_of(j * tk, tk)
        m_prev = m_s[...]
        m_new = jnp.maximum(m_prev, jnp.max(buf[...], axis=-1, keepdims=True))
        p = jnp.exp2(buf[...] - jnp.concatenate([m_new] * (tk // LANES), axis=1)).astype(BF16)
        alpha = jnp.exp2(m_prev - m_new)
        pv = _dot(p, vx_s[pl.ds(ks, tk), :])
        acc_s[...] = jnp.concatenate([alpha] * (2 * dv // LANES), axis=1) * acc_s[...] + pv
        m_s[...] = m_new

    def below_diagonal_bias(buf):
        for m in range(2):
            buf[m * tq:m * tq + sub, tk - sub:tk] += bias_ref[1]

    def diagonal_bias(buf):
        for m in range(2):
            for r in range(tq // sub):
                rows = slice(m * tq + r * sub, m * tq + (r + 1) * sub)
                if r >= 1:
                    buf[rows, (r - 1) * sub:r * sub] += bias_ref[1]
                buf[rows, r * sub:(r + 1) * sub] += bias_ref[0]
                if (r + 1) * sub < tk:
                    buf[rows, (r + 1) * sub:tk] = jnp.full((sub, tk - (r + 1) * sub), -jnp.inf, F32)

    n_pairs = jnp.maximum(qi - 1, 0) // 2
    scores(s0_s, 0)

    def pair(i, carry):
        scores(s1_s, 2 * i + 1)
        update(s0_s, 2 * i)
        scores(s0_s, 2 * i + 2)
        update(s1_s, 2 * i + 1)
        return carry

    lax.fori_loop(0, n_pairs, pair, 0)

    @pl.when(qi == 0)
    def _():
        diagonal_bias(s0_s)
        update(s0_s, 0)

    @pl.when(qi % 2 == 1)
    def _():
        scores(s1_s, qi)
        below_diagonal_bias(s0_s)
        diagonal_bias(s1_s)
        update(s0_s, qi - 1)
        update(s1_s, qi)

    @pl.when((qi % 2 == 0) & (qi >= 2))
    def _():
        scores(s1_s, qi - 1)
        below_diagonal_bias(s1_s)
        update(s0_s, qi - 2)
        scores(s0_s, qi)
        diagonal_bias(s0_s)
        update(s1_s, qi - 1)
        update(s0_s, qi)

    lam = _lambda(lp_ref, li_ref)
    acc = acc_s[...]
    on = acc[:, 0:dv] * (1.0 / acc[:, dv:2 * dv])
    o = on[0:tq] - lam * on[tq:2 * tq]
    ms = jnp.mean(o * o, axis=-1, keepdims=True)
    o_ref[...] = (o * lax.rsqrt(ms + RMS_EPS) * gain_ref[...] * (1.0 - li_ref[:, 0:1])).astype(o_ref.dtype)


def _attn_tile(l):
    return 512 if l % 512 == 0 else l


def _decode_pages_per_attn_step(attn_steps, n_seq, n_pages):
    total = n_seq * n_pages
    if total % attn_steps:
        return None
    n_pp = total // attn_steps
    return n_pp if (0 < n_pp <= 16 and n_pages % n_pp == 0) else None


def _attn(q, k_bf, v_bf, bias, lam_params, lam_init, out_gain, layer, dec=None):
    b, l, width = q.shape
    dv = 2 * HEAD_DIM
    assert dv == LANES
    n_heads = width // dv
    tq = tk = _attn_tile(l)
    nq = l // tq
    sub = bias.shape[-1]
    lay3 = lambda bi, h, qi, *_: (layer, 0, 0)
    in_specs = [pl.BlockSpec((None, tq, dv), lambda bi, h, qi, *_: (bi, qi, h)),
                pl.BlockSpec((None, l, dv), lambda bi, h, qi, *_: (bi, 0, h)),
                pl.BlockSpec((None, l, dv), lambda bi, h, qi, *_: (bi, 0, h)),
                pl.BlockSpec((None, 2, sub, sub), lambda bi, h, qi, *_: (h, 0, 0, 0)),
                pl.BlockSpec((None,) + lam_params.shape[1:], lay3),
                pl.BlockSpec((None, 1, LANES), lay3),
                pl.BlockSpec((None, 1, dv), lay3)]
    out_specs = pl.BlockSpec((None, tq, dv), lambda bi, h, qi, *_: (bi, qi, h))
    out_shape = jax.ShapeDtypeStruct((b, l, width), BF16)
    scratch = [pltpu.VMEM((2 * tq, dv), BF16),
               pltpu.VMEM((2 * tq, LANES), F32),
               pltpu.VMEM((2 * tq, 2 * dv), F32),
               pltpu.VMEM((l, 2 * dv), BF16),
               pltpu.VMEM((2 * tq, tk), F32),
               pltpu.VMEM((2 * tq, tk), F32)]
    args = [q, k_bf, v_bf, bias, lam_params, lam_init, out_gain]
    if dec is None:
        return pl.pallas_call(
            functools.partial(_attn_kernel, tk, sub, None),
            grid=(b, n_heads, nq), in_specs=in_specs, out_specs=out_specs, out_shape=out_shape,
            scratch_shapes=scratch,
            compiler_params=pltpu.CompilerParams(
                dimension_semantics=("parallel", "parallel", "arbitrary"), vmem_limit_bytes=VMEM_LIMIT),
            name="attn",
        )(*args)

    dq, k_new, v_new, cache_k, cache_v, page_table, dbias, n_pp = dec
    n_seq, t_new, dwidth = dq.shape
    dh = dwidth // dv
    page = cache_k.shape[2] // dh
    n_groups = page_table.shape[1] // n_pp
    assert b * n_heads * nq == n_seq * n_groups and t_new <= page
    rows = 2 * t_new

    def step_id(bi, h, qi):
        return (bi * n_heads + h) * nq + qi

    seq3 = lambda bi, h, qi, pt: (step_id(bi, h, qi) // n_groups, 0, 0)
    new3 = lambda bi, h, qi, pt: (layer, step_id(bi, h, qi) // n_groups, 0)

    def page_spec(i):
        def index(bi, h, qi, pt):
            sid = step_id(bi, h, qi)
            return (layer, pt[sid // n_groups, (sid % n_groups) * n_pp + i], 0, 0)
        return pl.BlockSpec((None, None, page * dh, dv), index)

    in_specs += [pl.BlockSpec((None, t_new, dwidth), seq3),
                 pl.BlockSpec((None, t_new * dh, dv), new3),
                 pl.BlockSpec((None, t_new * dh, dv), new3),
                 pl.BlockSpec(dbias.shape, lambda bi, h, qi, pt: (0, 0, 0, 0))]
    in_specs += [page_spec(i) for i in range(n_pp)] * 2
    scratch += _decode_scratch(dh, t_new, page)
    grid_spec = pltpu.PrefetchScalarGridSpec(
        num_scalar_prefetch=1, grid=(b, n_heads, nq), in_specs=in_specs,
        out_specs=[out_specs, pl.BlockSpec((None, t_new, dwidth), seq3)], scratch_shapes=scratch)
    return pl.pallas_call(
        functools.partial(_attn_kernel, tk, sub, (n_pp, dh, n_groups)),
        grid_spec=grid_spec,
        out_shape=[out_shape, jax.ShapeDtypeStruct((n_seq, t_new, dwidth), BF16)],
        compiler_params=pltpu.CompilerParams(
            dimension_semantics=("arbitrary", "arbitrary", "arbitrary"), vmem_limit_bytes=VMEM_LIMIT),
        name="attn_decode",
    )(page_table, *args, dq, k_new, v_new, dbias, *([cache_k] * n_pp), *([cache_v] * n_pp))


def _decode_scratch(n_heads, t_new, page):
    assert n_heads % 2 == 0
    n_pairs, cols, width = n_heads // 2, 4 * t_new, 4 * HEAD_DIM
    return [pltpu.VMEM((n_pairs, cols, width), F32),
            pltpu.VMEM((n_pairs, cols, LANES), F32),
            pltpu.VMEM((n_pairs, cols, LANES), F32),
            pltpu.VMEM((n_pairs, cols, width), F32),
            pltpu.VMEM((n_pairs, page, width), F32),
            pltpu.VMEM((n_pairs, page, width), F32)]


def _decode_kernel(n_pp, n_heads, pt_ref, q_ref, kn_ref, vn_ref, bias_ref, lp_ref, li_ref, gain_ref, *rest):
    del pt_ref
    _decode_step(n_heads, pl.program_id(1), pl.num_programs(1), q_ref, kn_ref, vn_ref, bias_ref, lp_ref, li_ref,
                 gain_ref, rest[:n_pp], rest[n_pp:2 * n_pp], *rest[2 * n_pp:])


def _decode_step(n_heads, j, n_j, q_ref, kn_ref, vn_ref, bias_ref, lp_ref, li_ref, gain_ref, k_refs, v_refs,
                 o_ref, q_s, m_s, l_s, acc_s, kpad_s, vpad_s):
    n_pp = len(k_refs)
    t_new = q_ref.shape[0]
    page = k_refs[0].shape[0] // n_heads
    dv = 2 * HEAD_DIM
    rows = 2 * t_new
    n_pairs = n_heads // 2
    cols = 2 * rows

    @pl.when(j == 0)
    def _():
        m_s[...] = jnp.full_like(m_s, NEG_BIG)
        l_s[...] = jnp.zeros_like(l_s)
        acc_s[...] = jnp.zeros_like(acc_s)
        q_s[...] = jnp.zeros_like(q_s)
        kpad_s[...] = jnp.zeros_like(kpad_s)
        vpad_s[...] = jnp.zeros_like(vpad_s)
        lane = lax.broadcasted_iota(jnp.int32, (t_new, dv), 1)
        for h in range(n_heads):
            pr, side = h // 2, h % 2
            cs = slice(h * dv, (h + 1) * dv)
            ls = slice(side * dv, (side + 1) * dv)
            qh = q_ref[:, cs]
            q_s[pr, side * rows:side * rows + t_new, ls] = jnp.where(lane < HEAD_DIM, qh, 0.0)
            q_s[pr, side * rows + t_new:(side + 1) * rows, ls] = jnp.where(lane >= HEAD_DIM, qh, 0.0)
            kpad_s[pr, 0:t_new, ls] = kn_ref[pl.ds(h, t_new, stride=n_heads), :]
            vpad_s[pr, 0:t_new, ls] = vn_ref[pl.ds(h, t_new, stride=n_heads), :]

    def pair_rows(ref, pr):
        return jnp.concatenate([ref[pl.ds(2 * pr, page, stride=n_heads), :],
                                ref[pl.ds(2 * pr + 1, page, stride=n_heads), :]], axis=1)

    def pair_step(n_plain, with_tail):
        state = []
        for pr in range(n_pairs):
            pieces = []
            if n_plain:
                k_cat = jnp.concatenate([pair_rows(k_refs[i], pr) for i in range(n_plain)], axis=0)
                pieces.append((jnp.transpose(_dot_nt(k_cat, q_s[pr])), range(n_plain)))
            if with_tail:
                k_cat = jnp.concatenate([pair_rows(k_refs[n_pp - 1], pr), kpad_s[pr]], axis=0)
                bias = jnp.concatenate([bias_ref[0, pr], bias_ref[1, pr]], axis=0)
                pieces.append((jnp.transpose(_dot_nt(k_cat, q_s[pr]) + bias), None))
            state.append(pieces)
        m_new = []
        for pr in range(n_pairs):
            m_cur = state[pr][0][0].max(axis=1, keepdims=True)
            for s, _ in state[pr][1:]:
                m_cur = jnp.maximum(m_cur, s.max(axis=1, keepdims=True))
            m_new.append(jnp.maximum(m_s[pr], m_cur))
        for pr in range(n_pairs):
            alpha = jnp.exp2(m_s[pr] - m_new[pr])
            psum = jnp.zeros((cols, 1), F32)
            pv = jnp.zeros((cols, 2 * dv), F32)
            for s, plain in state[pr]:
                p = jnp.exp2(s - m_new[pr][:, 0:1])
                psum = psum + jnp.sum(p, axis=1, keepdims=True)
                if plain is not None:
                    v_cat = jnp.concatenate([pair_rows(v_refs[i], pr) for i in plain], axis=0)
                else:
                    v_cat = jnp.concatenate([pair_rows(v_refs[n_pp - 1], pr), vpad_s[pr]], axis=0)
                pv = pv + _dot(p, v_cat)
            l_s[pr] = alpha * l_s[pr] + psum
            acc_s[pr] = jnp.concatenate([alpha] * (2 * dv // LANES), axis=1) * acc_s[pr] + pv
            m_s[pr] = m_new[pr]

    @pl.when(j < n_j - 1)
    def _():
        pair_step(n_pp, False)

    @pl.when(j == n_j - 1)
    def _():
        pair_step(n_pp - 1, True)
        lam = _lambda(lp_ref, li_ref)
        scale = 1.0 - li_ref[:, 0:1]
        for h in range(n_heads):
            pr, side = h // 2, h % 2
            on = (acc_s[pr][side * rows:(side + 1) * rows, side * dv:(side + 1) * dv]
                  / l_s[pr][side * rows:(side + 1) * rows, :])
            o = on[0:t_new] - lam * on[t_new:rows]
            ms = jnp.mean(o * o, axis=-1, keepdims=True)
            o_ref[:, h * dv:(h + 1) * dv] = (o * lax.rsqrt(ms + RMS_EPS) * gain_ref[...] * scale).astype(o_ref.dtype)


def _decode(q, k_new, v_new, cache_k, cache_v, page_table, bias, lam_params, lam_init, out_gain, layer,
            pages_per_step):
    b, t_new, width = q.shape
    dv = 2 * HEAD_DIM
    n_heads = width // dv
    page = cache_k.shape[2] // n_heads
    n_pages = page_table.shape[1]
    n_pp = pages_per_step
    assert n_pages % n_pp == 0 and t_new <= page
    rows = 2 * t_new
    seq = lambda bi, j, pt: (bi, 0, 0)
    lay3 = lambda bi, j, pt: (layer, 0, 0)

    def page_spec(i):
        return pl.BlockSpec((None, None, page * n_heads, dv),
                            lambda bi, j, pt: (layer, pt[bi, j * n_pp + i], 0, 0))

    grid_spec = pltpu.PrefetchScalarGridSpec(
        num_scalar_prefetch=1,
        grid=(b, n_pages // n_pp),
        in_specs=[pl.BlockSpec((None, t_new, width), seq),
                  pl.BlockSpec((None, t_new * n_heads, dv), lambda bi, j, pt: (layer, bi, 0)),
                  pl.BlockSpec((None, t_new * n_heads, dv), lambda bi, j, pt: (layer, bi, 0)),
                  pl.BlockSpec(bias.shape, lambda bi, j, pt: (0, 0, 0, 0)),
                  pl.BlockSpec((None,) + lam_params.shape[1:], lay3),
                  pl.BlockSpec((None, 1, LANES), lay3),
                  pl.BlockSpec((None, 1, dv), lay3)]
                 + [page_spec(i) for i in range(n_pp)] * 2,
        out_specs=pl.BlockSpec((None, t_new, width), seq),
        scratch_shapes=_decode_scratch(n_heads, t_new, page))
    return pl.pallas_call(
        functools.partial(_decode_kernel, n_pp, n_heads),
        grid_spec=grid_spec,
        out_shape=jax.ShapeDtypeStruct((b, t_new, width), BF16),
        compiler_params=pltpu.CompilerParams(
            dimension_semantics=("parallel", "arbitrary"), vmem_limit_bytes=VMEM_LIMIT),
        name="decode",
    )(page_table, q, k_new, v_new, bias, lam_params, lam_init, out_gain,
      *([cache_k] * n_pp), *([cache_v] * n_pp))


def _pad_lanes(a, width=LANES):
    return jnp.pad(a, [(0, 0)] * (a.ndim - 1) + [(0, width - a.shape[-1])])


def kernel(x_prompt, x_sample, cache_k, cache_v, state_gdn, state_conv, page_table, ffn1_norm, ffn1_w_gate_up, ffn1_w_down, mix_norm, w_in, conv_w, gdn_a_log, gdn_dt_bias, gdn_out_norm, diff_q_norm, diff_k_norm, diff_lambda, diff_out_norm, rel_bias, w_out, ffn2_norm, ffn2_w_gate_up, ffn2_w_down):
    depth, d_model, _ = w_in.shape
    n_gdn = gdn_a_log.shape[1]
    n_diff = rel_bias.shape[1]
    gdn_w = n_gdn * HEAD_DIM
    diff_w = n_diff * 2 * HEAD_DIM
    conv_dim = conv_w.shape[2]
    assert conv_dim == 3 * gdn_w and w_out.shape[1] == gdn_w + diff_w
    page = cache_k.shape[2]
    assert page >= MAX_DISTANCE
    past_len = page_table.shape[1] * page

    o = 0
    cols = {}
    for name, wd in (("qkv", conv_dim), ("gate", gdn_w), ("b", n_gdn), ("a", n_gdn),
                     ("dq", diff_w), ("dk", diff_w), ("dv", diff_w)):
        cols[name] = w_in[:, :, o:o + wd]
        o += wd
    w_in_r = jnp.concatenate([cols["qkv"], cols["gate"], cols["dq"], cols["dk"], cols["dv"],
                              _pad_lanes(cols["b"]), _pad_lanes(cols["a"])], axis=-1).astype(BF16)
    widths = (conv_dim, gdn_w, diff_w, 2 * LANES)
    bf = lambda a: a.astype(BF16)
    w1_gu, w1_d, w2_gu, w2_d, w_out_bf = bf(ffn1_w_gate_up), bf(ffn1_w_down), bf(ffn2_w_gate_up), bf(ffn2_w_down), bf(w_out)
    row3 = lambda a: a.reshape(depth, 1, -1)
    n1, n2, nm = row3(ffn1_norm), row3(ffn2_norm), row3(mix_norm)
    q_gain = row3(jnp.tile(diff_q_norm, (1, diff_w // HEAD_DIM)))
    k_gain = row3(jnp.tile(diff_k_norm, (1, diff_w // HEAD_DIM)))
    a_log = row3(_pad_lanes(gdn_a_log))
    dt_bias = row3(_pad_lanes(gdn_dt_bias))
    gdn_gain = row3(jnp.tile(gdn_out_norm, (1, n_gdn)))
    diff_gain = row3(diff_out_norm)
    lam_init = jnp.asarray([0.8 - 0.6 * math.exp(-0.3 * l) for l in range(depth)], F32)
    lam_init = jnp.broadcast_to(lam_init[:, None, None], (depth, 1, LANES))
    cache_k2 = cache_k.reshape(cache_k.shape[:2] + (page * n_diff, 2 * HEAD_DIM))
    cache_v2 = cache_v.reshape(cache_v.shape[:2] + (page * n_diff, 2 * HEAD_DIM))

    def bias_prompt(l):
        t = _attn_tile(l)
        sub = min(t, MAX_DISTANCE)
        assert t % sub == 0 and (l == t or sub == MAX_DISTANCE)
        i = jnp.arange(sub, dtype=jnp.int32)
        rel = i[:, None] - i[None, :]
        return jnp.stack([_bias_tile(rel_bias, rel), _bias_tile(rel_bias, rel + sub)], axis=1)

    def bias_sample(t_new):
        tok = jnp.arange(t_new, dtype=jnp.int32)
        lane = jnp.arange(page, dtype=jnp.int32)
        rel_last = (page + tok)[:, None] - lane[None, :]
        rel_new = jnp.where(lane[None, :] < t_new, tok[:, None] - lane[None, :], -1)
        tiles = jnp.stack([_bias_tile(rel_bias, rel_last), _bias_tile(rel_bias, rel_new)], axis=0)
        tiles = jnp.broadcast_to(tiles[:, :, None], (2, n_diff, 2, t_new, page))
        return jnp.swapaxes(tiles.reshape(2, n_diff // 2, 4 * t_new, page), 2, 3)

    class Group:
        def __init__(self, x, paged):
            self.b, self.l, _ = x.shape
            self.m = self.b * self.l
            self.paged = paged
            self.x = x.reshape(self.m, d_model)
            self.k_all = jnp.zeros((depth, self.m * n_diff, 2 * HEAD_DIM), F32)
            self.v_all = jnp.zeros((depth, self.m * n_diff, 2 * HEAD_DIM), F32)
            self.states, self.convs = [], []
            self.bias = bias_sample(self.l) if paged else bias_prompt(self.l)

        def r3(self, a):
            return a.reshape(self.b, self.l, a.shape[-1])

        def before_attention(self, layer):
            b = self.b
            self.x = _ffn(self.x, n1, w1_gu, w1_d, layer)
            qkv, gate, ba, self.q, self.k_all, self.v_all, self.k_bf, self.v_bf = _inproj(
                self.x, nm, w_in_r, q_gain, k_gain, widths, layer, self.k_all, self.v_all)
            if self.paged:
                conv_buf, s0, bb = state_conv[layer], state_gdn[layer], 4 if b % 4 == 0 else 1
            else:
                conv_buf = jnp.zeros((b, conv_w.shape[1] - 1, conv_dim), F32)
                s0 = jnp.zeros((b, n_gdn, HEAD_DIM, HEAD_DIM), F32)
                bb = b
            self.o_gdn, s_new, conv_new = _gdn(self.r3(qkv), self.r3(gate), self.r3(ba), conv_buf, s0, conv_w,
                                               a_log, dt_bias, gdn_gain, layer, bb)
            self.states.append(s_new)
            self.convs.append(conv_new)

        def after_attention(self, layer, o_diff):
            self.x = _ffn(self.x, n2, w2_gu, w2_d, layer,
                          mix=(self.o_gdn.reshape(self.m, gdn_w), o_diff.reshape(self.m, diff_w), w_out_bf))

        def outputs(self):
            kv_shape = (depth, self.b, self.l, n_diff, 2 * HEAD_DIM)
            return (self.x.reshape(self.b, self.l, d_model), self.k_all.reshape(kv_shape),
                    self.v_all.reshape(kv_shape), jnp.stack(self.states), jnp.stack(self.convs))

    prompt, sample = Group(x_prompt, False), Group(x_sample, True)
    n_pages = page_table.shape[1]
    attn_steps = prompt.b * n_diff * (prompt.l // _attn_tile(prompt.l))
    n_pp_fused = _decode_pages_per_attn_step(attn_steps, sample.b, n_pages)
    for layer in range(depth):
        sample.before_attention(layer)
        prompt.before_attention(layer)
        dec = (sample.r3(sample.q), sample.k_all, sample.v_all, cache_k2, cache_v2, page_table, sample.bias)
        attn_args = (prompt.r3(prompt.q), prompt.r3(prompt.k_bf), prompt.r3(prompt.v_bf), prompt.bias,
                     diff_lambda, lam_init, diff_gain, layer)
        if n_pp_fused is not None:
            o_p, o_s = _attn(*attn_args, dec=dec + (n_pp_fused,))
        else:
            o_p = _attn(*attn_args)
            o_s = _decode(*dec, diff_lambda, lam_init, diff_gain, layer,
                          next(n for n in (16, 8, 4, 2, 1) if n_pages % n == 0))
        prompt.after_attention(layer, o_p)
        sample.after_attention(layer, o_s)

    y_p, k_p, v_p, s_p, c_p = prompt.outputs()
    y_s, k_s, v_s, s_s, c_s = sample.outputs()
    return (y_p, y_s, k_p, v_p, s_p, c_p, k_s, v_s, s_s, c_s)
```

```python
import functools
import math

import jax
import jax.numpy as jnp
from jax import lax
from jax.experimental import pallas as pl
from jax.experimental.pallas import tpu as pltpu

F32 = jnp.float32
BF16 = jnp.bfloat16

HEAD_DIM = 64
GDN_CHUNK = 64
MAX_DISTANCE = 128
RMS_EPS = 1e-6
L2_EPS = 1e-6
LANES = 128
MXU_DIM = 256
VMEM_LIMIT = 52 * 1024 * 1024
NEG_BIG = -1e30
LOG2E = 1.4426950408889634


def _dot(a, b):
    return jnp.dot(a, b, preferred_element_type=F32)


def _dot_nt(a, b):
    return lax.dot_general(a, b, (((1,), (1,)), ((), ())), preferred_element_type=F32)


def _dot_tn(a, b):
    return lax.dot_general(a, b, (((0,), (0,)), ((), ())), preferred_element_type=F32)


def _split(x, pieces):
    out = []
    for _ in range(pieces - 1):
        hi = x.astype(BF16)
        out.append(hi)
        x = x - hi.astype(F32)
    out.append(x.astype(BF16))
    return out


def _dot_exact_lhs(a_bf, b, pieces=3):
    return sum(_dot(a_bf, p) for p in _split(b, pieces))


def _group_sumsq(x, gmat):
    return _dot((x * x).astype(BF16), gmat)


def _group_matrix(width):
    r = lax.broadcasted_iota(jnp.int32, (width, width), 0) // HEAD_DIM
    c = lax.broadcasted_iota(jnp.int32, (width, width), 1) // HEAD_DIM
    return jnp.where(r == c, 1.0, 0.0).astype(BF16)


def _ffn_kernel(has_mix, cf, *refs):
    if has_mix:
        x_ref, og_ref, od_ref, wo_ref, nw_ref, wgu_ref, wd_ref, o_ref, h_s, a_s = refs
    else:
        x_ref, nw_ref, wgu_ref, wd_ref, o_ref, h_s, a_s = refs
    d_ff = wd_ref.shape[0]
    x = x_ref[...]
    if has_mix:
        half = og_ref.shape[-1]
        x = x + _dot(og_ref[...], wo_ref[0:half, :]) + _dot(od_ref[...], wo_ref[half:, :])
    ms = jnp.mean(x * x, axis=-1, keepdims=True)
    h_s[...] = (x * lax.rsqrt(ms + RMS_EPS) * nw_ref[...]).astype(BF16)
    for c0 in range(0, d_ff, cf):
        h = h_s[...]
        g = _dot(h, wgu_ref[:, c0:c0 + cf])
        u = _dot(h, wgu_ref[:, d_ff + c0:d_ff + c0 + cf])
        a_s[:, c0:c0 + cf] = ((g * jax.nn.sigmoid(g)) * u).astype(BF16)
    o_ref[...] = x + 0.5 * _dot(a_s[...], wd_ref[...])


def _ffn_tiles(m, d_ff):
    tm = 512 if m % 512 == 0 else m
    cf = 2 * LANES if d_ff % (2 * LANES) == 0 else d_ff
    return tm, cf


def _ffn(x, norm_w, w_gu, w_down, layer, mix=None):
    m, d = x.shape
    d_ff = w_down.shape[1]
    tm, cf = _ffn_tiles(m, d_ff)
    row = lambda i: (i, 0)
    lay3 = lambda i: (layer, 0, 0)
    once = pl.Buffered(1)
    in_specs = [pl.BlockSpec((tm, d), row)]
    args = [x]
    if mix is not None:
        og, od, w_out = mix
        in_specs += [pl.BlockSpec((tm, og.shape[1]), row), pl.BlockSpec((tm, od.shape[1]), row),
                     pl.BlockSpec((None,) + w_out.shape[1:], lay3, pipeline_mode=once)]
        args += [og, od, w_out]
    in_specs += [
        pl.BlockSpec((None, 1, d), lay3),
        pl.BlockSpec((None, d, 2 * d_ff), lay3, pipeline_mode=once),
        pl.BlockSpec((None, d_ff, d), lay3, pipeline_mode=once),
    ]
    args += [norm_w, w_gu, w_down]
    return pl.pallas_call(
        functools.partial(_ffn_kernel, mix is not None, cf),
        grid=(m // tm,),
        in_specs=in_specs,
        out_specs=pl.BlockSpec((tm, d), row),
        out_shape=jax.ShapeDtypeStruct((m, d), F32),
        scratch_shapes=[pltpu.VMEM((tm, d), BF16), pltpu.VMEM((tm, d_ff), BF16)],
        compiler_params=pltpu.CompilerParams(
            dimension_semantics=("parallel",), vmem_limit_bytes=VMEM_LIMIT),
        name="ffn_mix" if mix is not None else "ffn",
    )(*args)


def _inproj_kernel(widths, conv_tiles, *refs):
    if conv_tiles is None:
        (x_ref, nw_ref, w_ref, qg_ref, kg_ref, _, _,
         qkv_ref, gate_ref, ba_ref, q_ref, k_ref, v_ref, kb_ref, vb_ref) = refs
    else:
        (x_ref, nw_ref, w_ref, qg_ref, kg_ref, _, _, cbuf_ref, cw_ref,
         qkv_ref, gate_ref, ba_ref, q_ref, k_ref, v_ref, kb_ref, vb_ref, cnew_ref, xp_s) = refs
    c_qkv, c_gate, c_d, c_ba = widths
    x = x_ref[...]
    ms = jnp.mean(x * x, axis=-1, keepdims=True)
    h = (x * lax.rsqrt(ms + RMS_EPS) * nw_ref[...]).astype(BF16)
    o = 0
    if conv_tiles is None:
        qkv_ref[...] = _dot(h, w_ref[:, o:o + c_qkv])
    else:
        tm = x.shape[0]
        n_tail = cbuf_ref.shape[0]
        top = 8 - n_tail
        tile = pl.program_id(0) % conv_tiles

        @pl.when(tile == 0)
        def _():
            xp_s[top:8, :] = cbuf_ref[...]

        xp_s[8:8 + tm, :] = _dot(h, w_ref[:, o:o + c_qkv])
        cw = cw_ref[...]
        y = xp_s[top:top + tm, :] * cw[0:1, :]
        for i in range(1, n_tail + 1):
            y = y + xp_s[top + i:top + i + tm, :] * cw[i:i + 1, :]
        qkv_ref[...] = y * jax.nn.sigmoid(y)
        tail = xp_s[8 + tm - n_tail:8 + tm, :]
        xp_s[top:8, :] = tail

        @pl.when(tile == conv_tiles - 1)
        def _():
            cnew_ref[...] = tail
    o += c_qkv
    gate_ref[...] = _dot(h, w_ref[:, o:o + c_gate]); o += c_gate
    dq = _dot(h, w_ref[:, o:o + c_d]); o += c_d
    dk = _dot(h, w_ref[:, o:o + c_d]); o += c_d
    dv = _dot(h, w_ref[:, o:o + c_d]); o += c_d
    ba_ref[...] = _dot(h, w_ref[:, o:o + c_ba])
    gmat = _group_matrix(c_d)
    inv_hd = 1.0 / HEAD_DIM
    qn = dq * lax.rsqrt(_group_sumsq(dq, gmat) * inv_hd + RMS_EPS) * qg_ref[...]
    kn = dk * lax.rsqrt(_group_sumsq(dk, gmat) * inv_hd + RMS_EPS) * kg_ref[...]
    q_ref[...] = qn * (HEAD_DIM ** -0.5 * LOG2E)
    tm = kn.shape[0]
    dv_w = 2 * HEAD_DIM
    n_heads = c_d // dv_w
    for h in range(n_heads):
        k_ref[pl.ds(h, tm, stride=n_heads), :] = kn[:, h * dv_w:(h + 1) * dv_w]
        v_ref[pl.ds(h, tm, stride=n_heads), :] = dv[:, h * dv_w:(h + 1) * dv_w]
    kb_ref[...] = kn.astype(BF16)
    vb_ref[...] = dv.astype(BF16)


def _inproj_conv_ok(m, seq_len, n_tail):
    tm = 512 if m % 512 == 0 else m
    return seq_len % tm == 0 and 0 < n_tail <= 8


def _inproj(x, norm_w, w_in, q_gain, k_gain, widths, layer, k_all, v_all, conv=None):
    m, d = x.shape
    c_qkv, c_gate, c_d, c_ba = widths
    dv_w = 2 * HEAD_DIM
    n_heads = c_d // dv_w
    tm = 512 if m % 512 == 0 else m
    row = lambda i: (i, 0)
    lay3 = lambda i: (layer, 0, 0)
    slab = pl.BlockSpec((None, tm * n_heads, dv_w), lambda i: (layer, i, 0))
    outs = [(c_qkv, F32), (c_gate, F32), (c_ba, F32), (c_d, F32), None, None, (c_d, BF16), (c_d, BF16)]
    in_specs = [pl.BlockSpec((tm, d), row),
                pl.BlockSpec((None, 1, d), lay3),
                pl.BlockSpec((None,) + w_in.shape[1:], lay3),
                pl.BlockSpec((None, 1, c_d), lay3),
                pl.BlockSpec((None, 1, c_d), lay3),
                pl.BlockSpec(memory_space=pl.ANY),
                pl.BlockSpec(memory_space=pl.ANY)]
    out_specs = [slab if o is None else pl.BlockSpec((tm, o[0]), row) for o in outs]
    out_shape = [jax.ShapeDtypeStruct(k_all.shape, F32) if o is None else jax.ShapeDtypeStruct((m, o[0]), o[1])
                 for o in outs]
    args = [x, norm_w, w_in, q_gain, k_gain, k_all, v_all]
    conv_tiles, scratch = None, []
    if conv is not None:
        conv_buf, conv_w, seq_len = conv
        assert _inproj_conv_ok(m, seq_len, conv_buf.shape[1])
        conv_tiles = seq_len // tm
        state = pl.BlockSpec((None,) + conv_buf.shape[1:], lambda i: (i // conv_tiles, 0, 0))
        in_specs += [state, pl.BlockSpec((None,) + conv_w.shape[1:], lay3)]
        out_specs.append(state)
        out_shape.append(jax.ShapeDtypeStruct(conv_buf.shape, F32))
        args += [conv_buf, conv_w]
        scratch = [pltpu.VMEM((8 + tm, c_qkv), F32)]
    return pl.pallas_call(
        functools.partial(_inproj_kernel, widths, conv_tiles),
        grid=(m // tm,),
        in_specs=in_specs,
        out_specs=out_specs,
        out_shape=out_shape,
        scratch_shapes=scratch,
        input_output_aliases={5: 4, 6: 5},
        compiler_params=pltpu.CompilerParams(
            dimension_semantics=("arbitrary" if conv is not None else "parallel",), vmem_limit_bytes=VMEM_LIMIT),
        name="inproj",
    )(*args)


def _block_mask(rows, cols, row_group, col_group):
    r = lax.broadcasted_iota(jnp.int32, (rows, cols), 0) // row_group
    c = lax.broadcasted_iota(jnp.int32, (rows, cols), 1) // col_group
    return r == c


def _block_diag(x, mask):
    reps = mask.shape[0] // x.shape[0]
    return jnp.where(mask, jnp.concatenate([x] * reps, axis=0), 0.0).astype(BF16)


def _dot_exact_rhs(a, b_bf, pieces=3):
    return sum(_dot(p, b_bf) for p in _split(a, pieces))


def _gdn_kernel(n_heads, chunk, gh, nc, conv_done, qkv_ref, gate_ref, ba_ref, cbuf_ref, s0_ref, cw_ref, alog_ref,
                dtb_ref, gain_ref, o_ref, snew_ref, *rest):
    if conv_done:
        (s_s,) = rest
        cnew_ref = xp_s = None
    else:
        cnew_ref, xp_s, s_s = rest
    bb = qkv_ref.shape[0]
    c = chunk
    tb = nc * c
    d = HEAD_DIM
    width = n_heads * d
    n_grp = n_heads // gh
    tw = gh * c
    dw = gh * d
    n_tail = cbuf_ref.shape[1]
    n_taps = n_tail + 1
    top = 8 - n_tail
    t = pl.program_id(1)
    n_t = pl.num_programs(1)

    @pl.when(t == 0)
    def _():
        if not conv_done:
            xp_s[:, top:8, :] = cbuf_ref[...]
        s_s[...] = jnp.zeros_like(s_s)
        for b in range(bb):
            for h in range(n_heads):
                o = (h % gh) * d
                s_s[b, h // gh, o:o + d, o:o + d] = s0_ref[b, h]

    if not conv_done:
        xp_s[:, 8:8 + tb, :] = qkv_ref[...]

    cw = cw_ref[...]
    gmat = _group_matrix(width)
    ri = lax.broadcasted_iota(jnp.int32, (c, c), 0)
    ci = lax.broadcasted_iota(jnp.int32, (c, c), 1)
    tril_bf = jnp.where(ri >= ci, 1.0, 0.0).astype(BF16)
    rb = lax.broadcasted_iota(jnp.int32, (tb, tb), 0)
    cb = lax.broadcasted_iota(jnp.int32, (tb, tb), 1)
    tril_chunks = jnp.where((rb >= cb) & (rb // c == cb // c), 1.0, 0.0).astype(BF16)
    row_t = lax.broadcasted_iota(jnp.int32, (c, tw), 0)
    col_t = lax.broadcasted_iota(jnp.int32, (c, tw), 1) % c
    causal = row_t >= col_t
    strict = row_t > col_t
    eye = jnp.where(row_t == col_t, 1.0, 0.0).astype(F32)
    bd_tt = _block_mask(tw, tw, c, c)
    bd_td = _block_mask(tw, dw, c, d)
    bd_dd = _block_mask(dw, dw, d, d)
    gmat_g = jnp.where(bd_dd, 1.0, 0.0).astype(BF16)
    expand_d = jnp.where(_block_mask(LANES, width, 1, d), 1.0, 0.0).astype(BF16)
    expand_t = jnp.where(_block_mask(LANES, n_heads * c, 1, c), 1.0, 0.0).astype(BF16)
    neg_a = -jnp.exp(alog_ref[...])
    dtb = dtb_ref[...]
    gain = gain_ref[...]

    chains = []
    for b in range(bb):
        if conv_done:
            qkv = qkv_ref[b]
        else:
            y = xp_s[b, top:top + tb, :] * cw[0:1, :]
            for i in range(1, n_taps):
                y = y + xp_s[b, top + i:top + i + tb, :] * cw[i:i + 1, :]
            qkv = y * jax.nn.sigmoid(y)
        q_all = qkv[:, 0:width]
        k_all = qkv[:, width:2 * width]
        v_all = qkv[:, 2 * width:3 * width]
        q_all = q_all * lax.rsqrt(_group_sumsq(q_all, gmat) + L2_EPS) * (HEAD_DIM ** -0.5)
        k_all = k_all * lax.rsqrt(_group_sumsq(k_all, gmat) + L2_EPS)
        ba = ba_ref[b]
        beta_all = jax.nn.sigmoid(ba[:, 0:LANES])
        z = ba[:, LANES:2 * LANES] + dtb
        softplus = jnp.maximum(z, 0.0) + jnp.log1p(jnp.exp(-jnp.abs(z)))
        g_all = neg_a * softplus
        gc_all = _dot_exact_lhs(tril_chunks, g_all, 2)
        gate_all = gate_ref[b]
        gate_all = gate_all * jax.nn.sigmoid(gate_all)
        beta_e = _dot_exact_rhs(beta_all, expand_d, 1)
        gc_e = _dot_exact_rhs(gc_all, expand_d, 2)
        g_t = _dot_exact_rhs(g_all, expand_t, 2)
        egc_e = jnp.exp(gc_e)

        for cc in range(nc):
            rows = slice(cc * c, (cc + 1) * c)
            glast = gc_e[(cc + 1) * c - 1:(cc + 1) * c, :]
            kend_e = jnp.exp(glast - gc_e[rows])
            sdec_e = jnp.exp(glast)
            for gi in range(n_grp):
                sl = slice(gi * dw, (gi + 1) * dw)
                k = k_all[rows, sl]
                beta = beta_e[rows, sl]
                chains.append(dict(b=b, cc=cc, gi=gi, rows=rows, sl=sl, q=q_all[rows, sl], k=k, kb=k * beta,
                                   vb=v_all[rows, sl] * beta, egc=egc_e[rows, sl],
                                   g_t=g_t[rows, gi * tw:(gi + 1) * tw], kend=kend_e[:, sl], sdec=sdec_e[:, sl],
                                   gate=gate_all[rows, sl]))

    for ch in chains:
        dm = _dot_exact_lhs(tril_bf, jnp.where(strict, ch["g_t"], 0.0), 2)
        ch["decay"] = jnp.where(causal, jnp.exp(dm), 0.0)
        ch["k_bd"] = _block_diag(ch["k"], bd_td)
    for ch in chains:
        ch["a"] = jnp.where(strict, _dot_nt(ch["kb"].astype(BF16), ch["k_bd"]) * ch["decay"], 0.0)
        ch["tinv"] = eye - jnp.where(row_t // 2 == col_t // 2, ch["a"], 0.0)
    size = 2
    while size < c:
        lower_left = ((row_t // (2 * size) == col_t // (2 * size))
                      & (row_t % (2 * size) >= size) & (col_t % (2 * size) < size))
        for ch in chains:
            a21 = _block_diag(jnp.where(lower_left, ch["a"], 0.0), bd_tt)
            ch["t2_a21"] = _dot(ch["tinv"].astype(BF16), a21)
        for ch in chains:
            ch["tinv"] = ch["tinv"] - _dot(ch["t2_a21"].astype(BF16), _block_diag(ch["tinv"], bd_tt))
        size *= 2
    for ch in chains:
        tinv_bf = ch["tinv"].astype(BF16)
        ch["u"] = _dot(tinv_bf, _block_diag(ch["vb"], bd_td))
        ch["w"] = _dot(tinv_bf, _block_diag(ch["kb"] * ch["egc"], bd_td))
        ch["qk"] = jnp.where(causal, _dot_nt(ch["q"].astype(BF16), ch["k_bd"]) * ch["decay"], 0.0)
    for cc in range(nc):
        now = [ch for ch in chains if ch["cc"] == cc]
        for ch in now:
            ch["s"] = s_s[ch["b"], ch["gi"]]
            lhs = jnp.concatenate([ch["w"], ch["q"] * ch["egc"]], axis=0).astype(BF16)
            ch["ws"] = _dot(lhs, ch["s"].astype(BF16))
        for ch in now:
            v_new = ch["u"] - ch["ws"][0:c]
            ch["o"] = ch["ws"][c:2 * c] + _dot(ch["qk"].astype(BF16), _block_diag(v_new, bd_td))
            k_end = ch["k"] * ch["kend"]
            cross = _dot_tn(k_end.astype(BF16), v_new.astype(BF16))
            s_s[ch["b"], ch["gi"]] = ch["s"] * ch["sdec"] + jnp.where(bd_dd, cross, 0.0)
    for ch in chains:
        o = ch["o"]
        ms = _group_sumsq(o, gmat_g) * (1.0 / d)
        on = o * lax.rsqrt(ms + RMS_EPS) * gain[:, ch["sl"]]
        o_ref[ch["b"], ch["rows"], ch["sl"]] = (on * ch["gate"]).astype(o_ref.dtype)

    if not conv_done:
        tail = xp_s[:, 8 + tb - n_tail:8 + tb, :]
        xp_s[:, top:8, :] = tail

    @pl.when(t == n_t - 1)
    def _():
        if not conv_done:
            cnew_ref[...] = tail
        for b in range(bb):
            for h in range(n_heads):
                o = (h % gh) * d
                snew_ref[b, h] = s_s[b, h // gh, o:o + d, o:o + d]


def _gdn(qkv, gate, ba, conv_buf, s0, conv_w, a_log, dt_bias, out_gain, layer, batch_block, conv_done=False):
    b, l, w3 = qkv.shape
    width = w3 // 3
    n_heads = width // HEAD_DIM
    chunk = min(GDN_CHUNK, l)
    assert l % chunk == 0 and chunk % 8 == 0 and chunk & (chunk - 1) == 0 and b % batch_block == 0
    n_tail = conv_buf.shape[1]
    assert n_tail <= min(8, chunk)
    bb = batch_block
    gh = min(n_heads, max(1, MXU_DIM // chunk))
    assert n_heads % gh == 0
    nc = next(n for n in (4, 2, 1) if (l // chunk) % n == 0)
    tb = nc * chunk
    blk = lambda i, t: (i, t, 0)
    fix3 = lambda i, t: (i, 0, 0)
    lay3 = lambda i, t: (layer, 0, 0)
    n_out = 2 if conv_done else 3
    return pl.pallas_call(
        functools.partial(_gdn_kernel, n_heads, chunk, gh, nc, conv_done),
        grid=(b // bb, l // tb),
        in_specs=[pl.BlockSpec((bb, tb, w3), blk),
                  pl.BlockSpec((bb, tb, width), blk),
                  pl.BlockSpec((bb, tb, ba.shape[2]), blk),
                  pl.BlockSpec((bb, n_tail, w3), fix3),
                  pl.BlockSpec((bb, n_heads, HEAD_DIM, HEAD_DIM), lambda i, t: (i, 0, 0, 0)),
                  pl.BlockSpec((None,) + conv_w.shape[1:], lay3),
                  pl.BlockSpec((None, 1, LANES), lay3),
                  pl.BlockSpec((None, 1, LANES), lay3),
                  pl.BlockSpec((None, 1, width), lay3)],
        out_specs=[pl.BlockSpec((bb, tb, width), blk),
                   pl.BlockSpec((bb, n_heads, HEAD_DIM, HEAD_DIM), lambda i, t: (i, 0, 0, 0)),
                   pl.BlockSpec((bb, n_tail, w3), fix3)][:n_out],
        out_shape=[jax.ShapeDtypeStruct((b, l, width), BF16),
                   jax.ShapeDtypeStruct(s0.shape, F32),
                   jax.ShapeDtypeStruct(conv_buf.shape, F32)][:n_out],
        scratch_shapes=([] if conv_done else [pltpu.VMEM((bb, 8 + tb, w3), F32)])
                       + [pltpu.VMEM((bb, n_heads // gh, gh * HEAD_DIM, gh * HEAD_DIM), F32)],
        compiler_params=pltpu.CompilerParams(
            dimension_semantics=("parallel", "arbitrary"), vmem_limit_bytes=VMEM_LIMIT),
        name="gdn",
    )(qkv, gate, ba, conv_buf, s0, conv_w, a_log, dt_bias, out_gain)


def _rel_bucket(rel, n_buckets):
    n = jnp.maximum(rel, 0)
    max_exact = n_buckets // 2
    nf = jnp.maximum(n, 1).astype(F32)
    large = max_exact + (jnp.log(nf / max_exact) / math.log(MAX_DISTANCE / max_exact)
                         * (n_buckets - max_exact)).astype(jnp.int32)
    large = jnp.minimum(large, n_buckets - 1)
    return jnp.where(n < max_exact, n, large)


def _bias_tile(rel_bias, rel):
    n_buckets, n_heads = rel_bias.shape
    tab = rel_bias.astype(F32) - rel_bias[n_buckets - 1].astype(F32)[None, :]
    bucket = _rel_bucket(rel, n_buckets)[None]
    bias = jnp.zeros((n_heads,) + rel.shape, F32)
    for n in range(n_buckets):
        bias = jnp.where(bucket == n, tab[n].reshape((n_heads,) + (1,) * rel.ndim), bias)
    return jnp.where((rel >= 0)[None], bias * LOG2E, -jnp.inf)


def _lambda(lp_ref, li_ref):
    lp = lp_ref[...]
    e1 = jnp.exp(jnp.sum(lp[0:1, :] * lp[1:2, :], axis=-1, keepdims=True))
    e2 = jnp.exp(jnp.sum(lp[2:3, :] * lp[3:4, :], axis=-1, keepdims=True))
    return e1 - e2 + li_ref[:, 0:1]


def _attn_kernel(tk, sub, dec, *refs):
    if dec is None:
        q_ref, k_ref, v_ref, bias_ref, lp_ref, li_ref, gain_ref, o_ref, q_s, m_s, acc_s, vx_s, s0_s, s1_s = refs
    else:
        n_pp, n_dec_heads, n_groups = dec
        q_ref, k_ref, v_ref, bias_ref, lp_ref, li_ref, gain_ref, dq_ref, kn_ref, vn_ref, dbias_ref = refs[1:12]
        k_pages = refs[12:12 + n_pp]
        v_pages = refs[12 + n_pp:12 + 2 * n_pp]
        o_ref, od_ref, q_s, m_s, acc_s, vx_s, s0_s, s1_s = refs[12 + 2 * n_pp:20 + 2 * n_pp]
        step_id = ((pl.program_id(0) * pl.num_programs(1) + pl.program_id(1)) * pl.num_programs(2)
                   + pl.program_id(2))
        _decode_step(n_dec_heads, step_id % n_groups, n_groups, dq_ref, kn_ref, vn_ref, dbias_ref, lp_ref, li_ref,
                     gain_ref, k_pages, v_pages, od_ref, *refs[20 + 2 * n_pp:])
    tq = q_ref.shape[0]
    dv = 2 * HEAD_DIM
    qi = pl.program_id(2)

    @pl.when(qi == 0)
    def _():
        vx_s[:, 0:dv] = v_ref[...]
        vx_s[:, dv:2 * dv] = jnp.ones((vx_s.shape[0], dv), BF16)

    q = q_ref[...]
    lane = lax.broadcasted_iota(jnp.int32, q.shape, 1)
    q_s[0:tq, :] = jnp.where(lane < HEAD_DIM, q, 0.0).astype(BF16)
    q_s[tq:2 * tq, :] = jnp.where(lane >= HEAD_DIM, q, 0.0).astype(BF16)
    m_s[...] = jnp.full_like(m_s, NEG_BIG)
    acc_s[...] = jnp.zeros_like(acc_s)

    def scores(buf, j):
        ks = pl.multiple_of(j * tk, tk)
        buf[...] = _dot_nt(q_s[...], k_ref[pl.ds(ks, tk), :])

    def update(buf, j):
        ks = pl.multiple_of(j * tk, tk)
        m_prev = m_s[...]
        m_new = jnp.maximum(m_prev, jnp.max(buf[...], axis=-1, keepdims=True))
        p = jnp.exp2(buf[...] - jnp.concatenate([m_new] * (tk // LANES), axis=1)).astype(BF16)
        alpha = jnp.exp2(m_prev - m_new)
        pv = _dot(p, vx_s[pl.ds(ks, tk), :])
        acc_s[...] = jnp.concatenate([alpha] * (2 * dv // LANES), axis=1) * acc_s[...] + pv
        m_s[...] = m_new

    def below_diagonal_bias(buf):
        for m in range(2):
            buf[m * tq:m * tq + sub, tk - sub:tk] += bias_ref[1]

    def diagonal_bias(buf):
        for m in range(2):
            for r in range(tq // sub):
                rows = slice(m * tq + r * sub, m * tq + (r + 1) * sub)
                if r >= 1:
                    buf[rows, (r - 1) * sub:r * sub] += bias_ref[1]
                buf[rows, r * sub:(r + 1) * sub] += bias_ref[0]
                if (r + 1) * sub < tk:
                    buf[rows, (r + 1) * sub:tk] = jnp.full((sub, tk - (r + 1) * sub), -jnp.inf, F32)

    n_pairs = jnp.maximum(qi - 1, 0) // 2
    scores(s0_s, 0)

    def pair(i, carry):
        scores(s1_s, 2 * i + 1)
        update(s0_s, 2 * i)
        scores(s0_s, 2 * i + 2)
        update(s1_s, 2 * i + 1)
        return carry

    lax.fori_loop(0, n_pairs, pair, 0)

    @pl.when(qi == 0)
    def _():
        diagonal_bias(s0_s)
        update(s0_s, 0)

    @pl.when(qi % 2 == 1)
    def _():
        scores(s1_s, qi)
        below_diagonal_bias(s0_s)
        diagonal_bias(s1_s)
        update(s0_s, qi - 1)
        update(s1_s, qi)

    @pl.when((qi % 2 == 0) & (qi >= 2))
    def _():
        scores(s1_s, qi - 1)
        below_diagonal_bias(s1_s)
        update(s0_s, qi - 2)
        scores(s0_s, qi)
        diagonal_bias(s0_s)
        update(s1_s, qi - 1)
        update(s0_s, qi)

    lam = _lambda(lp_ref, li_ref)
    acc = acc_s[...]
    on = acc[:, 0:dv] * (1.0 / acc[:, dv:2 * dv])
    o = on[0:tq] - lam * on[tq:2 * tq]
    ms = jnp.mean(o * o, axis=-1, keepdims=True)
    o_ref[...] = (o * lax.rsqrt(ms + RMS_EPS) * gain_ref[...] * (1.0 - li_ref[:, 0:1])).astype(o_ref.dtype)


def _attn_tile(l):
    return 512 if l % 512 == 0 else l


def _decode_pages_per_attn_step(attn_steps, n_seq, n_pages):
    total = n_seq * n_pages
    if total % attn_steps:
        return None
    n_pp = total // attn_steps
    return n_pp if (0 < n_pp <= 16 and n_pages % n_pp == 0) else None


def _attn(q, k_bf, v_bf, bias, lam_params, lam_init, out_gain, layer, dec=None):
    b, l, width = q.shape
    dv = 2 * HEAD_DIM
    assert dv == LANES
    n_heads = width // dv
    tq = tk = _attn_tile(l)
    nq = l // tq
    sub = bias.shape[-1]
    lay3 = lambda bi, h, qi, *_: (layer, 0, 0)
    in_specs = [pl.BlockSpec((None, tq, dv), lambda bi, h, qi, *_: (bi, qi, h)),
                pl.BlockSpec((None, l, dv), lambda bi, h, qi, *_: (bi, 0, h)),
                pl.BlockSpec((None, l, dv), lambda bi, h, qi, *_: (bi, 0, h)),
                pl.BlockSpec((None, 2, sub, sub), lambda bi, h, qi, *_: (h, 0, 0, 0)),
                pl.BlockSpec((None,) + lam_params.shape[1:], lay3),
                pl.BlockSpec((None, 1, LANES), lay3),
                pl.BlockSpec((None, 1, dv), lay3)]
    out_specs = pl.BlockSpec((None, tq, dv), lambda bi, h, qi, *_: (bi, qi, h))
    out_shape = jax.ShapeDtypeStruct((b, l, width), BF16)
    scratch = [pltpu.VMEM((2 * tq, dv), BF16),
               pltpu.VMEM((2 * tq, LANES), F32),
               pltpu.VMEM((2 * tq, 2 * dv), F32),
               pltpu.VMEM((l, 2 * dv), BF16),
               pltpu.VMEM((2 * tq, tk), F32),
               pltpu.VMEM((2 * tq, tk), F32)]
    args = [q, k_bf, v_bf, bias, lam_params, lam_init, out_gain]
    if dec is None:
        return pl.pallas_call(
            functools.partial(_attn_kernel, tk, sub, None),
            grid=(b, n_heads, nq), in_specs=in_specs, out_specs=out_specs, out_shape=out_shape,
            scratch_shapes=scratch,
            compiler_params=pltpu.CompilerParams(
                dimension_semantics=("parallel", "parallel", "arbitrary"), vmem_limit_bytes=VMEM_LIMIT),
            name="attn",
        )(*args)

    dq, k_new, v_new, cache_k, cache_v, page_table, dbias, n_pp = dec
    n_seq, t_new, dwidth = dq.shape
    dh = dwidth // dv
    page = cache_k.shape[2] // dh
    n_groups = page_table.shape[1] // n_pp
    assert b * n_heads * nq == n_seq * n_groups and t_new <= page
    rows = 2 * t_new

    def step_id(bi, h, qi):
        return (bi * n_heads + h) * nq + qi

    seq3 = lambda bi, h, qi, pt: (step_id(bi, h, qi) // n_groups, 0, 0)
    new3 = lambda bi, h, qi, pt: (layer, step_id(bi, h, qi) // n_groups, 0)

    def page_spec(i):
        def index(bi, h, qi, pt):
            sid = step_id(bi, h, qi)
            return (layer, pt[sid * n_pp + i], 0, 0)
        return pl.BlockSpec((None, None, page * dh, dv), index)

    in_specs += [pl.BlockSpec((None, t_new, dwidth), seq3),
                 pl.BlockSpec((None, t_new * dh, dv), new3),
                 pl.BlockSpec((None, t_new * dh, dv), new3),
                 pl.BlockSpec(dbias.shape, lambda bi, h, qi, pt: (0, 0, 0, 0))]
    in_specs += [page_spec(i) for i in range(n_pp)] * 2
    scratch += _decode_scratch(dh, t_new, page)
    grid_spec = pltpu.PrefetchScalarGridSpec(
        num_scalar_prefetch=1, grid=(b, n_heads, nq), in_specs=in_specs,
        out_specs=[out_specs, pl.BlockSpec((None, t_new, dwidth), seq3)], scratch_shapes=scratch)
    return pl.pallas_call(
        functools.partial(_attn_kernel, tk, sub, (n_pp, dh, n_groups)),
        grid_spec=grid_spec,
        out_shape=[out_shape, jax.ShapeDtypeStruct((n_seq, t_new, dwidth), BF16)],
        compiler_params=pltpu.CompilerParams(
            dimension_semantics=("arbitrary", "arbitrary", "arbitrary"), vmem_limit_bytes=VMEM_LIMIT),
        name="attn_decode",
    )(page_table.reshape(-1), *args, dq, k_new, v_new, dbias, *([cache_k] * n_pp), *([cache_v] * n_pp))


def _decode_scratch(n_heads, t_new, page):
    assert n_heads % 2 == 0
    n_pairs, cols, width = n_heads // 2, 4 * t_new, 4 * HEAD_DIM
    return [pltpu.VMEM((n_pairs, cols, width), F32),
            pltpu.VMEM((n_pairs, cols, LANES), F32),
            pltpu.VMEM((n_pairs, cols, LANES), F32),
            pltpu.VMEM((n_pairs, cols, width), F32),
            pltpu.VMEM((n_pairs, page, width), F32),
            pltpu.VMEM((n_pairs, page, width), F32)]


def _decode_kernel(n_pp, n_heads, pt_ref, q_ref, kn_ref, vn_ref, bias_ref, lp_ref, li_ref, gain_ref, *rest):
    del pt_ref
    _decode_step(n_heads, pl.program_id(1), pl.num_programs(1), q_ref, kn_ref, vn_ref, bias_ref, lp_ref, li_ref,
                 gain_ref, rest[:n_pp], rest[n_pp:2 * n_pp], *rest[2 * n_pp:])


def _decode_step(n_heads, j, n_j, q_ref, kn_ref, vn_ref, bias_ref, lp_ref, li_ref, gain_ref, k_refs, v_refs,
                 o_ref, q_s, m_s, l_s, acc_s, kpad_s, vpad_s):
    n_pp = len(k_refs)
    t_new = q_ref.shape[0]
    page = k_refs[0].shape[0] // n_heads
    dv = 2 * HEAD_DIM
    rows = 2 * t_new
    n_pairs = n_heads // 2
    cols = 2 * rows

    @pl.when(j == 0)
    def _():
        m_s[...] = jnp.full_like(m_s, NEG_BIG)
        l_s[...] = jnp.zeros_like(l_s)
        acc_s[...] = jnp.zeros_like(acc_s)
        q_s[...] = jnp.zeros_like(q_s)
        kpad_s[...] = jnp.zeros_like(kpad_s)
        vpad_s[...] = jnp.zeros_like(vpad_s)
        lane = lax.broadcasted_iota(jnp.int32, (t_new, dv), 1)
        for h in range(n_heads):
            pr, side = h // 2, h % 2
            cs = slice(h * dv, (h + 1) * dv)
            ls = slice(side * dv, (side + 1) * dv)
            qh = q_ref[:, cs]
            q_s[pr, side * rows:side * rows + t_new, ls] = jnp.where(lane < HEAD_DIM, qh, 0.0)
            q_s[pr, side * rows + t_new:(side + 1) * rows, ls] = jnp.where(lane >= HEAD_DIM, qh, 0.0)
            kpad_s[pr, 0:t_new, ls] = kn_ref[pl.ds(h, t_new, stride=n_heads), :]
            vpad_s[pr, 0:t_new, ls] = vn_ref[pl.ds(h, t_new, stride=n_heads), :]

    def pair_rows(ref, pr):
        return jnp.concatenate([ref[pl.ds(2 * pr, page, stride=n_heads), :],
                                ref[pl.ds(2 * pr + 1, page, stride=n_heads), :]], axis=1)

    def pair_step(n_plain, with_tail):
        state = []
        for pr in range(n_pairs):
            pieces = []
            if n_plain:
                k_cat = jnp.concatenate([pair_rows(k_refs[i], pr) for i in range(n_plain)], axis=0)
                pieces.append((jnp.transpose(_dot_nt(k_cat, q_s[pr])), range(n_plain)))
            if with_tail:
                k_cat = jnp.concatenate([pair_rows(k_refs[n_pp - 1], pr), kpad_s[pr]], axis=0)
                bias = jnp.concatenate([bias_ref[0, pr], bias_ref[1, pr]], axis=0)
                pieces.append((jnp.transpose(_dot_nt(k_cat, q_s[pr]) + bias), None))
            state.append(pieces)
        m_new = []
        for pr in range(n_pairs):
            m_cur = state[pr][0][0].max(axis=1, keepdims=True)
            for s, _ in state[pr][1:]:
                m_cur = jnp.maximum(m_cur, s.max(axis=1, keepdims=True))
            m_new.append(jnp.maximum(m_s[pr], m_cur))
        for pr in range(n_pairs):
            alpha = jnp.exp2(m_s[pr] - m_new[pr])
            psum = jnp.zeros((cols, 1), F32)
            pv = jnp.zeros((cols, 2 * dv), F32)
            for s, plain in state[pr]:
                p = jnp.exp2(s - m_new[pr][:, 0:1])
                psum = psum + jnp.sum(p, axis=1, keepdims=True)
                if plain is not None:
                    v_cat = jnp.concatenate([pair_rows(v_refs[i], pr) for i in plain], axis=0)
                else:
                    v_cat = jnp.concatenate([pair_rows(v_refs[n_pp - 1], pr), vpad_s[pr]], axis=0)
                pv = pv + _dot(p, v_cat)
            l_s[pr] = alpha * l_s[pr] + psum
            acc_s[pr] = jnp.concatenate([alpha] * (2 * dv // LANES), axis=1) * acc_s[pr] + pv
            m_s[pr] = m_new[pr]

    @pl.when(j < n_j - 1)
    def _():
        pair_step(n_pp, False)

    @pl.when(j == n_j - 1)
    def _():
        pair_step(n_pp - 1, True)
        lam = _lambda(lp_ref, li_ref)
        scale = 1.0 - li_ref[:, 0:1]
        for h in range(n_heads):
            pr, side = h // 2, h % 2
            on = (acc_s[pr][side * rows:(side + 1) * rows, side * dv:(side + 1) * dv]
                  / l_s[pr][side * rows:(side + 1) * rows, :])
            o = on[0:t_new] - lam * on[t_new:rows]
            ms = jnp.mean(o * o, axis=-1, keepdims=True)
            o_ref[:, h * dv:(h + 1) * dv] = (o * lax.rsqrt(ms + RMS_EPS) * gain_ref[...] * scale).astype(o_ref.dtype)


def _decode(q, k_new, v_new, cache_k, cache_v, page_table, bias, lam_params, lam_init, out_gain, layer,
            pages_per_step):
    b, t_new, width = q.shape
    dv = 2 * HEAD_DIM
    n_heads = width // dv
    page = cache_k.shape[2] // n_heads
    n_pages = page_table.shape[1]
    n_pp = pages_per_step
    assert n_pages % n_pp == 0 and t_new <= page
    rows = 2 * t_new
    seq = lambda bi, j, pt: (bi, 0, 0)
    lay3 = lambda bi, j, pt: (layer, 0, 0)

    def page_spec(i):
        return pl.BlockSpec((None, None, page * n_heads, dv),
                            lambda bi, j, pt: (layer, pt[bi, j * n_pp + i], 0, 0))

    grid_spec = pltpu.PrefetchScalarGridSpec(
        num_scalar_prefetch=1,
        grid=(b, n_pages // n_pp),
        in_specs=[pl.BlockSpec((None, t_new, width), seq),
                  pl.BlockSpec((None, t_new * n_heads, dv), lambda bi, j, pt: (layer, bi, 0)),
                  pl.BlockSpec((None, t_new * n_heads, dv), lambda bi, j, pt: (layer, bi, 0)),
                  pl.BlockSpec(bias.shape, lambda bi, j, pt: (0, 0, 0, 0)),
                  pl.BlockSpec((None,) + lam_params.shape[1:], lay3),
                  pl.BlockSpec((None, 1, LANES), lay3),
                  pl.BlockSpec((None, 1, dv), lay3)]
                 + [page_spec(i) for i in range(n_pp)] * 2,
        out_specs=pl.BlockSpec((None, t_new, width), seq),
        scratch_shapes=_decode_scratch(n_heads, t_new, page))
    return pl.pallas_call(
        functools.partial(_decode_kernel, n_pp, n_heads),
        grid_spec=grid_spec,
        out_shape=jax.ShapeDtypeStruct((b, t_new, width), BF16),
        compiler_params=pltpu.CompilerParams(
            dimension_semantics=("parallel", "arbitrary"), vmem_limit_bytes=VMEM_LIMIT),
        name="decode",
    )(page_table, q, k_new, v_new, bias, lam_params, lam_init, out_gain,
      *([cache_k] * n_pp), *([cache_v] * n_pp))


def _pad_lanes(a, width=LANES):
    return jnp.pad(a, [(0, 0)] * (a.ndim - 1) + [(0, width - a.shape[-1])])


def kernel(x_prompt, x_sample, cache_k, cache_v, state_gdn, state_conv, page_table, ffn1_norm, ffn1_w_gate_up, ffn1_w_down, mix_norm, w_in, conv_w, gdn_a_log, gdn_dt_bias, gdn_out_norm, diff_q_norm, diff_k_norm, diff_lambda, diff_out_norm, rel_bias, w_out, ffn2_norm, ffn2_w_gate_up, ffn2_w_down):
    depth, d_model, _ = w_in.shape
    n_gdn = gdn_a_log.shape[1]
    n_diff = rel_bias.shape[1]
    gdn_w = n_gdn * HEAD_DIM
    diff_w = n_diff * 2 * HEAD_DIM
    conv_dim = conv_w.shape[2]
    assert conv_dim == 3 * gdn_w and w_out.shape[1] == gdn_w + diff_w
    page = cache_k.shape[2]
    assert page >= MAX_DISTANCE
    past_len = page_table.shape[1] * page

    o = 0
    cols = {}
    for name, wd in (("qkv", conv_dim), ("gate", gdn_w), ("b", n_gdn), ("a", n_gdn),
                     ("dq", diff_w), ("dk", diff_w), ("dv", diff_w)):
        cols[name] = w_in[:, :, o:o + wd]
        o += wd
    w_in_r = jnp.concatenate([cols["qkv"], cols["gate"], cols["dq"], cols["dk"], cols["dv"],
                              _pad_lanes(cols["b"]), _pad_lanes(cols["a"])], axis=-1).astype(BF16)
    widths = (conv_dim, gdn_w, diff_w, 2 * LANES)
    bf = lambda a: a.astype(BF16)
    w1_gu, w1_d, w2_gu, w2_d, w_out_bf = bf(ffn1_w_gate_up), bf(ffn1_w_down), bf(ffn2_w_gate_up), bf(ffn2_w_down), bf(w_out)
    row3 = lambda a: a.reshape(depth, 1, -1)
    n1, n2, nm = row3(ffn1_norm), row3(ffn2_norm), row3(mix_norm)
    q_gain = row3(jnp.tile(diff_q_norm, (1, diff_w // HEAD_DIM)))
    k_gain = row3(jnp.tile(diff_k_norm, (1, diff_w // HEAD_DIM)))
    a_log = row3(_pad_lanes(gdn_a_log))
    dt_bias = row3(_pad_lanes(gdn_dt_bias))
    gdn_gain = row3(jnp.tile(gdn_out_norm, (1, n_gdn)))
    diff_gain = row3(diff_out_norm)
    lam_init = jnp.asarray([0.8 - 0.6 * math.exp(-0.3 * l) for l in range(depth)], F32)
    lam_init = jnp.broadcast_to(lam_init[:, None, None], (depth, 1, LANES))
    cache_k2 = cache_k.reshape(cache_k.shape[:2] + (page * n_diff, 2 * HEAD_DIM))
    cache_v2 = cache_v.reshape(cache_v.shape[:2] + (page * n_diff, 2 * HEAD_DIM))

    def bias_prompt(l):
        t = _attn_tile(l)
        sub = min(t, MAX_DISTANCE)
        assert t % sub == 0 and (l == t or sub == MAX_DISTANCE)
        i = jnp.arange(sub, dtype=jnp.int32)
        rel = i[:, None] - i[None, :]
        return jnp.stack([_bias_tile(rel_bias, rel), _bias_tile(rel_bias, rel + sub)], axis=1)

    def bias_sample(t_new):
        tok = jnp.arange(t_new, dtype=jnp.int32)
        lane = jnp.arange(page, dtype=jnp.int32)
        rel_last = (page + tok)[:, None] - lane[None, :]
        rel_new = jnp.where(lane[None, :] < t_new, tok[:, None] - lane[None, :], -1)
        tiles = jnp.stack([_bias_tile(rel_bias, rel_last), _bias_tile(rel_bias, rel_new)], axis=0)
        tiles = jnp.broadcast_to(tiles[:, :, None], (2, n_diff, 2, t_new, page))
        return jnp.swapaxes(tiles.reshape(2, n_diff // 2, 4 * t_new, page), 2, 3)

    class Group:
        def __init__(self, x, paged):
            self.b, self.l, _ = x.shape
            self.m = self.b * self.l
            self.paged = paged
            self.x = x.reshape(self.m, d_model)
            self.k_all = jnp.zeros((depth, self.m * n_diff, 2 * HEAD_DIM), F32)
            self.v_all = jnp.zeros((depth, self.m * n_diff, 2 * HEAD_DIM), F32)
            self.states, self.convs = [], []
            self.bias = bias_sample(self.l) if paged else bias_prompt(self.l)

        def r3(self, a):
            return a.reshape(self.b, self.l, a.shape[-1])

        def before_attention(self, layer):
            b = self.b
            self.x = _ffn(self.x, n1, w1_gu, w1_d, layer)
            if self.paged:
                conv_buf, s0, bb = state_conv[layer], state_gdn[layer], 4 if b % 4 == 0 else 1
            else:
                conv_buf = jnp.zeros((b, conv_w.shape[1] - 1, conv_dim), F32)
                s0 = jnp.zeros((b, n_gdn, HEAD_DIM, HEAD_DIM), F32)
                bb = b
            conv_in_proj = _inproj_conv_ok(self.m, self.l, conv_buf.shape[1])
            outs = _inproj(self.x, nm, w_in_r, q_gain, k_gain, widths, layer, self.k_all, self.v_all,
                           conv=(conv_buf, conv_w, self.l) if conv_in_proj else None)
            qkv, gate, ba, self.q, self.k_all, self.v_all, self.k_bf, self.v_bf = outs[:8]
            gdn_out = _gdn(self.r3(qkv), self.r3(gate), self.r3(ba), conv_buf, s0, conv_w, a_log, dt_bias,
                           gdn_gain, layer, bb, conv_done=conv_in_proj)
            self.o_gdn, s_new = gdn_out[:2]
            self.states.append(s_new)
            self.convs.append(outs[8] if conv_in_proj else gdn_out[2])

        def after_attention(self, layer, o_diff):
            self.x = _ffn(self.x, n2, w2_gu, w2_d, layer,
                          mix=(self.o_gdn.reshape(self.m, gdn_w), o_diff.reshape(self.m, diff_w), w_out_bf))

        def outputs(self):
            kv_shape = (depth, self.b, self.l, n_diff, 2 * HEAD_DIM)
            return (self.x.reshape(self.b, self.l, d_model), self.k_all.reshape(kv_shape),
                    self.v_all.reshape(kv_shape), jnp.stack(self.states), jnp.stack(self.convs))

    prompt, sample = Group(x_prompt, False), Group(x_sample, True)
    n_pages = page_table.shape[1]
    attn_steps = prompt.b * n_diff * (prompt.l // _attn_tile(prompt.l))
    n_pp_fused = _decode_pages_per_attn_step(attn_steps, sample.b, n_pages)
    for layer in range(depth):
        sample.before_attention(layer)
        prompt.before_attention(layer)
        dec = (sample.r3(sample.q), sample.k_all, sample.v_all, cache_k2, cache_v2, page_table, sample.bias)
        attn_args = (prompt.r3(prompt.q), prompt.r3(prompt.k_bf), prompt.r3(prompt.v_bf), prompt.bias,
                     diff_lambda, lam_init, diff_gain, layer)
        if n_pp_fused is not None:
            o_p, o_s = _attn(*attn_args, dec=dec + (n_pp_fused,))
        else:
            o_p = _attn(*attn_args)
            o_s = _decode(*dec, diff_lambda, lam_init, diff_gain, layer,
                          next(n for n in (16, 8, 4, 2, 1) if n_pages % n == 0))
        prompt.after_attention(layer, o_p)
        sample.after_attention(layer, o_s)

    y_p, k_p, v_p, s_p, c_p = prompt.outputs()
    y_s, k_s, v_s, s_s, c_s = sample.outputs()
    return (y_p, y_s, k_p, v_p, s_p, c_p, k_s, v_s, s_s, c_s)
```

```python
import functools
import math

import jax
import jax.numpy as jnp
from jax import lax
from jax.experimental import pallas as pl
from jax.experimental.pallas import tpu as pltpu

F32 = jnp.float32
BF16 = jnp.bfloat16

HEAD_DIM = 64
GDN_CHUNK = 64
MAX_DISTANCE = 128
RMS_EPS = 1e-6
L2_EPS = 1e-6
LANES = 128
MXU_DIM = 256
VMEM_LIMIT = 52 * 1024 * 1024
NEG_BIG = -1e30
LOG2E = 1.4426950408889634


def _dot(a, b):
    return jnp.dot(a, b, preferred_element_type=F32)


def _dot_nt(a, b):
    return lax.dot_general(a, b, (((1,), (1,)), ((), ())), preferred_element_type=F32)


def _dot_tn(a, b):
    return lax.dot_general(a, b, (((0,), (0,)), ((), ())), preferred_element_type=F32)


def _split(x, pieces):
    out = []
    for _ in range(pieces - 1):
        hi = x.astype(BF16)
        out.append(hi)
        x = x - hi.astype(F32)
    out.append(x.astype(BF16))
    return out


def _dot_exact_lhs(a_bf, b, pieces=3):
    return sum(_dot(a_bf, p) for p in _split(b, pieces))


def _group_sumsq(x, gmat):
    return _dot((x * x).astype(BF16), gmat)


def _group_matrix(width):
    r = lax.broadcasted_iota(jnp.int32, (width, width), 0) // HEAD_DIM
    c = lax.broadcasted_iota(jnp.int32, (width, width), 1) // HEAD_DIM
    return jnp.where(r == c, 1.0, 0.0).astype(BF16)


def _ffn_kernel(has_mix, cf, *refs):
    if has_mix:
        x_ref, og_ref, od_ref, wo_ref, nw_ref, wgu_ref, wd_ref, o_ref, h_s, a_s = refs
    else:
        x_ref, nw_ref, wgu_ref, wd_ref, o_ref, h_s, a_s = refs
    d_ff = wd_ref.shape[0]
    x = x_ref[...]
    if has_mix:
        half = og_ref.shape[-1]
        x = x + _dot(og_ref[...], wo_ref[0:half, :]) + _dot(od_ref[...], wo_ref[half:, :])
    ms = jnp.mean(x * x, axis=-1, keepdims=True)
    h_s[...] = (x * lax.rsqrt(ms + RMS_EPS) * nw_ref[...]).astype(BF16)
    for c0 in range(0, d_ff, cf):
        h = h_s[...]
        g = _dot(h, wgu_ref[:, c0:c0 + cf])
        u = _dot(h, wgu_ref[:, d_ff + c0:d_ff + c0 + cf])
        a_s[:, c0:c0 + cf] = ((g * jax.nn.sigmoid(g)) * u).astype(BF16)
    o_ref[...] = x + 0.5 * _dot(a_s[...], wd_ref[...])


def _ffn_tiles(m, d_ff):
    tm = 512 if m % 512 == 0 else m
    cf = 2 * LANES if d_ff % (2 * LANES) == 0 else d_ff
    return tm, cf


def _ffn(x, norm_w, w_gu, w_down, layer, mix=None):
    m, d = x.shape
    d_ff = w_down.shape[1]
    tm, cf = _ffn_tiles(m, d_ff)
    row = lambda i: (i, 0)
    lay3 = lambda i: (layer, 0, 0)
    once = pl.Buffered(1)
    in_specs = [pl.BlockSpec((tm, d), row)]
    args = [x]
    if mix is not None:
        og, od, w_out = mix
        in_specs += [pl.BlockSpec((tm, og.shape[1]), row), pl.BlockSpec((tm, od.shape[1]), row),
                     pl.BlockSpec((None,) + w_out.shape[1:], lay3, pipeline_mode=once)]
        args += [og, od, w_out]
    in_specs += [
        pl.BlockSpec((None, 1, d), lay3),
        pl.BlockSpec((None, d, 2 * d_ff), lay3, pipeline_mode=once),
        pl.BlockSpec((None, d_ff, d), lay3, pipeline_mode=once),
    ]
    args += [norm_w, w_gu, w_down]
    return pl.pallas_call(
        functools.partial(_ffn_kernel, mix is not None, cf),
        grid=(m // tm,),
        in_specs=in_specs,
        out_specs=pl.BlockSpec((tm, d), row),
        out_shape=jax.ShapeDtypeStruct((m, d), F32),
        scratch_shapes=[pltpu.VMEM((tm, d), BF16), pltpu.VMEM((tm, d_ff), BF16)],
        compiler_params=pltpu.CompilerParams(
            dimension_semantics=("parallel",), vmem_limit_bytes=VMEM_LIMIT),
        name="ffn_mix" if mix is not None else "ffn",
    )(*args)


def _inproj_kernel(widths, conv_tiles, *refs):
    if conv_tiles is None:
        (x_ref, nw_ref, w_ref, qg_ref, kg_ref, _, _,
         qkv_ref, gate_ref, ba_ref, q_ref, k_ref, v_ref, kb_ref, vb_ref) = refs
    else:
        (x_ref, nw_ref, w_ref, qg_ref, kg_ref, _, _, cbuf_ref, cw_ref,
         qkv_ref, gate_ref, ba_ref, q_ref, k_ref, v_ref, kb_ref, vb_ref, cnew_ref, xp_s) = refs
    c_qkv, c_gate, c_d, c_ba = widths
    x = x_ref[...]
    ms = jnp.mean(x * x, axis=-1, keepdims=True)
    h = (x * lax.rsqrt(ms + RMS_EPS) * nw_ref[...]).astype(BF16)
    o = 0
    if conv_tiles is None:
        qkv_ref[...] = _dot(h, w_ref[:, o:o + c_qkv])
    else:
        tm = x.shape[0]
        n_tail = cbuf_ref.shape[0]
        top = 8 - n_tail
        tile = pl.program_id(0) % conv_tiles

        @pl.when(tile == 0)
        def _():
            xp_s[top:8, :] = cbuf_ref[...]

        xp_s[8:8 + tm, :] = _dot(h, w_ref[:, o:o + c_qkv])
        cw = cw_ref[...]
        y = xp_s[top:top + tm, :] * cw[0:1, :]
        for i in range(1, n_tail + 1):
            y = y + xp_s[top + i:top + i + tm, :] * cw[i:i + 1, :]
        qkv_ref[...] = y * jax.nn.sigmoid(y)
        tail = xp_s[8 + tm - n_tail:8 + tm, :]
        xp_s[top:8, :] = tail

        @pl.when(tile == conv_tiles - 1)
        def _():
            cnew_ref[...] = tail
    o += c_qkv
    gate_ref[...] = _dot(h, w_ref[:, o:o + c_gate]); o += c_gate
    dq = _dot(h, w_ref[:, o:o + c_d]); o += c_d
    dk = _dot(h, w_ref[:, o:o + c_d]); o += c_d
    dv = _dot(h, w_ref[:, o:o + c_d]); o += c_d
    ba_ref[...] = _dot(h, w_ref[:, o:o + c_ba])
    gmat = _group_matrix(c_d)
    inv_hd = 1.0 / HEAD_DIM
    qn = dq * lax.rsqrt(_group_sumsq(dq, gmat) * inv_hd + RMS_EPS) * qg_ref[...]
    kn = dk * lax.rsqrt(_group_sumsq(dk, gmat) * inv_hd + RMS_EPS) * kg_ref[...]
    q_ref[...] = qn * (HEAD_DIM ** -0.5 * LOG2E)
    tm = kn.shape[0]
    dv_w = 2 * HEAD_DIM
    n_heads = c_d // dv_w
    for h in range(n_heads):
        k_ref[pl.ds(h, tm, stride=n_heads), :] = kn[:, h * dv_w:(h + 1) * dv_w]
        v_ref[pl.ds(h, tm, stride=n_heads), :] = dv[:, h * dv_w:(h + 1) * dv_w]
    kb_ref[...] = kn.astype(BF16)
    vb_ref[...] = dv.astype(BF16)


def _inproj_conv_ok(m, seq_len, n_tail):
    tm = 512 if m % 512 == 0 else m
    return seq_len % tm == 0 and 0 < n_tail <= 8


def _inproj(x, norm_w, w_in, q_gain, k_gain, widths, layer, k_all, v_all, conv=None):
    m, d = x.shape
    c_qkv, c_gate, c_d, c_ba = widths
    dv_w = 2 * HEAD_DIM
    n_heads = c_d // dv_w
    tm = 512 if m % 512 == 0 else m
    row = lambda i: (i, 0)
    lay3 = lambda i: (layer, 0, 0)
    slab = pl.BlockSpec((None, tm * n_heads, dv_w), lambda i: (layer, i, 0))
    outs = [(c_qkv, F32), (c_gate, F32), (c_ba, F32), (c_d, F32), None, None, (c_d, BF16), (c_d, BF16)]
    in_specs = [pl.BlockSpec((tm, d), row),
                pl.BlockSpec((None, 1, d), lay3),
                pl.BlockSpec((None,) + w_in.shape[1:], lay3),
                pl.BlockSpec((None, 1, c_d), lay3),
                pl.BlockSpec((None, 1, c_d), lay3),
                pl.BlockSpec(memory_space=pl.ANY),
                pl.BlockSpec(memory_space=pl.ANY)]
    out_specs = [slab if o is None else pl.BlockSpec((tm, o[0]), row) for o in outs]
    out_shape = [jax.ShapeDtypeStruct(k_all.shape, F32) if o is None else jax.ShapeDtypeStruct((m, o[0]), o[1])
                 for o in outs]
    args = [x, norm_w, w_in, q_gain, k_gain, k_all, v_all]
    conv_tiles, scratch = None, []
    if conv is not None:
        conv_buf, conv_w, seq_len = conv
        assert _inproj_conv_ok(m, seq_len, conv_buf.shape[1])
        conv_tiles = seq_len // tm
        state = pl.BlockSpec((None,) + conv_buf.shape[1:], lambda i: (i // conv_tiles, 0, 0))
        in_specs += [state, pl.BlockSpec((None,) + conv_w.shape[1:], lay3)]
        out_specs.append(state)
        out_shape.append(jax.ShapeDtypeStruct(conv_buf.shape, F32))
        args += [conv_buf, conv_w]
        scratch = [pltpu.VMEM((8 + tm, c_qkv), F32)]
    return pl.pallas_call(
        functools.partial(_inproj_kernel, widths, conv_tiles),
        grid=(m // tm,),
        in_specs=in_specs,
        out_specs=out_specs,
        out_shape=out_shape,
        scratch_shapes=scratch,
        input_output_aliases={5: 4, 6: 5},
        compiler_params=pltpu.CompilerParams(
            dimension_semantics=("arbitrary" if conv is not None else "parallel",), vmem_limit_bytes=VMEM_LIMIT),
        name="inproj",
    )(*args)


def _block_mask(rows, cols, row_group, col_group):
    r = lax.broadcasted_iota(jnp.int32, (rows, cols), 0) // row_group
    c = lax.broadcasted_iota(jnp.int32, (rows, cols), 1) // col_group
    return r == c


def _block_diag(x, mask):
    reps = mask.shape[0] // x.shape[0]
    return jnp.where(mask, jnp.concatenate([x] * reps, axis=0), 0.0).astype(BF16)


def _dot_exact_rhs(a, b_bf, pieces=3):
    return sum(_dot(p, b_bf) for p in _split(a, pieces))


def _gdn_kernel(n_heads, chunk, gh, nc, conv_done, qkv_ref, gate_ref, ba_ref, cbuf_ref, s0_ref, cw_ref, alog_ref,
                dtb_ref, gain_ref, o_ref, snew_ref, *rest):
    if conv_done:
        (s_s,) = rest
        cnew_ref = xp_s = None
    else:
        cnew_ref, xp_s, s_s = rest
    bb = qkv_ref.shape[0]
    c = chunk
    tb = nc * c
    d = HEAD_DIM
    width = n_heads * d
    n_grp = n_heads // gh
    tw = gh * c
    dw = gh * d
    n_tail = cbuf_ref.shape[1]
    n_taps = n_tail + 1
    top = 8 - n_tail
    t = pl.program_id(1)
    n_t = pl.num_programs(1)

    @pl.when(t == 0)
    def _():
        if not conv_done:
            xp_s[:, top:8, :] = cbuf_ref[...]
        s_s[...] = jnp.zeros_like(s_s)
        for b in range(bb):
            for h in range(n_heads):
                o = (h % gh) * d
                s_s[b, h // gh, o:o + d, o:o + d] = s0_ref[b, h]

    if not conv_done:
        xp_s[:, 8:8 + tb, :] = qkv_ref[...]

    cw = cw_ref[...]
    gmat = _group_matrix(width)
    ri = lax.broadcasted_iota(jnp.int32, (c, c), 0)
    ci = lax.broadcasted_iota(jnp.int32, (c, c), 1)
    tril_bf = jnp.where(ri >= ci, 1.0, 0.0).astype(BF16)
    rb = lax.broadcasted_iota(jnp.int32, (tb, tb), 0)
    cb = lax.broadcasted_iota(jnp.int32, (tb, tb), 1)
    tril_chunks = jnp.where((rb >= cb) & (rb // c == cb // c), 1.0, 0.0).astype(BF16)
    row_t = lax.broadcasted_iota(jnp.int32, (c, tw), 0)
    col_t = lax.broadcasted_iota(jnp.int32, (c, tw), 1) % c
    causal = row_t >= col_t
    strict = row_t > col_t
    eye = jnp.where(row_t == col_t, 1.0, 0.0).astype(F32)
    bd_tt = _block_mask(tw, tw, c, c)
    bd_td = _block_mask(tw, dw, c, d)
    bd_dd = _block_mask(dw, dw, d, d)
    gmat_g = jnp.where(bd_dd, 1.0, 0.0).astype(BF16)
    expand_d = jnp.where(_block_mask(LANES, width, 1, d), 1.0, 0.0).astype(BF16)
    expand_t = jnp.where(_block_mask(LANES, n_heads * c, 1, c), 1.0, 0.0).astype(BF16)
    neg_a = -jnp.exp(alog_ref[...])
    dtb = dtb_ref[...]
    gain = gain_ref[...]

    chains = []
    for b in range(bb):
        if conv_done:
            qkv = qkv_ref[b]
        else:
            y = xp_s[b, top:top + tb, :] * cw[0:1, :]
            for i in range(1, n_taps):
                y = y + xp_s[b, top + i:top + i + tb, :] * cw[i:i + 1, :]
            qkv = y * jax.nn.sigmoid(y)
        q_all = qkv[:, 0:width]
        k_all = qkv[:, width:2 * width]
        v_all = qkv[:, 2 * width:3 * width]
        q_all = q_all * lax.rsqrt(_group_sumsq(q_all, gmat) + L2_EPS) * (HEAD_DIM ** -0.5)
        k_all = k_all * lax.rsqrt(_group_sumsq(k_all, gmat) + L2_EPS)
        ba = ba_ref[b]
        beta_all = jax.nn.sigmoid(ba[:, 0:LANES])
        z = ba[:, LANES:2 * LANES] + dtb
        softplus = jnp.maximum(z, 0.0) + jnp.log1p(jnp.exp(-jnp.abs(z)))
        g_all = neg_a * softplus
        gc_all = _dot_exact_lhs(tril_chunks, g_all, 2)
        gate_all = gate_ref[b]
        gate_all = gate_all * jax.nn.sigmoid(gate_all)
        beta_e = _dot_exact_rhs(beta_all, expand_d, 1)
        gc_e = _dot_exact_rhs(gc_all, expand_d, 2)
        g_t = _dot_exact_rhs(g_all, expand_t, 2)
        egc_e = jnp.exp(gc_e)

        for cc in range(nc):
            rows = slice(cc * c, (cc + 1) * c)
            glast = gc_e[(cc + 1) * c - 1:(cc + 1) * c, :]
            kend_e = jnp.exp(glast - gc_e[rows])
            sdec_e = jnp.exp(glast)
            for gi in range(n_grp):
                sl = slice(gi * dw, (gi + 1) * dw)
                k = k_all[rows, sl]
                beta = beta_e[rows, sl]
                chains.append(dict(b=b, cc=cc, gi=gi, rows=rows, sl=sl, q=q_all[rows, sl], k=k, kb=k * beta,
                                   vb=v_all[rows, sl] * beta, egc=egc_e[rows, sl],
                                   g_t=g_t[rows, gi * tw:(gi + 1) * tw], kend=kend_e[:, sl], sdec=sdec_e[:, sl],
                                   gate=gate_all[rows, sl]))

    for ch in chains:
        dm = _dot_exact_lhs(tril_bf, jnp.where(strict, ch["g_t"], 0.0), 2)
        ch["decay"] = jnp.where(causal, jnp.exp(dm), 0.0)
        ch["k_bd"] = _block_diag(ch["k"], bd_td)
    for ch in chains:
        ch["a"] = jnp.where(strict, _dot_nt(ch["kb"].astype(BF16), ch["k_bd"]) * ch["decay"], 0.0)
        ch["tinv"] = eye - jnp.where(row_t // 2 == col_t // 2, ch["a"], 0.0)
    size = 2
    while size < c:
        lower_left = ((row_t // (2 * size) == col_t // (2 * size))
                      & (row_t % (2 * size) >= size) & (col_t % (2 * size) < size))
        for ch in chains:
            a21 = _block_diag(jnp.where(lower_left, ch["a"], 0.0), bd_tt)
            ch["t2_a21"] = _dot(ch["tinv"].astype(BF16), a21)
        for ch in chains:
            ch["tinv"] = ch["tinv"] - _dot(ch["t2_a21"].astype(BF16), _block_diag(ch["tinv"], bd_tt))
        size *= 2
    for ch in chains:
        tinv_bf = ch["tinv"].astype(BF16)
        ch["u"] = _dot(tinv_bf, _block_diag(ch["vb"], bd_td))
        ch["w"] = _dot(tinv_bf, _block_diag(ch["kb"] * ch["egc"], bd_td))
        ch["qk"] = jnp.where(causal, _dot_nt(ch["q"].astype(BF16), ch["k_bd"]) * ch["decay"], 0.0)
    for cc in range(nc):
        now = [ch for ch in chains if ch["cc"] == cc]
        for ch in now:
            ch["s"] = s_s[ch["b"], ch["gi"]]
            lhs = jnp.concatenate([ch["w"], ch["q"] * ch["egc"]], axis=0).astype(BF16)
            ch["ws"] = _dot(lhs, ch["s"].astype(BF16))
        for ch in now:
            v_new = ch["u"] - ch["ws"][0:c]
            ch["o"] = ch["ws"][c:2 * c] + _dot(ch["qk"].astype(BF16), _block_diag(v_new, bd_td))
            k_end = ch["k"] * ch["kend"]
            cross = _dot_tn(k_end.astype(BF16), v_new.astype(BF16))
            s_s[ch["b"], ch["gi"]] = ch["s"] * ch["sdec"] + jnp.where(bd_dd, cross, 0.0)
    for ch in chains:
        o = ch["o"]
        ms = _group_sumsq(o, gmat_g) * (1.0 / d)
        on = o * lax.rsqrt(ms + RMS_EPS) * gain[:, ch["sl"]]
        o_ref[ch["b"], ch["rows"], ch["sl"]] = (on * ch["gate"]).astype(o_ref.dtype)

    if not conv_done:
        tail = xp_s[:, 8 + tb - n_tail:8 + tb, :]
        xp_s[:, top:8, :] = tail

    @pl.when(t == n_t - 1)
    def _():
        if not conv_done:
            cnew_ref[...] = tail
        for b in range(bb):
            for h in range(n_heads):
                o = (h % gh) * d
                snew_ref[b, h] = s_s[b, h // gh, o:o + d, o:o + d]


def _gdn(qkv, gate, ba, conv_buf, s0, conv_w, a_log, dt_bias, out_gain, layer, batch_block, conv_done=False):
    b, l, w3 = qkv.shape
    width = w3 // 3
    n_heads = width // HEAD_DIM
    chunk = min(GDN_CHUNK, l)
    assert l % chunk == 0 and chunk % 8 == 0 and chunk & (chunk - 1) == 0 and b % batch_block == 0
    n_tail = conv_buf.shape[1]
    assert n_tail <= min(8, chunk)
    bb = batch_block
    gh = min(n_heads, max(1, MXU_DIM // chunk))
    assert n_heads % gh == 0
    nc = next(n for n in (4, 2, 1) if (l // chunk) % n == 0)
    tb = nc * chunk
    blk = lambda i, t: (i, t, 0)
    fix3 = lambda i, t: (i, 0, 0)
    lay3 = lambda i, t: (layer, 0, 0)
    n_out = 2 if conv_done else 3
    return pl.pallas_call(
        functools.partial(_gdn_kernel, n_heads, chunk, gh, nc, conv_done),
        grid=(b // bb, l // tb),
        in_specs=[pl.BlockSpec((bb, tb, w3), blk),
                  pl.BlockSpec((bb, tb, width), blk),
                  pl.BlockSpec((bb, tb, ba.shape[2]), blk),
                  pl.BlockSpec((bb, n_tail, w3), fix3),
                  pl.BlockSpec((bb, n_heads, HEAD_DIM, HEAD_DIM), lambda i, t: (i, 0, 0, 0)),
                  pl.BlockSpec((None,) + conv_w.shape[1:], lay3),
                  pl.BlockSpec((None, 1, LANES), lay3),
                  pl.BlockSpec((None, 1, LANES), lay3),
                  pl.BlockSpec((None, 1, width), lay3)],
        out_specs=[pl.BlockSpec((bb, tb, width), blk),
                   pl.BlockSpec((bb, n_heads, HEAD_DIM, HEAD_DIM), lambda i, t: (i, 0, 0, 0)),
                   pl.BlockSpec((bb, n_tail, w3), fix3)][:n_out],
        out_shape=[jax.ShapeDtypeStruct((b, l, width), BF16),
                   jax.ShapeDtypeStruct(s0.shape, F32),
                   jax.ShapeDtypeStruct(conv_buf.shape, F32)][:n_out],
        scratch_shapes=([] if conv_done else [pltpu.VMEM((bb, 8 + tb, w3), F32)])
                       + [pltpu.VMEM((bb, n_heads // gh, gh * HEAD_DIM, gh * HEAD_DIM), F32)],
        compiler_params=pltpu.CompilerParams(
            dimension_semantics=("parallel", "arbitrary"), vmem_limit_bytes=VMEM_LIMIT),
        name="gdn",
    )(qkv, gate, ba, conv_buf, s0, conv_w, a_log, dt_bias, out_gain)


def _rel_bucket(rel, n_buckets):
    n = jnp.maximum(rel, 0)
    max_exact = n_buckets // 2
    nf = jnp.maximum(n, 1).astype(F32)
    large = max_exact + (jnp.log(nf / max_exact) / math.log(MAX_DISTANCE / max_exact)
                         * (n_buckets - max_exact)).astype(jnp.int32)
    large = jnp.minimum(large, n_buckets - 1)
    return jnp.where(n < max_exact, n, large)


def _bias_tile(rel_bias, rel):
    n_buckets, n_heads = rel_bias.shape
    tab = rel_bias.astype(F32) - rel_bias[n_buckets - 1].astype(F32)[None, :]
    bucket = _rel_bucket(rel, n_buckets)[None]
    bias = jnp.zeros((n_heads,) + rel.shape, F32)
    for n in range(n_buckets):
        bias = jnp.where(bucket == n, tab[n].reshape((n_heads,) + (1,) * rel.ndim), bias)
    return jnp.where((rel >= 0)[None], bias * LOG2E, -jnp.inf)


def _lambda(lp_ref, li_ref):
    lp = lp_ref[...]
    e1 = jnp.exp(jnp.sum(lp[0:1, :] * lp[1:2, :], axis=-1, keepdims=True))
    e2 = jnp.exp(jnp.sum(lp[2:3, :] * lp[3:4, :], axis=-1, keepdims=True))
    return e1 - e2 + li_ref[:, 0:1]


def _attn_kernel(tk, sub, dec, *refs):
    if dec is None:
        q_ref, k_ref, v_ref, bias_ref, lp_ref, li_ref, gain_ref, o_ref, q_s, m_s, acc_s, vx_s, s0_s, s1_s = refs
    else:
        n_pp, n_dec_heads, n_groups = dec
        q_ref, k_ref, v_ref, bias_ref, lp_ref, li_ref, gain_ref, dq_ref, kn_ref, vn_ref, dbias_ref = refs[1:12]
        k_pages = refs[12:12 + n_pp]
        v_pages = refs[12 + n_pp:12 + 2 * n_pp]
        o_ref, od_ref, q_s, m_s, acc_s, vx_s, s0_s, s1_s = refs[12 + 2 * n_pp:20 + 2 * n_pp]
        step_id = ((pl.program_id(0) * pl.num_programs(1) + pl.program_id(1)) * pl.num_programs(2)
                   + pl.program_id(2))

        def decode(phase):
            _decode_step(n_dec_heads, step_id % n_groups, n_groups, dq_ref, kn_ref, vn_ref, dbias_ref, lp_ref,
                         li_ref, gain_ref, k_pages, v_pages, od_ref, *refs[20 + 2 * n_pp:], phase=phase)
    tq = q_ref.shape[0]
    dv = 2 * HEAD_DIM
    qi = pl.program_id(2)

    @pl.when(qi == 0)
    def _():
        vx_s[:, 0:dv] = v_ref[...]
        vx_s[:, dv:2 * dv] = jnp.ones((vx_s.shape[0], dv), BF16)

    if dec is not None:
        decode("main")
    q = q_ref[...]
    lane = lax.broadcasted_iota(jnp.int32, q.shape, 1)
    q_s[0:tq, :] = jnp.where(lane < HEAD_DIM, q, 0.0).astype(BF16)
    q_s[tq:2 * tq, :] = jnp.where(lane >= HEAD_DIM, q, 0.0).astype(BF16)
    m_s[...] = jnp.full_like(m_s, NEG_BIG)
    acc_s[...] = jnp.zeros_like(acc_s)

    def scores(buf, j):
        ks = pl.multiple_of(j * tk, tk)
        buf[...] = _dot_nt(q_s[...], k_ref[pl.ds(ks, tk), :])

    def update(buf, j):
        ks = pl.multiple_of(j * tk, tk)
        m_prev = m_s[...]
        m_new = jnp.maximum(m_prev, jnp.max(buf[...], axis=-1, keepdims=True))
        p = jnp.exp2(buf[...] - jnp.concatenate([m_new] * (tk // LANES), axis=1)).astype(BF16)
        alpha = jnp.exp2(m_prev - m_new)
        pv = _dot(p, vx_s[pl.ds(ks, tk), :])
        acc_s[...] = jnp.concatenate([alpha] * (2 * dv // LANES), axis=1) * acc_s[...] + pv
        m_s[...] = m_new

    def below_diagonal_bias(buf):
        for m in range(2):
            buf[m * tq:m * tq + sub, tk - sub:tk] += bias_ref[1]

    def diagonal_bias(buf):
        for m in range(2):
            for r in range(tq // sub):
                rows = slice(m * tq + r * sub, m * tq + (r + 1) * sub)
                if r >= 1:
                    buf[rows, (r - 1) * sub:r * sub] += bias_ref[1]
                buf[rows, r * sub:(r + 1) * sub] += bias_ref[0]
                if (r + 1) * sub < tk:
                    buf[rows, (r + 1) * sub:tk] = jnp.full((sub, tk - (r + 1) * sub), -jnp.inf, F32)

    n_pairs = jnp.maximum(qi - 1, 0) // 2
    scores(s0_s, 0)
    if dec is not None:
        decode("tail")

    def pair(i, carry):
        scores(s1_s, 2 * i + 1)
        update(s0_s, 2 * i)
        scores(s0_s, 2 * i + 2)
        update(s1_s, 2 * i + 1)
        return carry

    lax.fori_loop(0, n_pairs, pair, 0)

    @pl.when(qi == 0)
    def _():
        diagonal_bias(s0_s)
        update(s0_s, 0)

    @pl.when(qi % 2 == 1)
    def _():
        scores(s1_s, qi)
        below_diagonal_bias(s0_s)
        diagonal_bias(s1_s)
        update(s0_s, qi - 1)
        update(s1_s, qi)

    @pl.when((qi % 2 == 0) & (qi >= 2))
    def _():
        scores(s1_s, qi - 1)
        below_diagonal_bias(s1_s)
        update(s0_s, qi - 2)
        scores(s0_s, qi)
        diagonal_bias(s0_s)
        update(s1_s, qi - 1)
        update(s0_s, qi)

    lam = _lambda(lp_ref, li_ref)
    acc = acc_s[...]
    on = acc[:, 0:dv] * (1.0 / acc[:, dv:2 * dv])
    o = on[0:tq] - lam * on[tq:2 * tq]
    ms = jnp.mean(o * o, axis=-1, keepdims=True)
    o_ref[...] = (o * lax.rsqrt(ms + RMS_EPS) * gain_ref[...] * (1.0 - li_ref[:, 0:1])).astype(o_ref.dtype)


def _attn_tile(l):
    return 512 if l % 512 == 0 else l


def _decode_pages_per_attn_step(attn_steps, n_seq, n_pages):
    total = n_seq * n_pages
    if total % attn_steps:
        return None
    n_pp = total // attn_steps
    return n_pp if (0 < n_pp <= 16 and n_pages % n_pp == 0) else None


def _attn(q, k_bf, v_bf, bias, lam_params, lam_init, out_gain, layer, dec=None):
    b, l, width = q.shape
    dv = 2 * HEAD_DIM
    assert dv == LANES
    n_heads = width // dv
    tq = tk = _attn_tile(l)
    nq = l // tq
    sub = bias.shape[-1]
    lay3 = lambda bi, h, qi, *_: (layer, 0, 0)
    in_specs = [pl.BlockSpec((None, tq, dv), lambda bi, h, qi, *_: (bi, qi, h)),
                pl.BlockSpec((None, l, dv), lambda bi, h, qi, *_: (bi, 0, h)),
                pl.BlockSpec((None, l, dv), lambda bi, h, qi, *_: (bi, 0, h)),
                pl.BlockSpec((None, 2, sub, sub), lambda bi, h, qi, *_: (h, 0, 0, 0)),
                pl.BlockSpec((None,) + lam_params.shape[1:], lay3),
                pl.BlockSpec((None, 1, LANES), lay3),
                pl.BlockSpec((None, 1, dv), lay3)]
    out_specs = pl.BlockSpec((None, tq, dv), lambda bi, h, qi, *_: (bi, qi, h))
    out_shape = jax.ShapeDtypeStruct((b, l, width), BF16)
    scratch = [pltpu.VMEM((2 * tq, dv), BF16),
               pltpu.VMEM((2 * tq, LANES), F32),
               pltpu.VMEM((2 * tq, 2 * dv), F32),
               pltpu.VMEM((l, 2 * dv), BF16),
               pltpu.VMEM((2 * tq, tk), F32),
               pltpu.VMEM((2 * tq, tk), F32)]
    args = [q, k_bf, v_bf, bias, lam_params, lam_init, out_gain]
    if dec is None:
        return pl.pallas_call(
            functools.partial(_attn_kernel, tk, sub, None),
            grid=(b, n_heads, nq), in_specs=in_specs, out_specs=out_specs, out_shape=out_shape,
            scratch_shapes=scratch,
            compiler_params=pltpu.CompilerParams(
                dimension_semantics=("parallel", "parallel", "arbitrary"), vmem_limit_bytes=VMEM_LIMIT),
            name="attn",
        )(*args)

    dq, k_new, v_new, cache_k, cache_v, page_table, dbias, n_pp = dec
    n_seq, t_new, dwidth = dq.shape
    dh = dwidth // dv
    page = cache_k.shape[2] // dh
    n_groups = page_table.shape[1] // n_pp
    assert b * n_heads * nq == n_seq * n_groups and t_new <= page
    rows = 2 * t_new

    def step_id(bi, h, qi):
        return (bi * n_heads + h) * nq + qi

    seq3 = lambda bi, h, qi, pt: (step_id(bi, h, qi) // n_groups, 0, 0)
    new3 = lambda bi, h, qi, pt: (layer, step_id(bi, h, qi) // n_groups, 0)

    def page_spec(i):
        def index(bi, h, qi, pt):
            sid = step_id(bi, h, qi)
            return (layer, pt[sid * n_pp + i], 0, 0)
        return pl.BlockSpec((None, None, page * dh, dv), index)

    in_specs += [pl.BlockSpec((None, t_new, dwidth), seq3),
                 pl.BlockSpec((None, t_new * dh, dv), new3),
                 pl.BlockSpec((None, t_new * dh, dv), new3),
                 pl.BlockSpec(dbias.shape, lambda bi, h, qi, pt: (0, 0, 0, 0))]
    in_specs += [page_spec(i) for i in range(n_pp)] * 2
    scratch += _decode_scratch(dh, t_new, page)
    grid_spec = pltpu.PrefetchScalarGridSpec(
        num_scalar_prefetch=1, grid=(b, n_heads, nq), in_specs=in_specs,
        out_specs=[out_specs, pl.BlockSpec((None, t_new, dwidth), seq3)], scratch_shapes=scratch)
    return pl.pallas_call(
        functools.partial(_attn_kernel, tk, sub, (n_pp, dh, n_groups)),
        grid_spec=grid_spec,
        out_shape=[out_shape, jax.ShapeDtypeStruct((n_seq, t_new, dwidth), BF16)],
        compiler_params=pltpu.CompilerParams(
            dimension_semantics=("arbitrary", "arbitrary", "arbitrary"), vmem_limit_bytes=VMEM_LIMIT),
        name="attn_decode",
    )(page_table.reshape(-1), *args, dq, k_new, v_new, dbias, *([cache_k] * n_pp), *([cache_v] * n_pp))


def _decode_scratch(n_heads, t_new, page):
    assert n_heads % 2 == 0
    n_pairs, cols, width = n_heads // 2, 4 * t_new, 4 * HEAD_DIM
    return [pltpu.VMEM((n_pairs, cols, width), F32),
            pltpu.VMEM((n_pairs, cols, LANES), F32),
            pltpu.VMEM((n_pairs, cols, LANES), F32),
            pltpu.VMEM((n_pairs, cols, width), F32),
            pltpu.VMEM((n_pairs, page, width), F32),
            pltpu.VMEM((n_pairs, page, width), F32)]


def _decode_kernel(n_pp, n_heads, pt_ref, q_ref, kn_ref, vn_ref, bias_ref, lp_ref, li_ref, gain_ref, *rest):
    del pt_ref
    _decode_step(n_heads, pl.program_id(1), pl.num_programs(1), q_ref, kn_ref, vn_ref, bias_ref, lp_ref, li_ref,
                 gain_ref, rest[:n_pp], rest[n_pp:2 * n_pp], *rest[2 * n_pp:])


def _decode_step(n_heads, j, n_j, q_ref, kn_ref, vn_ref, bias_ref, lp_ref, li_ref, gain_ref, k_refs, v_refs,
                 o_ref, q_s, m_s, l_s, acc_s, kpad_s, vpad_s, phase="all"):
    n_pp = len(k_refs)
    t_new = q_ref.shape[0]
    page = k_refs[0].shape[0] // n_heads
    dv = 2 * HEAD_DIM
    rows = 2 * t_new
    n_pairs = n_heads // 2
    cols = 2 * rows

    def start_sequence():
        m_s[...] = jnp.full_like(m_s, NEG_BIG)
        l_s[...] = jnp.zeros_like(l_s)
        acc_s[...] = jnp.zeros_like(acc_s)
        q_s[...] = jnp.zeros_like(q_s)
        kpad_s[...] = jnp.zeros_like(kpad_s)
        vpad_s[...] = jnp.zeros_like(vpad_s)
        lane = lax.broadcasted_iota(jnp.int32, (t_new, dv), 1)
        for h in range(n_heads):
            pr, side = h // 2, h % 2
            cs = slice(h * dv, (h + 1) * dv)
            ls = slice(side * dv, (side + 1) * dv)
            qh = q_ref[:, cs]
            q_s[pr, side * rows:side * rows + t_new, ls] = jnp.where(lane < HEAD_DIM, qh, 0.0)
            q_s[pr, side * rows + t_new:(side + 1) * rows, ls] = jnp.where(lane >= HEAD_DIM, qh, 0.0)
            kpad_s[pr, 0:t_new, ls] = kn_ref[pl.ds(h, t_new, stride=n_heads), :]
            vpad_s[pr, 0:t_new, ls] = vn_ref[pl.ds(h, t_new, stride=n_heads), :]

    if phase != "tail":
        pl.when(j == 0)(start_sequence)

    def pair_rows(ref, pr):
        return jnp.concatenate([ref[pl.ds(2 * pr, page, stride=n_heads), :],
                                ref[pl.ds(2 * pr + 1, page, stride=n_heads), :]], axis=1)

    def pair_step(plain_pages, with_tail):
        state = []
        for pr in range(n_pairs):
            pieces = []
            if plain_pages:
                k_cat = jnp.concatenate([pair_rows(k_refs[i], pr) for i in plain_pages], axis=0)
                pieces.append((jnp.transpose(_dot_nt(k_cat, q_s[pr])), plain_pages))
            if with_tail:
                k_cat = jnp.concatenate([pair_rows(k_refs[n_pp - 1], pr), kpad_s[pr]], axis=0)
                bias = jnp.concatenate([bias_ref[0, pr], bias_ref[1, pr]], axis=0)
                pieces.append((jnp.transpose(_dot_nt(k_cat, q_s[pr]) + bias), None))
            state.append(pieces)
        m_new = []
        for pr in range(n_pairs):
            m_cur = state[pr][0][0].max(axis=1, keepdims=True)
            for s, _ in state[pr][1:]:
                m_cur = jnp.maximum(m_cur, s.max(axis=1, keepdims=True))
            m_new.append(jnp.maximum(m_s[pr], m_cur))
        for pr in range(n_pairs):
            alpha = jnp.exp2(m_s[pr] - m_new[pr])
            psum = jnp.zeros((cols, 1), F32)
            pv = jnp.zeros((cols, 2 * dv), F32)
            for s, plain in state[pr]:
                p = jnp.exp2(s - m_new[pr][:, 0:1])
                psum = psum + jnp.sum(p, axis=1, keepdims=True)
                if plain is not None:
                    v_cat = jnp.concatenate([pair_rows(v_refs[i], pr) for i in plain], axis=0)
                else:
                    v_cat = jnp.concatenate([pair_rows(v_refs[n_pp - 1], pr), vpad_s[pr]], axis=0)
                pv = pv + _dot(p, v_cat)
            l_s[pr] = alpha * l_s[pr] + psum
            acc_s[pr] = jnp.concatenate([alpha] * (2 * dv // LANES), axis=1) * acc_s[pr] + pv
            m_s[pr] = m_new[pr]

    if phase != "tail" and n_pp > 1:
        pair_step(list(range(n_pp - 1)), False)
    if phase == "main":
        return

    @pl.when(j < n_j - 1)
    def _():
        pair_step([n_pp - 1], False)

    @pl.when(j == n_j - 1)
    def _():
        pair_step([], True)
        lam = _lambda(lp_ref, li_ref)
        scale = 1.0 - li_ref[:, 0:1]
        for h in range(n_heads):
            pr, side = h // 2, h % 2
            on = (acc_s[pr][side * rows:(side + 1) * rows, side * dv:(side + 1) * dv]
                  / l_s[pr][side * rows:(side + 1) * rows, :])
            o = on[0:t_new] - lam * on[t_new:rows]
            ms = jnp.mean(o * o, axis=-1, keepdims=True)
            o_ref[:, h * dv:(h + 1) * dv] = (o * lax.rsqrt(ms + RMS_EPS) * gain_ref[...] * scale).astype(o_ref.dtype)


def _decode(q, k_new, v_new, cache_k, cache_v, page_table, bias, lam_params, lam_init, out_gain, layer,
            pages_per_step):
    b, t_new, width = q.shape
    dv = 2 * HEAD_DIM
    n_heads = width // dv
    page = cache_k.shape[2] // n_heads
    n_pages = page_table.shape[1]
    n_pp = pages_per_step
    assert n_pages % n_pp == 0 and t_new <= page
    rows = 2 * t_new
    seq = lambda bi, j, pt: (bi, 0, 0)
    lay3 = lambda bi, j, pt: (layer, 0, 0)

    def page_spec(i):
        return pl.BlockSpec((None, None, page * n_heads, dv),
                            lambda bi, j, pt: (layer, pt[bi, j * n_pp + i], 0, 0))

    grid_spec = pltpu.PrefetchScalarGridSpec(
        num_scalar_prefetch=1,
        grid=(b, n_pages // n_pp),
        in_specs=[pl.BlockSpec((None, t_new, width), seq),
                  pl.BlockSpec((None, t_new * n_heads, dv), lambda bi, j, pt: (layer, bi, 0)),
                  pl.BlockSpec((None, t_new * n_heads, dv), lambda bi, j, pt: (layer, bi, 0)),
                  pl.BlockSpec(bias.shape, lambda bi, j, pt: (0, 0, 0, 0)),
                  pl.BlockSpec((None,) + lam_params.shape[1:], lay3),
                  pl.BlockSpec((None, 1, LANES), lay3),
                  pl.BlockSpec((None, 1, dv), lay3)]
                 + [page_spec(i) for i in range(n_pp)] * 2,
        out_specs=pl.BlockSpec((None, t_new, width), seq),
        scratch_shapes=_decode_scratch(n_heads, t_new, page))
    return pl.pallas_call(
        functools.partial(_decode_kernel, n_pp, n_heads),
        grid_spec=grid_spec,
        out_shape=jax.ShapeDtypeStruct((b, t_new, width), BF16),
        compiler_params=pltpu.CompilerParams(
            dimension_semantics=("parallel", "arbitrary"), vmem_limit_bytes=VMEM_LIMIT),
        name="decode",
    )(page_table, q, k_new, v_new, bias, lam_params, lam_init, out_gain,
      *([cache_k] * n_pp), *([cache_v] * n_pp))


def _pad_lanes(a, width=LANES):
    return jnp.pad(a, [(0, 0)] * (a.ndim - 1) + [(0, width - a.shape[-1])])


def kernel(x_prompt, x_sample, cache_k, cache_v, state_gdn, state_conv, page_table, ffn1_norm, ffn1_w_gate_up, ffn1_w_down, mix_norm, w_in, conv_w, gdn_a_log, gdn_dt_bias, gdn_out_norm, diff_q_norm, diff_k_norm, diff_lambda, diff_out_norm, rel_bias, w_out, ffn2_norm, ffn2_w_gate_up, ffn2_w_down):
    depth, d_model, _ = w_in.shape
    n_gdn = gdn_a_log.shape[1]
    n_diff = rel_bias.shape[1]
    gdn_w = n_gdn * HEAD_DIM
    diff_w = n_diff * 2 * HEAD_DIM
    conv_dim = conv_w.shape[2]
    assert conv_dim == 3 * gdn_w and w_out.shape[1] == gdn_w + diff_w
    page = cache_k.shape[2]
    assert page >= MAX_DISTANCE
    past_len = page_table.shape[1] * page

    o = 0
    cols = {}
    for name, wd in (("qkv", conv_dim), ("gate", gdn_w), ("b", n_gdn), ("a", n_gdn),
                     ("dq", diff_w), ("dk", diff_w), ("dv", diff_w)):
        cols[name] = w_in[:, :, o:o + wd]
        o += wd
    w_in_r = jnp.concatenate([cols["qkv"], cols["gate"], cols["dq"], cols["dk"], cols["dv"],
                              _pad_lanes(cols["b"]), _pad_lanes(cols["a"])], axis=-1).astype(BF16)
    widths = (conv_dim, gdn_w, diff_w, 2 * LANES)
    bf = lambda a: a.astype(BF16)
    w1_gu, w1_d, w2_gu, w2_d, w_out_bf = bf(ffn1_w_gate_up), bf(ffn1_w_down), bf(ffn2_w_gate_up), bf(ffn2_w_down), bf(w_out)
    row3 = lambda a: a.reshape(depth, 1, -1)
    n1, n2, nm = row3(ffn1_norm), row3(ffn2_norm), row3(mix_norm)
    q_gain = row3(jnp.tile(diff_q_norm, (1, diff_w // HEAD_DIM)))
    k_gain = row3(jnp.tile(diff_k_norm, (1, diff_w // HEAD_DIM)))
    a_log = row3(_pad_lanes(gdn_a_log))
    dt_bias = row3(_pad_lanes(gdn_dt_bias))
    gdn_gain = row3(jnp.tile(gdn_out_norm, (1, n_gdn)))
    diff_gain = row3(diff_out_norm)
    lam_init = jnp.asarray([0.8 - 0.6 * math.exp(-0.3 * l) for l in range(depth)], F32)
    lam_init = jnp.broadcast_to(lam_init[:, None, None], (depth, 1, LANES))
    cache_k2 = cache_k.reshape(cache_k.shape[:2] + (page * n_diff, 2 * HEAD_DIM))
    cache_v2 = cache_v.reshape(cache_v.shape[:2] + (page * n_diff, 2 * HEAD_DIM))

    def bias_prompt(l):
        t = _attn_tile(l)
        sub = min(t, MAX_DISTANCE)
        assert t % sub == 0 and (l == t or sub == MAX_DISTANCE)
        i = jnp.arange(sub, dtype=jnp.int32)
        rel = i[:, None] - i[None, :]
        return jnp.stack([_bias_tile(rel_bias, rel), _bias_tile(rel_bias, rel + sub)], axis=1)

    def bias_sample(t_new):
        tok = jnp.arange(t_new, dtype=jnp.int32)
        lane = jnp.arange(page, dtype=jnp.int32)
        rel_last = (page + tok)[:, None] - lane[None, :]
        rel_new = jnp.where(lane[None, :] < t_new, tok[:, None] - lane[None, :], -1)
        tiles = jnp.stack([_bias_tile(rel_bias, rel_last), _bias_tile(rel_bias, rel_new)], axis=0)
        tiles = jnp.broadcast_to(tiles[:, :, None], (2, n_diff, 2, t_new, page))
        return jnp.swapaxes(tiles.reshape(2, n_diff // 2, 4 * t_new, page), 2, 3)

    class Group:
        def __init__(self, x, paged):
            self.b, self.l, _ = x.shape
            self.m = self.b * self.l
            self.paged = paged
            self.x = x.reshape(self.m, d_model)
            self.k_all = jnp.zeros((depth, self.m * n_diff, 2 * HEAD_DIM), F32)
            self.v_all = jnp.zeros((depth, self.m * n_diff, 2 * HEAD_DIM), F32)
            self.states, self.convs = [], []
            self.bias = bias_sample(self.l) if paged else bias_prompt(self.l)

        def r3(self, a):
            return a.reshape(self.b, self.l, a.shape[-1])

        def before_attention(self, layer):
            b = self.b
            self.x = _ffn(self.x, n1, w1_gu, w1_d, layer)
            if self.paged:
                conv_buf, s0, bb = state_conv[layer], state_gdn[layer], 4 if b % 4 == 0 else 1
            else:
                conv_buf = jnp.zeros((b, conv_w.shape[1] - 1, conv_dim), F32)
                s0 = jnp.zeros((b, n_gdn, HEAD_DIM, HEAD_DIM), F32)
                bb = b
            conv_in_proj = _inproj_conv_ok(self.m, self.l, conv_buf.shape[1])
            outs = _inproj(self.x, nm, w_in_r, q_gain, k_gain, widths, layer, self.k_all, self.v_all,
                           conv=(conv_buf, conv_w, self.l) if conv_in_proj else None)
            qkv, gate, ba, self.q, self.k_all, self.v_all, self.k_bf, self.v_bf = outs[:8]
            gdn_out = _gdn(self.r3(qkv), self.r3(gate), self.r3(ba), conv_buf, s0, conv_w, a_log, dt_bias,
                           gdn_gain, layer, bb, conv_done=conv_in_proj)
            self.o_gdn, s_new = gdn_out[:2]
            self.states.append(s_new)
            self.convs.append(outs[8] if conv_in_proj else gdn_out[2])

        def after_attention(self, layer, o_diff):
            self.x = _ffn(self.x, n2, w2_gu, w2_d, layer,
                          mix=(self.o_gdn.reshape(self.m, gdn_w), o_diff.reshape(self.m, diff_w), w_out_bf))

        def outputs(self):
            kv_shape = (depth, self.b, self.l, n_diff, 2 * HEAD_DIM)
            return (self.x.reshape(self.b, self.l, d_model), self.k_all.reshape(kv_shape),
                    self.v_all.reshape(kv_shape), jnp.stack(self.states), jnp.stack(self.convs))

    prompt, sample = Group(x_prompt, False), Group(x_sample, True)
    n_pages = page_table.shape[1]
    attn_steps = prompt.b * n_diff * (prompt.l // _attn_tile(prompt.l))
    n_pp_fused = _decode_pages_per_attn_step(attn_steps, sample.b, n_pages)
    for layer in range(depth):
        sample.before_attention(layer)
        prompt.before_attention(layer)
        dec = (sample.r3(sample.q), sample.k_all, sample.v_all, cache_k2, cache_v2, page_table, sample.bias)
        attn_args = (prompt.r3(prompt.q), prompt.r3(prompt.k_bf), prompt.r3(prompt.v_bf), prompt.bias,
                     diff_lambda, lam_init, diff_gain, layer)
        if n_pp_fused is not None:
            o_p, o_s = _attn(*attn_args, dec=dec + (n_pp_fused,))
        else:
            o_p = _attn(*attn_args)
            o_s = _decode(*dec, diff_lambda, lam_init, diff_gain, layer,
                          next(n for n in (16, 8, 4, 2, 1) if n_pages % n == 0))
        prompt.after_attention(layer, o_p)
        sample.after_attention(layer, o_s)

    y_p, k_p, v_p, s_p, c_p = prompt.outputs()
    y_s, k_s, v_s, s_s, c_s = sample.outputs()
    return (y_p, y_s, k_p, v_p, s_p, c_p, k_s, v_s, s_s, c_s)
```
